```python
import math
import jax
import jax.numpy as jnp
from jax import lax
import numpy as np

D_MODEL = 1024
BATCH = 8
SEQ = 4096
DEPTH = 4

N_A_LAYERS = DEPTH // 2
N_B_LAYERS = DEPTH - N_A_LAYERS
RMS_EPS = 1e-5

GLA_HEADS = 4
GLA_KEY_DIM = D_MODEL // 2
GLA_VAL_DIM = D_MODEL
GLA_HK = GLA_KEY_DIM // GLA_HEADS
GLA_HV = GLA_VAL_DIM // GLA_HEADS
GLA_RANK = 16
GLA_TAU = 16.0
GLA_CHUNK = 64

NSA_HEADS = 16
NSA_GROUPS = 4
NSA_HPG = NSA_HEADS // NSA_GROUPS
NSA_HD = D_MODEL // NSA_HEADS
CMP_LEN = 32
CMP_STRIDE = 16
CMP_HIDDEN = 2 * NSA_HD
SLC_LEN = 64
SLC_TOPK = 16
WIN = 512
NSA_Q_BLOCK = 64

N_EXPERTS = 32
TOP_K = 4
EXPERT_FF = D_MODEL
SWIGLU_LIMIT = 7.0
SWIGLU_ALPHA = 1.702
MOE_BLOCK = 128

FORCE = 1e4
NEG = -1e30

kernel_name = 'yoco_gla_nsa_moe_adaln_trunk'


def rms_norm(x, g):
    xf = x.astype(jnp.float32)
    y = xf * lax.rsqrt(jnp.mean(xf * xf, axis=-1, keepdims=True) + RMS_EPS)
    return (y * g.astype(jnp.float32)).astype(x.dtype)


def modulate(h, shift, scale):
    return h * (1.0 + scale[:, None, :]) + shift[:, None, :]


def alibi_slopes(n):
    start = 2.0 ** (-8.0 / n)
    return jnp.asarray(start ** np.arange(1, n + 1), jnp.float32)


def masked_softmax(s, mask):
    s = jnp.where(mask, s, NEG)
    p = jax.nn.softmax(s, axis=-1)
    return jnp.where(mask, p, 0.0)


def gla_mixer(h, w_in, w_gate2, b_gate, norm_g, w_out):
    B, S, _ = h.shape
    nc = S // GLA_CHUNK
    proj = h @ w_in
    cuts = [GLA_KEY_DIM, 2 * GLA_KEY_DIM, 2 * GLA_KEY_DIM + GLA_VAL_DIM, 2 * GLA_KEY_DIM + 2 * GLA_VAL_DIM]
    q, k, v, r, g_lr = jnp.split(proj, cuts, axis=-1)
    log_a = jax.nn.log_sigmoid((g_lr @ w_gate2 + b_gate).astype(jnp.float32)) / GLA_TAU

    def to_chunks(t, hd):
        return t.astype(jnp.float32).reshape(B, nc, GLA_CHUNK, GLA_HEADS, hd).transpose(0, 3, 1, 2, 4)

    q = to_chunks(q, GLA_HK) * (GLA_HK ** -0.5)
    k = to_chunks(k, GLA_HK)
    v = to_chunks(v, GLA_HV)
    b = jnp.cumsum(to_chunks(log_a, GLA_HK), axis=3)
    b_last = b[:, :, :, -1:, :]
    q_dec = q * jnp.exp(b)
    k_intra = k * jnp.exp(-b)
    k_inter = k * jnp.exp(b_last - b)
    causal = jnp.tril(jnp.ones((GLA_CHUNK, GLA_CHUNK), bool))
    att = jnp.where(causal, jnp.einsum('bhncd,bhnsd->bhncs', q_dec, k_intra), 0.0)
    o_intra = jnp.einsum('bhncs,bhnse->bhnce', att, v)

    def step(state, xs):
        qd, ki, vv, dl = xs
        o = jnp.einsum('bhcd,bhde->bhce', qd, state)
        state = state * dl[..., None] + jnp.einsum('bhcd,bhce->bhde', ki, vv)
        return state, o

    decay_last = jnp.exp(b_last[:, :, :, 0, :])
    xs = (jnp.moveaxis(q_dec, 2, 0), jnp.moveaxis(k_inter, 2, 0), jnp.moveaxis(v, 2, 0), jnp.moveaxis(decay_last, 2, 0))
    state0 = jnp.zeros((B, GLA_HEADS, GLA_HK, GLA_HV), jnp.float32)
    _, o_inter = lax.scan(step, state0, xs)
    o = o_intra + jnp.moveaxis(o_inter, 0, 2)
    o = o * lax.rsqrt(jnp.mean(o * o, axis=-1, keepdims=True) + RMS_EPS)
    o = o.transpose(0, 2, 3, 1, 4).reshape(B, S, GLA_VAL_DIM) * norm_g.astype(jnp.float32)
    o = o.astype(h.dtype) * jax.nn.silu(r)
    return o @ w_out


def nsa_shared_kv(x, cs, kv_norm_g, kv_ada_w, kv_ada_b, kv_w, cmp_pos, cmp_w1, cmp_b1, cmp_w2, cmp_b2):
    B, S, _ = x.shape
    shift, scale = jnp.split(cs @ kv_ada_w + kv_ada_b, 2, axis=-1)
    h = modulate(rms_norm(x, kv_norm_g), shift, scale)
    kv = (h @ kv_w).reshape(B, S, 6, NSA_GROUPS, NSA_HD).transpose(2, 0, 3, 1, 4)
    n_cmp = (S - CMP_LEN) // CMP_STRIDE + 1
    idx = jnp.arange(n_cmp)[:, None] * CMP_STRIDE + jnp.arange(CMP_LEN)[None, :]
    blocks = kv[0:2][:, :, :, idx, :] + cmp_pos[:, None, None, None]
    flat = blocks.reshape(2, B, NSA_GROUPS, n_cmp, CMP_LEN * NSA_HD)
    hid = jax.nn.gelu(jnp.einsum('tbgnf,tfe->tbgne', flat, cmp_w1) + cmp_b1[:, None, None, None])
    comp = jnp.einsum('tbgne,ted->tbgnd', hid, cmp_w2) + cmp_b2[:, None, None, None]
    return (comp[0], comp[1], kv[2], kv[3], kv[4], kv[5])


def nsa_mixer(h, shared, w_in, b_gate, w_out):
    k_cmp, v_cmp, k_slc, v_slc, k_win, v_win = shared
    B, S, _ = h.shape
    proj = h @ w_in
    q = proj[..., :NSA_HEADS * NSA_HD].reshape(B, S, NSA_GROUPS, NSA_HPG, NSA_HD) * (NSA_HD ** -0.5)
    gates = jax.nn.sigmoid((proj[..., NSA_HEADS * NSA_HD:] + b_gate).astype(jnp.float32))
    gates = gates.reshape(B, S, 3, NSA_GROUPS, NSA_HPG)
    slopes = alibi_slopes(NSA_HEADS).reshape(NSA_GROUPS, NSA_HPG)[None, :, :, None, None]
    n_cmp = k_cmp.shape[2]
    n_slc = S // SLC_LEN
    topk = min(SLC_TOPK, n_slc)
    cmp_start = jnp.arange(n_cmp) * CMP_STRIDE
    cmp_end = cmp_start + CMP_LEN - 1
    slc_start = jnp.arange(n_slc) * SLC_LEN
    overlap = ((cmp_start[:, None] <= slc_start[None, :] + SLC_LEN - 1) & (cmp_end[:, None] >= slc_start[None, :])).astype(jnp.float32)
    kb = k_slc.reshape(B, NSA_GROUPS, n_slc, SLC_LEN, NSA_HD)
    vb = v_slc.reshape(B, NSA_GROUPS, n_slc, SLC_LEN, NSA_HD)
    pad = ((0, 0), (0, 0), (WIN, 0), (0, 0))
    k_win_p = jnp.pad(k_win, pad)
    v_win_p = jnp.pad(v_win, pad)
    b_ix = jnp.arange(B)[:, None, None, None]
    g_ix = jnp.arange(NSA_GROUPS)[None, :, None, None]
    blk_ids = jnp.arange(n_slc)

    def block(qi):
        q0 = qi * NSA_Q_BLOCK
        qb = lax.dynamic_slice_in_dim(q, q0, NSA_Q_BLOCK, axis=1)
        gb = lax.dynamic_slice_in_dim(gates, q0, NSA_Q_BLOCK, axis=1)
        t = q0 + jnp.arange(NSA_Q_BLOCK)
        dist_c = t[:, None] - cmp_end[None, :]
        s_c = jnp.einsum('bqghd,bgnd->bghqn', qb, k_cmp).astype(jnp.float32) - slopes * dist_c.astype(jnp.float32)
        p_c = masked_softmax(s_c, dist_c >= 0)
        o_c = jnp.einsum('bghqn,bgnd->bqghd', p_c, v_cmp)
        imp = jnp.einsum('bghqn,nj->bgqj', p_c, overlap)
        cur = t // SLC_LEN
        valid_b = blk_ids[None, :] <= cur[:, None]
        forced = (blk_ids[None, :] == 0) | (blk_ids[None, :] == cur[:, None]) | (blk_ids[None, :] == cur[:, None] - 1)
        score = jnp.where(valid_b, imp + jnp.where(forced, FORCE, 0.0), -FORCE)
        _, sel = lax.top_k(score, topk)
        k_sel = kb[b_ix, g_ix, sel].reshape(B, NSA_GROUPS, NSA_Q_BLOCK, topk * SLC_LEN, NSA_HD)
        v_sel = vb[b_ix, g_ix, sel].reshape(B, NSA_GROUPS, NSA_Q_BLOCK, topk * SLC_LEN, NSA_HD)
        key_pos = (sel[..., None] * SLC_LEN + jnp.arange(SLC_LEN)).reshape(B, NSA_GROUPS, NSA_Q_BLOCK, topk * SLC_LEN)
        dist_s = t[None, None, :, None] - key_pos
        s_s = jnp.einsum('bqghd,bgqkd->bghqk', qb, k_sel).astype(jnp.float32) - slopes * dist_s[:, :, None].astype(jnp.float32)
        p_s = masked_softmax(s_s, (dist_s >= 0)[:, :, None])
        o_s = jnp.einsum('bghqk,bgqkd->bqghd', p_s, v_sel)
        k_w = lax.dynamic_slice_in_dim(k_win_p, q0, NSA_Q_BLOCK + WIN, axis=2)
        v_w = lax.dynamic_slice_in_dim(v_win_p, q0, NSA_Q_BLOCK + WIN, axis=2)
        kw_pos = q0 - WIN + jnp.arange(NSA_Q_BLOCK + WIN)
        dist_w = t[:, None] - kw_pos[None, :]
        mask_w = (dist_w >= 0) & (dist_w < WIN) & (kw_pos[None, :] >= 0)
        s_w = jnp.einsum('bqghd,bgkd->bghqk', qb, k_w).astype(jnp.float32) - slopes * dist_w.astype(jnp.float32)
        p_w = masked_softmax(s_w, mask_w)
        o_w = jnp.einsum('bghqk,bgkd->bqghd', p_w, v_w)
        return gb[:, :, 0, :, :, None] * o_c + gb[:, :, 1, :, :, None] * o_s + gb[:, :, 2, :, :, None] * o_w

    outs = lax.map(block, jnp.arange(S // NSA_Q_BLOCK))
    o = jnp.moveaxis(outs, 0, 1).reshape(B, S, NSA_HEADS * NSA_HD).astype(h.dtype)
    return o @ w_out


def clamped_swiglu(gu):
    gate = jnp.minimum(gu[..., ::2], SWIGLU_LIMIT)
    up = jnp.clip(gu[..., 1::2], -SWIGLU_LIMIT, SWIGLU_LIMIT)
    return (up + 1.0) * gate * jax.nn.sigmoid(SWIGLU_ALPHA * gate)


def moe_ffn(h, router_w, router_b, w_gu, b_gu, w_down, b_down):
    B, S, D = h.shape
    n_tok = B * S
    xt = h.reshape(n_tok, D)
    logits = (xt @ router_w + router_b).astype(jnp.float32)
    top_val, top_idx = lax.top_k(logits, TOP_K)
    top_w = jax.nn.softmax(top_val, axis=-1)
    n_asg = n_tok * TOP_K
    e_flat = top_idx.reshape(n_asg)
    w_flat = top_w.reshape(n_asg)
    tok_flat = jnp.arange(n_asg, dtype=jnp.int32) // TOP_K
    order = jnp.argsort(e_flat)
    e_sorted = e_flat[order]
    counts = jnp.bincount(e_flat, length=N_EXPERTS)
    padded = (counts + MOE_BLOCK - 1) // MOE_BLOCK * MOE_BLOCK
    ends_padded = jnp.cumsum(padded)
    start_padded = ends_padded - padded
    start_sorted = jnp.cumsum(counts) - counts
    dest = start_padded[e_sorted] + jnp.arange(n_asg, dtype=jnp.int32) - start_sorted[e_sorted]
    cap = n_asg + N_EXPERTS * MOE_BLOCK
    n_blk = cap // MOE_BLOCK
    buf_tok = jnp.full((cap,), n_tok, jnp.int32).at[dest].set(tok_flat[order])
    buf_w = jnp.zeros((cap,), jnp.float32).at[dest].set(w_flat[order])
    blk_expert = jnp.minimum(jnp.searchsorted(ends_padded, jnp.arange(n_blk) * MOE_BLOCK, side='right'), N_EXPERTS - 1)
    x_pad = jnp.concatenate([xt, jnp.zeros((1, D), xt.dtype)], axis=0)

    def expert_block(args):
        tok, e = args
        gu = x_pad[tok] @ w_gu[e] + b_gu[e]
        return clamped_swiglu(gu) @ w_down[e] + b_down[e]

    ys = lax.map(expert_block, (buf_tok.reshape(n_blk, MOE_BLOCK), blk_expert))
    ys = ys.reshape(cap, D).astype(jnp.float32) * buf_w[:, None]
    out = jnp.zeros((n_tok + 1, D), jnp.float32).at[buf_tok].add(ys)[:n_tok]
    return out.reshape(B, S, D).astype(h.dtype)


def setup_inputs(seed: int = 0) -> dict:
    key = jax.random.key(seed)
    ks = list(jax.random.split(key, 32))

    def nrm(shape, scale):
        return jax.random.normal(ks.pop(), shape, jnp.float32) * scale

    D = D_MODEL
    gla_in = 2 * GLA_KEY_DIM + 2 * GLA_VAL_DIM + GLA_RANK
    kv_out = 6 * NSA_GROUPS * NSA_HD
    nsa_in = NSA_HEADS * NSA_HD + 3 * NSA_HEADS
    return {
        'x': nrm((BATCH, SEQ, D), 1.0),
        'c': nrm((BATCH, D), 1.0),
        'ada_w': nrm((DEPTH, D, 6 * D), 0.5 * D ** -0.5),
        'ada_b': nrm((DEPTH, 6 * D), 0.02),
        'norm1_g': 1.0 + nrm((DEPTH, D), 0.02),
        'norm2_g': 1.0 + nrm((DEPTH, D), 0.02),
        'gla_w_in': nrm((N_A_LAYERS, D, gla_in), D ** -0.5),
        'gla_w_gate2': nrm((N_A_LAYERS, GLA_RANK, GLA_KEY_DIM), GLA_RANK ** -0.5),
        'gla_b_gate': nrm((N_A_LAYERS, GLA_KEY_DIM), 0.1),
        'gla_norm_g': 1.0 + nrm((N_A_LAYERS, GLA_VAL_DIM), 0.02),
        'gla_w_out': nrm((N_A_LAYERS, GLA_VAL_DIM, D), GLA_VAL_DIM ** -0.5),
        'kv_norm_g': 1.0 + nrm((D,), 0.02),
        'kv_ada_w': nrm((D, 2 * D), 0.5 * D ** -0.5),
        'kv_ada_b': nrm((2 * D,), 0.02),
        'kv_w': nrm((D, kv_out), D ** -0.5),
        'cmp_pos': nrm((2, CMP_LEN, NSA_HD), 0.1),
        'cmp_w1': nrm((2, CMP_LEN * NSA_HD, CMP_HIDDEN), (CMP_LEN * NSA_HD) ** -0.5),
        'cmp_b1': nrm((2, CMP_HIDDEN), 0.02),
        'cmp_w2': nrm((2, CMP_HIDDEN, NSA_HD), CMP_HIDDEN ** -0.5),
        'cmp_b2': nrm((2, NSA_HD), 0.02),
        'nsa_w_in': nrm((N_B_LAYERS, D, nsa_in), D ** -0.5),
        'nsa_b_gate': nrm((N_B_LAYERS, 3 * NSA_HEADS), 0.1),
        'nsa_w_out': nrm((N_B_LAYERS, NSA_HEADS * NSA_HD, D), (NSA_HEADS * NSA_HD) ** -0.5),
        'router_w': nrm((DEPTH, D, N_EXPERTS), D ** -0.5),
        'router_b': nrm((DEPTH, N_EXPERTS), 0.01),
        'moe_w_gate_up': nrm((DEPTH, N_EXPERTS, D, 2 * EXPERT_FF), D ** -0.5),
        'moe_b_gate_up': nrm((DEPTH, N_EXPERTS, 2 * EXPERT_FF), 0.01),
        'moe_w_down': nrm((DEPTH, N_EXPERTS, EXPERT_FF, D), EXPERT_FF ** -0.5),
        'moe_b_down': nrm((DEPTH, N_EXPERTS, D), 0.01),
        'final_g': 1.0 + nrm((D,), 0.02),
    }


def reference(x, c, ada_w, ada_b, norm1_g, norm2_g, gla_w_in, gla_w_gate2, gla_b_gate, gla_norm_g, gla_w_out, kv_norm_g, kv_ada_w, kv_ada_b, kv_w, cmp_pos, cmp_w1, cmp_b1, cmp_w2, cmp_b2, nsa_w_in, nsa_b_gate, nsa_w_out, router_w, router_b, moe_w_gate_up, moe_b_gate_up, moe_w_down, moe_b_down, final_g):
    cs = jax.nn.silu(c)
    shared = None
    for layer in range(DEPTH):
        mod = cs @ ada_w[layer] + ada_b[layer]
        sh1, sc1, g1, sh2, sc2, g2 = jnp.split(mod, 6, axis=-1)
        h = modulate(rms_norm(x, norm1_g[layer]), sh1, sc1)
        if layer < N_A_LAYERS:
            i = layer
            y = gla_mixer(h, gla_w_in[i], gla_w_gate2[i], gla_b_gate[i], gla_norm_g[i], gla_w_out[i])
        else:
            i = layer - N_A_LAYERS
            y = nsa_mixer(h, shared, nsa_w_in[i], nsa_b_gate[i], nsa_w_out[i])
        x = x + g1[:, None, :] * y
        h = modulate(rms_norm(x, norm2_g[layer]), sh2, sc2)
        x = x + g2[:, None, :] * moe_ffn(h, router_w[layer], router_b[layer], moe_w_gate_up[layer], moe_b_gate_up[layer], moe_w_down[layer], moe_b_down[layer])
        if layer == N_A_LAYERS - 1:
            shared = nsa_shared_kv(x, cs, kv_norm_g, kv_ada_w, kv_ada_b, kv_w, cmp_pos, cmp_w1, cmp_b1, cmp_w2, cmp_b2)
    return rms_norm(x, final_g)
```

```python
import functools

import numpy as np
import jax
import jax.numpy as jnp
from jax import lax
from jax.experimental import pallas as pl
from jax.experimental.pallas import tpu as pltpu

F32 = jnp.float32
BF16 = jnp.bfloat16
HIGHEST = lax.Precision.HIGHEST

D_MODEL = 1024
DEPTH = 4
N_A_LAYERS = DEPTH // 2
RMS_EPS = 1e-5

GLA_HEADS = 4
GLA_KEY_DIM = D_MODEL // 2
GLA_VAL_DIM = D_MODEL
GLA_HK = GLA_KEY_DIM // GLA_HEADS
GLA_HV = GLA_VAL_DIM // GLA_HEADS
GLA_RANK = 16
GLA_TAU = 16.0
GLA_CHUNK = 64

NSA_HEADS = 16
NSA_GROUPS = 4
NSA_HPG = NSA_HEADS // NSA_GROUPS
NSA_HD = D_MODEL // NSA_HEADS
CMP_LEN = 32
CMP_STRIDE = 16
CMP_HIDDEN = 2 * NSA_HD
SLC_LEN = 64
SLC_TOPK = 16
WIN = 512
NSA_Q_BLOCK = 64

N_EXPERTS = 32
TOP_K = 4
EXPERT_FF = D_MODEL
SWIGLU_LIMIT = 7.0
SWIGLU_ALPHA = 1.702

FORCE = 1e4
NEG = -1e30

VMEM_LIMIT_BYTES = 56 * 1024 * 1024

ROW_TILE = 512
EXPERT_TILE = 256
DISPATCH_TILE = 256
ROUTE_TILE = 512
GLA_STEP = 512
SLC_KEY_TILE = 512
WIN_KEYS = 640


def _params(*sem):
    return pltpu.CompilerParams(dimension_semantics=sem, vmem_limit_bytes=VMEM_LIMIT_BYTES)


def _norm_mod(x, g, shift, scale):
    y = x * lax.rsqrt(jnp.mean(x * x, axis=-1, keepdims=True) + RMS_EPS)
    return (y * g) * (1.0 + scale) + shift


def _dot(a, b):
    return jnp.dot(a, b, preferred_element_type=F32)


def _dot_nt(a, b, precision=None):
    return lax.dot_general(a, b, (((1,), (1,)), ((), ())), precision=precision,
                           preferred_element_type=F32)


def _ada_kernel(c_ref, w_ref, b_ref, o_ref):
    c = c_ref[...]
    cs = c * jax.nn.sigmoid(c)
    o_ref[0] = jnp.dot(cs, w_ref[0], precision=HIGHEST, preferred_element_type=F32) + b_ref[0]


def _ada_vectors(c, w, b):
    n_l, d, m = w.shape
    bsz = c.shape[0]
    tn = 1024
    return pl.pallas_call(
        _ada_kernel,
        grid=(n_l, m // tn),
        in_specs=[
            pl.BlockSpec((bsz, d), lambda l, j: (0, 0)),
            pl.BlockSpec((1, d, tn), lambda l, j: (l, 0, j)),
            pl.BlockSpec((1, 1, tn), lambda l, j: (l, 0, j)),
        ],
        out_specs=pl.BlockSpec((1, bsz, tn), lambda l, j: (l, 0, j)),
        out_shape=jax.ShapeDtypeStruct((n_l, bsz, m), F32),
        compiler_params=_params("parallel", "parallel"),
        name="ada_vectors",
    )(c, w, b.reshape(n_l, 1, m))


def _gla_proj_kernel(x_ref, g_ref, sh_ref, sc_ref, wq_ref, wk_ref, wv_ref, wr_ref, wlr_ref,
                     wg2_ref, bg_ref, q_ref, k_ref, v_ref, r_ref, la_ref):
    h = _norm_mod(x_ref[0], g_ref[...], sh_ref[0], sc_ref[0]).astype(BF16)
    q_ref[0] = _dot(h, wq_ref[...]).astype(BF16)
    k_ref[0] = _dot(h, wk_ref[...]).astype(BF16)
    v_ref[0] = _dot(h, wv_ref[...]).astype(BF16)
    r_ref[0] = _dot(h, wr_ref[...]).astype(BF16)
    g_lr = _dot(h, wlr_ref[...])
    z = jnp.dot(g_lr, wg2_ref[...], precision=HIGHEST, preferred_element_type=F32) + bg_ref[...]
    log_sig = jnp.minimum(z, 0.0) - jnp.log(1.0 + jnp.exp(-jnp.abs(z)))
    la_ref[0] = log_sig / GLA_TAU


def _gla_proj(x, g, shift, scale, w_in, w_gate2, b_gate):
    bsz, seq, d = x.shape
    tm = min(ROW_TILE, seq)
    kd, vd = GLA_KEY_DIM, GLA_VAL_DIM
    wq = w_in[:, :kd].astype(BF16)
    wk = w_in[:, kd:2 * kd].astype(BF16)
    wv = w_in[:, 2 * kd:2 * kd + vd].astype(BF16)
    wr = w_in[:, 2 * kd + vd:2 * kd + 2 * vd].astype(BF16)
    wlr = w_in[:, 2 * kd + 2 * vd:].astype(BF16)
    full = lambda shape: pl.BlockSpec(shape, lambda b, i: (0,) * len(shape))
    row = lambda n: pl.BlockSpec((1, tm, n), lambda b, i: (b, i, 0))
    vec = pl.BlockSpec((1, 1, d), lambda b, i: (b, 0, 0))
    return pl.pallas_call(
        _gla_proj_kernel,
        grid=(bsz, seq // tm),
        in_specs=[row(d), full((1, d)), vec, vec, full((d, kd)), full((d, kd)), full((d, vd)),
                  full((d, vd)), full((d, GLA_RANK)), full((GLA_RANK, kd)), full((1, kd))],
        out_specs=[row(kd), row(kd), row(vd), row(vd), row(kd)],
        out_shape=[jax.ShapeDtypeStruct((bsz, seq, kd), BF16),
                   jax.ShapeDtypeStruct((bsz, seq, kd), BF16),
                   jax.ShapeDtypeStruct((bsz, seq, vd), BF16),
                   jax.ShapeDtypeStruct((bsz, seq, vd), BF16),
                   jax.ShapeDtypeStruct((bsz, seq, kd), F32)],
        compiler_params=_params("parallel", "parallel"),
        name="gla_proj",
    )(x, g.reshape(1, d), shift, scale, wq, wk, wv, wr, wlr, w_gate2, b_gate.reshape(1, kd))


def _gla_core_kernel(q_ref, k_ref, v_ref, r_ref, la_ref, ng_ref, o_ref, state_ref, *, n_chunks):
    @pl.when(pl.program_id(2) == 0)
    def _():
        state_ref[...] = jnp.zeros_like(state_ref)

    c_len = GLA_CHUNK
    row = lax.broadcasted_iota(jnp.int32, (c_len, c_len), 0)
    col = lax.broadcasted_iota(jnp.int32, (c_len, c_len), 1)
    causal = col <= row
    tril = causal.astype(F32)
    ng = ng_ref[...]

    def chunk(c, carry):
        c0 = pl.multiple_of(c * c_len, c_len)
        la = la_ref[0, pl.ds(c0, c_len), :]
        b = jnp.dot(tril, la, precision=HIGHEST, preferred_element_type=F32)
        q = q_ref[0, pl.ds(c0, c_len), :].astype(F32) * (GLA_HK ** -0.5)
        k = k_ref[0, pl.ds(c0, c_len), :].astype(F32)
        v = v_ref[0, pl.ds(c0, c_len), :]
        q_dec = (q * jnp.exp(b)).astype(BF16)
        k_intra = (k * jnp.exp(-b)).astype(BF16)
        b_t = b.T
        bl_t = b_t[:, c_len - 1:c_len]
        k_inter_t = (k.T * jnp.exp(bl_t - b_t)).astype(BF16)
        att = jnp.where(causal, _dot_nt(q_dec, k_intra), 0.0).astype(BF16)
        state = state_ref[...]
        o = _dot(att, v) + _dot(q_dec, state.astype(BF16))
        state_ref[...] = state * jnp.exp(bl_t) + _dot(k_inter_t, v)
        o = o * lax.rsqrt(jnp.mean(o * o, axis=-1, keepdims=True) + RMS_EPS)
        r = r_ref[0, pl.ds(c0, c_len), :].astype(F32)
        o_ref[0, pl.ds(c0, c_len), :] = ((o * ng) * (r * jax.nn.sigmoid(r))).astype(BF16)
        return carry

    lax.fori_loop(0, n_chunks, chunk, 0)


def _gla_core(q, k, v, r, la, norm_g):
    bsz, seq, _ = q.shape
    ts = min(GLA_STEP, seq)
    kern = functools.partial(_gla_core_kernel, n_chunks=ts // GLA_CHUNK)
    hk = lambda: pl.BlockSpec((1, ts, GLA_HK), lambda b, h, s: (b, s, h))
    hv = lambda: pl.BlockSpec((1, ts, GLA_HV), lambda b, h, s: (b, s, h))
    return pl.pallas_call(
        kern,
        grid=(bsz, GLA_HEADS, seq // ts),
        in_specs=[hk(), hk(), hv(), hv(), hk(), pl.BlockSpec((1, GLA_HV), lambda b, h, s: (0, h))],
        out_specs=hv(),
        out_shape=jax.ShapeDtypeStruct((bsz, seq, GLA_VAL_DIM), BF16),
        scratch_shapes=[pltpu.VMEM((GLA_HK, GLA_HV), F32)],
        compiler_params=_params("parallel", "parallel", "arbitrary"),
        name="gla_core",
    )(q, k, v, r, la, norm_g.reshape(1, GLA_VAL_DIM))


def _res_matmul_kernel(a_ref, w_ref, x_ref, gate_ref, o_ref):
    o_ref[0] = x_ref[0] + gate_ref[0] * _dot(a_ref[0], w_ref[...])


def _res_matmul(a, w, x, gate):
    bsz, seq, d = x.shape
    kdim = a.shape[-1]
    tm = min(ROW_TILE, seq)
    return pl.pallas_call(
        _res_matmul_kernel,
        grid=(bsz, seq // tm),
        in_specs=[pl.BlockSpec((1, tm, kdim), lambda b, i: (b, i, 0)),
                  pl.BlockSpec((kdim, d), lambda b, i: (0, 0)),
                  pl.BlockSpec((1, tm, d), lambda b, i: (b, i, 0)),
                  pl.BlockSpec((1, 1, d), lambda b, i: (b, 0, 0))],
        out_specs=pl.BlockSpec((1, tm, d), lambda b, i: (b, i, 0)),
        out_shape=jax.ShapeDtypeStruct((bsz, seq, d), F32),
        compiler_params=_params("parallel", "parallel"),
        name="res_matmul",
    )(a, w.astype(BF16), x, gate)


def _route_kernel(x_ref, g_ref, sh_ref, sc_ref, rwt_ref, rb_ref,
                  h_ref, idx_ref, w_ref, rank_ref, cnt_ref, run_ref):
    @pl.when(pl.program_id(0) == 0)
    def _():
        run_ref[...] = jnp.zeros_like(run_ref)

    tm = x_ref.shape[0]
    h = _norm_mod(x_ref[...], g_ref[...], sh_ref[0], sc_ref[0])
    h_ref[...] = h
    logits = _dot_nt(rwt_ref[...], h, precision=HIGHEST) + rb_ref[...]
    e_iota = lax.broadcasted_iota(jnp.int32, logits.shape, 0)
    vals, idxs, hots = [], [], []
    for _ in range(TOP_K):
        m = jnp.max(logits, axis=0, keepdims=True)
        idx = jnp.min(jnp.where(logits == m, e_iota, N_EXPERTS), axis=0, keepdims=True)
        hot = e_iota == idx
        vals.append(m)
        idxs.append(idx)
        hots.append(hot)
        logits = jnp.where(hot, -jnp.inf, logits)
    exps = [jnp.exp(v - vals[0]) for v in vals]
    denom = exps[0] + exps[1] + exps[2] + exps[3]
    sel = (hots[0] | hots[1] | hots[2] | hots[3]).astype(F32)
    s_iota = lax.broadcasted_iota(jnp.int32, (tm, tm), 0)
    t_iota = lax.broadcasted_iota(jnp.int32, (tm, tm), 1)
    before = (s_iota < t_iota).astype(BF16)
    prefix = _dot(sel.astype(BF16), before) + run_ref[:, 0:1]
    for kk in range(TOP_K):
        idx_ref[kk:kk + 1, :] = idxs[kk]
        w_ref[kk:kk + 1, :] = exps[kk] / denom
        rank_ref[kk:kk + 1, :] = jnp.sum(jnp.where(hots[kk], prefix, 0.0), axis=0,
                                         keepdims=True).astype(jnp.int32)
    run_ref[...] = run_ref[...] + jnp.sum(sel, axis=1, keepdims=True)
    cnt_ref[...] = run_ref[...].astype(jnp.int32)


def _route(x2d, g, shift, scale, router_w, router_b, seq):
    n_tok, d = x2d.shape
    tm = min(ROUTE_TILE, seq)
    per_b = seq // tm
    vec = pl.BlockSpec((1, 1, d), lambda i: (i // per_b, 0, 0))
    tok = lambda: pl.BlockSpec((TOP_K, tm), lambda i: (0, i))
    return pl.pallas_call(
        _route_kernel,
        grid=(n_tok // tm,),
        in_specs=[pl.BlockSpec((tm, d), lambda i: (i, 0)),
                  pl.BlockSpec((1, d), lambda i: (0, 0)), vec, vec,
                  pl.BlockSpec((N_EXPERTS, d), lambda i: (0, 0)),
                  pl.BlockSpec((N_EXPERTS, 1), lambda i: (0, 0))],
        out_specs=[pl.BlockSpec((tm, d), lambda i: (i, 0)), tok(), tok(), tok(),
                   pl.BlockSpec((N_EXPERTS, 128), lambda i: (0, 0))],
        out_shape=[jax.ShapeDtypeStruct((n_tok, d), F32),
                   jax.ShapeDtypeStruct((TOP_K, n_tok), jnp.int32),
                   jax.ShapeDtypeStruct((TOP_K, n_tok), F32),
                   jax.ShapeDtypeStruct((TOP_K, n_tok), jnp.int32),
                   jax.ShapeDtypeStruct((N_EXPERTS, 128), jnp.int32)],
        scratch_shapes=[pltpu.VMEM((N_EXPERTS, 128), F32)],
        compiler_params=_params("arbitrary"),
        name="moe_route",
    )(x2d, g.reshape(1, d), shift, scale, router_w.T, router_b.reshape(N_EXPERTS, 1))


def _dispatch_kernel(dest_hbm, h_ref, xs_in, xs_out, dest_smem, idx_sem, row_sem):
    del xs_in
    i = pl.program_id(0)
    tm = h_ref.shape[0]
    cp = pltpu.make_async_copy(dest_hbm.at[i], dest_smem, idx_sem)
    cp.start()
    cp.wait()

    def row_copy(j, kk):
        return pltpu.make_async_copy(h_ref.at[pl.ds(j, 1), :],
                                     xs_out.at[pl.ds(dest_smem[kk * tm + j], 1), :], row_sem)

    def issue(j, carry):
        for kk in range(TOP_K):
            row_copy(j, kk).start()
        return carry

    def drain(j, carry):
        for kk in range(TOP_K):
            row_copy(j, kk).wait()
        return carry

    lax.fori_loop(0, tm, issue, 0)
    lax.fori_loop(0, tm, drain, 0)


def _dispatch(h2d, dest_tiles, xs_zero):
    n_tok, d = h2d.shape
    tm = DISPATCH_TILE
    return pl.pallas_call(
        _dispatch_kernel,
        grid=(n_tok // tm,),
        in_specs=[pl.BlockSpec(memory_space=pl.ANY),
                  pl.BlockSpec((tm, d), lambda i: (i, 0)),
                  pl.BlockSpec(memory_space=pl.ANY)],
        out_specs=pl.BlockSpec(memory_space=pl.ANY),
        out_shape=jax.ShapeDtypeStruct(xs_zero.shape, xs_zero.dtype),
        input_output_aliases={2: 0},
        scratch_shapes=[pltpu.SMEM((TOP_K * tm,), jnp.int32),
                        pltpu.SemaphoreType.DMA, pltpu.SemaphoreType.DMA],
        compiler_params=_params("arbitrary"),
        name="moe_dispatch",
    )(dest_tiles, h2d, xs_zero)


def _expert_kernel(te_ref, nu_ref, xs_ref, wg_ref, wu_ref, wd_ref, bg_ref, bu_ref, bd_ref, ys_ref):
    del te_ref

    @pl.when(pl.program_id(0) < nu_ref[0])
    def _():
        xb = xs_ref[...].astype(BF16)
        gate = jnp.minimum(_dot(xb, wg_ref[0]) + bg_ref[0], SWIGLU_LIMIT)
        up = jnp.clip(_dot(xb, wu_ref[0]) + bu_ref[0], -SWIGLU_LIMIT, SWIGLU_LIMIT)
        act = (up + 1.0) * gate * jax.nn.sigmoid(SWIGLU_ALPHA * gate)
        ys_ref[...] = _dot(act.astype(BF16), wd_ref[0]) + bd_ref[0]

    @pl.when(pl.program_id(0) >= nu_ref[0])
    def _():
        ys_ref[...] = jnp.zeros_like(ys_ref)


def _experts(xs, tile_expert, n_used, wg, wu, wd, bg, bu, bd):
    cap, d = xs.shape
    tm = EXPERT_TILE
    ff = wg.shape[-1]
    rows = lambda i, te, nu: (jnp.minimum(i, nu[0] - 1), 0)
    wsel = lambda i, te, nu: (te[i], 0, 0)
    return pl.pallas_call(
        _expert_kernel,
        grid_spec=pltpu.PrefetchScalarGridSpec(
            num_scalar_prefetch=2,
            grid=(cap // tm,),
            in_specs=[pl.BlockSpec((tm, d), rows),
                      pl.BlockSpec((1, d, ff), wsel), pl.BlockSpec((1, d, ff), wsel),
                      pl.BlockSpec((1, ff, d), wsel),
                      pl.BlockSpec((1, 1, ff), wsel), pl.BlockSpec((1, 1, ff), wsel),
                      pl.BlockSpec((1, 1, d), wsel)],
            out_specs=pl.BlockSpec((tm, d), lambda i, te, nu: (i, 0))),
        out_shape=jax.ShapeDtypeStruct((cap, d), F32),
        compiler_params=_params("arbitrary"),
        name="moe_experts",
    )(tile_expert, n_used, xs, wg, wu, wd, bg, bu, bd)


def _combine_kernel(dest_hbm, ys_hbm, x_ref, w_ref, gate_ref, o_ref, buf, dest_smem, idx_sem, row_sem):
    i = pl.program_id(0)
    tm = x_ref.shape[0]
    cp = pltpu.make_async_copy(dest_hbm.at[i], dest_smem, idx_sem)
    cp.start()
    cp.wait()

    def row_copy(j, kk):
        return pltpu.make_async_copy(ys_hbm.at[pl.ds(dest_smem[kk * tm + j], 1), :],
                                     buf.at[kk, pl.ds(j, 1), :], row_sem)

    def issue(j, carry):
        for kk in range(TOP_K):
            row_copy(j, kk).start()
        return carry

    def drain(j, carry):
        for kk in range(TOP_K):
            row_copy(j, kk).wait()
        return carry

    lax.fori_loop(0, tm, issue, 0)
    lax.fori_loop(0, tm, drain, 0)
    w = w_ref[...]
    y = w[:, 0:1] * buf[0]
    for kk in range(1, TOP_K):
        y = y + w[:, kk:kk + 1] * buf[kk]
    o_ref[...] = x_ref[...] + gate_ref[0] * y


def _combine(ys, dest_tiles, x2d, w_tok, gate, seq):
    n_tok, d = x2d.shape
    tm = DISPATCH_TILE
    per_b = seq // tm
    return pl.pallas_call(
        _combine_kernel,
        grid=(n_tok // tm,),
        in_specs=[pl.BlockSpec(memory_space=pl.ANY),
                  pl.BlockSpec(memory_space=pl.ANY),
                  pl.BlockSpec((tm, d), lambda i: (i, 0)),
                  pl.BlockSpec((tm, TOP_K), lambda i: (i, 0)),
                  pl.BlockSpec((1, 1, d), lambda i: (i // per_b, 0, 0))],
        out_specs=pl.BlockSpec((tm, d), lambda i: (i, 0)),
        out_shape=jax.ShapeDtypeStruct((n_tok, d), F32),
        scratch_shapes=[pltpu.VMEM((TOP_K, tm, d), F32),
                        pltpu.SMEM((TOP_K * tm,), jnp.int32),
                        pltpu.SemaphoreType.DMA, pltpu.SemaphoreType.DMA],
        compiler_params=_params("arbitrary"),
        name="moe_combine",
    )(dest_tiles, ys, x2d, w_tok, gate)


def _moe_layer(x, norm_g, shift, scale, gate, router_w, router_b, w_gu, b_gu, w_down, b_down):
    bsz, seq, d = x.shape
    n_tok = bsz * seq
    x2d = x.reshape(n_tok, d)
    h2d, idx_t, w_t, rank_t, counts = _route(x2d, norm_g, shift, scale, router_w, router_b, seq)

    counts = counts[:, 0]
    te = EXPERT_TILE
    padded = (counts + te - 1) // te * te
    ends = jnp.cumsum(padded)
    starts = ends - padded
    cap = n_tok * TOP_K + N_EXPERTS * te
    n_tiles = cap // te
    tile_expert = jnp.minimum(
        jnp.searchsorted(ends, jnp.arange(n_tiles, dtype=jnp.int32) * te, side="right"),
        N_EXPERTS - 1).astype(jnp.int32)
    n_used = (ends[-1:] // te).astype(jnp.int32)
    dest = starts.astype(jnp.int32)[idx_t] + rank_t
    tm = DISPATCH_TILE
    dest_tiles = dest.reshape(TOP_K, n_tok // tm, tm).transpose(1, 0, 2).reshape(n_tok // tm, TOP_K * tm)

    xs = _dispatch(h2d, dest_tiles, jnp.zeros((cap, d), F32))
    wg = w_gu[:, :, 0::2].astype(BF16)
    wu = w_gu[:, :, 1::2].astype(BF16)
    bg = b_gu[:, None, 0::2]
    bu = b_gu[:, None, 1::2]
    ys = _experts(xs, tile_expert, n_used, wg, wu, w_down.astype(BF16), bg, bu, b_down[:, None, :])
    out = _combine(ys, dest_tiles, x2d, w_t.T, gate, seq)
    return out.reshape(bsz, seq, d)


def _kv_proj_kernel(x_ref, g_ref, sh_ref, sc_ref, w_ref, cmp_ref, kv_ref):
    h = _norm_mod(x_ref[0], g_ref[...], sh_ref[0], sc_ref[0]).astype(BF16)
    kv = _dot(h, w_ref[...])
    for t in range(6):
        for g in range(NSA_GROUPS):
            c0 = (t * NSA_GROUPS + g) * NSA_HD
            piece = kv[:, c0:c0 + NSA_HD]
            if t < 2:
                cmp_ref[t, 0, g] = piece
            else:
                kv_ref[t - 2, 0, g] = piece.astype(BF16)


def _kv_proj(x, g, shift, scale, kv_w):
    bsz, seq, d = x.shape
    tm = min(ROW_TILE, seq)
    n_out = kv_w.shape[1]
    vec = pl.BlockSpec((1, 1, d), lambda b, i: (b, 0, 0))
    return pl.pallas_call(
        _kv_proj_kernel,
        grid=(bsz, seq // tm),
        in_specs=[pl.BlockSpec((1, tm, d), lambda b, i: (b, i, 0)),
                  pl.BlockSpec((1, d), lambda b, i: (0, 0)), vec, vec,
                  pl.BlockSpec((d, n_out), lambda b, i: (0, 0))],
        out_specs=[pl.BlockSpec((2, 1, NSA_GROUPS, tm, NSA_HD), lambda b, i: (0, b, 0, i, 0)),
                   pl.BlockSpec((4, 1, NSA_GROUPS, tm, NSA_HD), lambda b, i: (0, b, 0, i, 0))],
        out_shape=[jax.ShapeDtypeStruct((2, bsz, NSA_GROUPS, seq, NSA_HD), F32),
                   jax.ShapeDtypeStruct((4, bsz, NSA_GROUPS, seq, NSA_HD), BF16)],
        compiler_params=_params("parallel", "parallel"),
        name="nsa_kv_proj",
    )(x, g.reshape(1, d), shift, scale, kv_w.astype(BF16))


def _compress_kernel(x_ref, pos_ref, w1_ref, b1_ref, w2_ref, b2_ref, o_ref, *, n_rows):
    for g in range(NSA_GROUPS):
        first = jnp.zeros((n_rows, CMP_HIDDEN), F32)
        second = jnp.zeros((n_rows, CMP_HIDDEN), F32)
        for l in range(CMP_STRIDE):
            rows = x_ref[0, 0, g, pl.ds(l, n_rows, stride=CMP_STRIDE), :]
            first = first + _dot((rows + pos_ref[0, l:l + 1, :]).astype(BF16), w1_ref[0, l])
            second = second + _dot((rows + pos_ref[0, CMP_STRIDE + l:CMP_STRIDE + l + 1, :]).astype(BF16),
                                   w1_ref[0, CMP_STRIDE + l])
        pre = first + pltpu.roll(second, n_rows - 1, 0) + b1_ref[0]
        hid = 0.5 * pre * (1.0 + jnp.tanh(0.7978845608028654 * (pre + 0.044715 * pre * pre * pre)))
        o_ref[0, 0, g] = (_dot(hid.astype(BF16), w2_ref[0]) + b2_ref[0]).astype(BF16)


def _compress(kv_cmp, cmp_pos, cmp_w1, cmp_b1, cmp_w2, cmp_b2):
    _, bsz, _, seq, _ = kv_cmp.shape
    n_rows = seq // CMP_STRIDE
    kern = functools.partial(_compress_kernel, n_rows=n_rows)
    w1 = cmp_w1.reshape(2, CMP_LEN, NSA_HD, CMP_HIDDEN).astype(BF16)
    return pl.pallas_call(
        kern,
        grid=(2, bsz),
        in_specs=[pl.BlockSpec((1, 1, NSA_GROUPS, seq, NSA_HD), lambda t, b: (t, b, 0, 0, 0)),
                  pl.BlockSpec((1, CMP_LEN, NSA_HD), lambda t, b: (t, 0, 0)),
                  pl.BlockSpec((1, CMP_LEN, NSA_HD, CMP_HIDDEN), lambda t, b: (t, 0, 0, 0)),
                  pl.BlockSpec((1, 1, CMP_HIDDEN), lambda t, b: (t, 0, 0)),
                  pl.BlockSpec((1, CMP_HIDDEN, NSA_HD), lambda t, b: (t, 0, 0)),
                  pl.BlockSpec((1, 1, NSA_HD), lambda t, b: (t, 0, 0))],
        out_specs=pl.BlockSpec((1, 1, NSA_GROUPS, n_rows, NSA_HD), lambda t, b: (t, b, 0, 0, 0)),
        out_shape=jax.ShapeDtypeStruct((2, bsz, NSA_GROUPS, n_rows, NSA_HD), BF16),
        compiler_params=_params("parallel", "parallel"),
        name="nsa_compress",
    )(kv_cmp, cmp_pos, w1, cmp_b1[:, None, :], cmp_w2.astype(BF16), cmp_b2[:, None, :])


def _nsa_proj_kernel(x_ref, g_ref, sh_ref, sc_ref, wq_ref, wg_ref, bg_ref, q_ref, gate_ref):
    h = _norm_mod(x_ref[0], g_ref[...], sh_ref[0], sc_ref[0]).astype(BF16)
    q_ref[0] = (_dot(h, wq_ref[...]) * (NSA_HD ** -0.5)).astype(BF16)
    gate_ref[0] = jax.nn.sigmoid(_dot(h, wg_ref[...]) + bg_ref[...])


def _nsa_proj(x, g, shift, scale, w_in, b_gate):
    bsz, seq, d = x.shape
    tm = min(ROW_TILE, seq)
    nq = NSA_HEADS * NSA_HD
    ng = 3 * NSA_HEADS
    vec = pl.BlockSpec((1, 1, d), lambda b, i: (b, 0, 0))
    return pl.pallas_call(
        _nsa_proj_kernel,
        grid=(bsz, seq // tm),
        in_specs=[pl.BlockSpec((1, tm, d), lambda b, i: (b, i, 0)),
                  pl.BlockSpec((1, d), lambda b, i: (0, 0)), vec, vec,
                  pl.BlockSpec((d, nq), lambda b, i: (0, 0)),
                  pl.BlockSpec((d, ng), lambda b, i: (0, 0)),
                  pl.BlockSpec((1, ng), lambda b, i: (0, 0))],
        out_specs=[pl.BlockSpec((1, tm, nq), lambda b, i: (b, i, 0)),
                   pl.BlockSpec((1, tm, ng), lambda b, i: (b, i, 0))],
        out_shape=[jax.ShapeDtypeStruct((bsz, seq, nq), BF16),
                   jax.ShapeDtypeStruct((bsz, seq, ng), F32)],
        compiler_params=_params("parallel", "parallel"),
        name="nsa_proj",
    )(x, g.reshape(1, d), shift, scale, w_in[:, :nq].astype(BF16), w_in[:, nq:].astype(BF16),
      b_gate.reshape(1, ng))


def _softmax_rows(s, mask):
    s = jnp.where(mask, s, NEG)
    m = jnp.max(s, axis=-1, keepdims=True)
    e = jnp.where(mask, jnp.exp(s - m), 0.0)
    l = jnp.sum(e, axis=-1, keepdims=True)
    return e / jnp.where(l > 0.0, l, 1.0)


def _nsa_attn_kernel(q_ref, gate_ref, slope_ref, ovl_ref, exp_ref, kc_ref, vc_ref, ks_ref, vs_ref,
                     kw_ref, vw_ref, o_ref, mask_ref, *, seq):
    qi = pl.program_id(2)
    qb = NSA_Q_BLOCK
    rows = NSA_HPG * qb
    q0 = qi * qb
    qt = q_ref[0]
    q = jnp.concatenate([qt[:, hh * NSA_HD:(hh + 1) * NSA_HD] for hh in range(NSA_HPG)], axis=0)
    slope = slope_ref[0]
    t_row = q0 + lax.broadcasted_iota(jnp.int32, (rows, 1), 0) % qb

    n_cmp = kc_ref.shape[2]
    cmp_end = lax.broadcasted_iota(jnp.int32, (1, n_cmp), 1) * CMP_STRIDE + (CMP_LEN - 1)
    dist_c = t_row - cmp_end
    s_c = _dot_nt(q, kc_ref[0, 0]) - slope * dist_c.astype(F32)
    p_c = _softmax_rows(s_c, dist_c >= 0)
    o_c = _dot(p_c.astype(BF16), vc_ref[0, 0])

    p_sum = p_c[0:qb]
    for hh in range(1, NSA_HPG):
        p_sum = p_sum + p_c[hh * qb:(hh + 1) * qb]
    imp_t = _dot_nt(ovl_ref[...], p_sum, precision=HIGHEST)
    n_slc = imp_t.shape[0]
    blk = lax.broadcasted_iota(jnp.int32, (n_slc, qb), 0)
    forced = (blk == 0) | (blk == qi) | (blk == qi - 1)
    score = jnp.where(blk <= qi, imp_t + jnp.where(forced, FORCE, 0.0), -FORCE)
    rank = jnp.zeros((n_slc, qb), F32)
    for i in range(n_slc):
        other = score[i:i + 1, :]
        ahead = (other > score) | ((other == score) & (blk > i))
        rank = rank + ahead.astype(F32)
    sel_t = (rank < float(min(SLC_TOPK, n_slc))).astype(BF16)
    eye = (lax.broadcasted_iota(jnp.int32, (qb, qb), 0) ==
           lax.broadcasted_iota(jnp.int32, (qb, qb), 1)).astype(BF16)
    sel = _dot_nt(eye, sel_t).astype(BF16)
    mask_ref[...] = _dot(sel, exp_ref[...])

    tk = SLC_KEY_TILE

    def slc_step(kt, carry):
        m_i, l_i, acc = carry
        k0 = pl.multiple_of(kt * tk, tk)
        kpos = k0 + lax.broadcasted_iota(jnp.int32, (1, tk), 1)
        dist = t_row - kpos
        s = _dot_nt(q, ks_ref[0, 0, pl.ds(k0, tk), :]) - slope * dist.astype(F32)
        picked = mask_ref[:, pl.ds(k0, tk)] > 0.5
        valid = jnp.concatenate([picked] * NSA_HPG, axis=0) & (dist >= 0)
        s = jnp.where(valid, s, NEG)
        m_new = jnp.maximum(m_i, jnp.max(s, axis=-1, keepdims=True))
        alpha = jnp.exp(m_i - m_new)
        e = jnp.where(valid, jnp.exp(s - m_new), 0.0)
        l_new = alpha * l_i + jnp.sum(e, axis=-1, keepdims=True)
        acc = alpha * acc + _dot(e.astype(BF16), vs_ref[0, 0, pl.ds(k0, tk), :])
        return m_new, l_new, acc

    n_kt = (q0 + qb + tk - 1) // tk
    init = (jnp.full((rows, 1), NEG, F32), jnp.zeros((rows, 1), F32), jnp.zeros((rows, NSA_HD), F32))
    _, l_s, acc_s = lax.fori_loop(0, n_kt, slc_step, init)
    o_s = acc_s / jnp.where(l_s > 0.0, l_s, 1.0)

    nw = min(WIN_KEYS, seq)
    w0 = pl.multiple_of(jnp.maximum(q0 + qb - nw, 0), qb)
    wpos = w0 + lax.broadcasted_iota(jnp.int32, (1, nw), 1)
    dist_w = t_row - wpos
    s_w = _dot_nt(q, kw_ref[0, 0, pl.ds(w0, nw), :]) - slope * dist_w.astype(F32)
    p_w = _softmax_rows(s_w, (dist_w >= 0) & (dist_w < WIN))
    o_w = _dot(p_w.astype(BF16), vw_ref[0, 0, pl.ds(w0, nw), :])

    gates = gate_ref[0, 0]
    outs = []
    for hh in range(NSA_HPG):
        r0 = hh * qb
        outs.append(gates[:, hh:hh + 1] * o_c[r0:r0 + qb]
                    + gates[:, NSA_HPG + hh:NSA_HPG + hh + 1] * o_s[r0:r0 + qb]
                    + gates[:, 2 * NSA_HPG + hh:2 * NSA_HPG + hh + 1] * o_w[r0:r0 + qb])
    o_ref[0] = jnp.concatenate(outs, axis=1).astype(BF16)


def _nsa_constants(seq):
    n_cmp_rows = seq // CMP_STRIDE
    n_slc = seq // SLC_LEN
    cmp_start = np.arange(n_cmp_rows) * CMP_STRIDE
    cmp_end = cmp_start + CMP_LEN - 1
    slc_start = np.arange(n_slc) * SLC_LEN
    overlap = ((cmp_start[:, None] <= slc_start[None, :] + SLC_LEN - 1)
               & (cmp_end[:, None] >= slc_start[None, :])).astype(np.float32)
    expand = (np.arange(seq)[None, :] // SLC_LEN == np.arange(n_slc)[:, None]).astype(np.float32)
    start = 2.0 ** (-8.0 / NSA_HEADS)
    slopes = np.asarray(start ** np.arange(1, NSA_HEADS + 1), np.float32).reshape(NSA_GROUPS, NSA_HPG)
    slope_rows = np.repeat(slopes, NSA_Q_BLOCK, axis=1)[:, :, None]
    return (jnp.asarray(overlap.T), jnp.asarray(expand, BF16), jnp.asarray(slope_rows))


def _nsa_attn(q, gates, kc, vc, kv):
    bsz, seq, _ = q.shape
    qb = NSA_Q_BLOCK
    gw = NSA_HPG * NSA_HD
    n_cmp_rows = seq // CMP_STRIDE
    n_slc = seq // SLC_LEN
    ovl_t, expand, slope_rows = _nsa_constants(seq)
    kern = functools.partial(_nsa_attn_kernel, seq=seq)
    cmp_spec = lambda: pl.BlockSpec((1, 1, n_cmp_rows, NSA_HD), lambda b, g, i: (b, g, 0, 0))
    kv_spec = lambda t: pl.BlockSpec((None, 1, 1, seq, NSA_HD), lambda b, g, i, t=t: (t, b, g, 0, 0))
    return pl.pallas_call(
        kern,
        grid=(bsz, NSA_GROUPS, seq // qb),
        in_specs=[pl.BlockSpec((1, qb, gw), lambda b, g, i: (b, i, g)),
                  pl.BlockSpec((1, 1, qb, 3 * NSA_HPG), lambda b, g, i: (b, g, i, 0)),
                  pl.BlockSpec((1, NSA_HPG * qb, 1), lambda b, g, i: (g, 0, 0)),
                  pl.BlockSpec((n_slc, n_cmp_rows), lambda b, g, i: (0, 0)),
                  pl.BlockSpec((n_slc, seq), lambda b, g, i: (0, 0)),
                  cmp_spec(), cmp_spec(), kv_spec(0), kv_spec(1), kv_spec(2), kv_spec(3)],
        out_specs=pl.BlockSpec((1, qb, gw), lambda b, g, i: (b, i, g)),
        out_shape=jax.ShapeDtypeStruct((bsz, seq, NSA_HEADS * NSA_HD), BF16),
        scratch_shapes=[pltpu.VMEM((qb, seq), F32)],
        compiler_params=_params("parallel", "parallel", "arbitrary"),
        name="nsa_attn",
    )(q, gates, slope_rows, ovl_t, expand, kc, vc, kv, kv, kv, kv)


def _nsa_mixer(x, norm_g, shift, scale, shared, w_in, b_gate):
    kcv, kv = shared
    bsz, seq, _ = x.shape
    q, gates = _nsa_proj(x, norm_g, shift, scale, w_in, b_gate)
    gates = gates.reshape(bsz, seq, 3, NSA_GROUPS, NSA_HPG).transpose(0, 3, 1, 2, 4)
    gates = gates.reshape(bsz, NSA_GROUPS, seq, 3 * NSA_HPG)
    return _nsa_attn(q, gates, kcv[0], kcv[1], kv)


def _final_norm_kernel(x_ref, g_ref, o_ref):
    x = x_ref[...]
    o_ref[...] = x * lax.rsqrt(jnp.mean(x * x, axis=-1, keepdims=True) + RMS_EPS) * g_ref[...]


def _final_norm(x, g):
    bsz, seq, d = x.shape
    x2d = x.reshape(bsz * seq, d)
    tm = min(ROW_TILE, seq)
    out = pl.pallas_call(
        _final_norm_kernel,
        grid=(x2d.shape[0] // tm,),
        in_specs=[pl.BlockSpec((tm, d), lambda i: (i, 0)), pl.BlockSpec((1, d), lambda i: (0, 0))],
        out_specs=pl.BlockSpec((tm, d), lambda i: (i, 0)),
        out_shape=jax.ShapeDtypeStruct(x2d.shape, F32),
        compiler_params=_params("parallel"),
        name="final_norm",
    )(x2d, g.reshape(1, d))
    return out.reshape(bsz, seq, d)


def kernel(x, c, ada_w, ada_b, norm1_g, norm2_g, gla_w_in, gla_w_gate2, gla_b_gate, gla_norm_g, gla_w_out, kv_norm_g, kv_ada_w, kv_ada_b, kv_w, cmp_pos, cmp_w1, cmp_b1, cmp_w2, cmp_b2, nsa_w_in, nsa_b_gate, nsa_w_out, router_w, router_b, moe_w_gate_up, moe_b_gate_up, moe_w_down, moe_b_down, final_g):
    bsz, seq, d = x.shape
    mod = _ada_vectors(c, ada_w, ada_b)
    kv_mod = _ada_vectors(c, kv_ada_w[None], kv_ada_b[None])[0]
    vec = lambda m, j: m[:, None, j * d:(j + 1) * d]
    shared = None
    for layer in range(DEPTH):
        m = mod[layer]
        sh1, sc1, g1, sh2, sc2, g2 = (vec(m, j) for j in range(6))
        if layer < N_A_LAYERS:
            i = layer
            q, k, v, r, la = _gla_proj(x, norm1_g[layer], sh1, sc1, gla_w_in[i], gla_w_gate2[i], gla_b_gate[i])
            o = _gla_core(q, k, v, r, la, gla_norm_g[i])
            x = _res_matmul(o, gla_w_out[i], x, g1)
        else:
            i = layer - N_A_LAYERS
            o = _nsa_mixer(x, norm1_g[layer], sh1, sc1, shared, nsa_w_in[i], nsa_b_gate[i])
            x = _res_matmul(o, nsa_w_out[i], x, g1)
        x = _moe_layer(x, norm2_g[layer], sh2, sc2, g2, router_w[layer], router_b[layer],
                       moe_w_gate_up[layer], moe_b_gate_up[layer], moe_w_down[layer], moe_b_down[layer])
        if layer == N_A_LAYERS - 1:
            kv_cmp, kv = _kv_proj(x, kv_norm_g, vec(kv_mod, 0), vec(kv_mod, 1), kv_w)
            shared = (_compress(kv_cmp, cmp_pos, cmp_w1, cmp_b1, cmp_w2, cmp_b2), kv)
    return _final_norm(x, final_g)
```

```python
import functools

import numpy as np
import jax
import jax.numpy as jnp
from jax import lax
from jax.experimental import pallas as pl
from jax.experimental.pallas import tpu as pltpu

F32 = jnp.float32
BF16 = jnp.bfloat16
HIGHEST = lax.Precision.HIGHEST

D_MODEL = 1024
DEPTH = 4
N_A_LAYERS = DEPTH // 2
RMS_EPS = 1e-5

GLA_HEADS = 4
GLA_KEY_DIM = D_MODEL // 2
GLA_VAL_DIM = D_MODEL
GLA_HK = GLA_KEY_DIM // GLA_HEADS
GLA_HV = GLA_VAL_DIM // GLA_HEADS
GLA_RANK = 16
GLA_TAU = 16.0
GLA_CHUNK = 64

NSA_HEADS = 16
NSA_GROUPS = 4
NSA_HPG = NSA_HEADS // NSA_GROUPS
NSA_HD = D_MODEL // NSA_HEADS
CMP_LEN = 32
CMP_STRIDE = 16
CMP_HIDDEN = 2 * NSA_HD
SLC_LEN = 64
SLC_TOPK = 16
WIN = 512
NSA_Q_BLOCK = 64

N_EXPERTS = 32
TOP_K = 4
EXPERT_FF = D_MODEL
SWIGLU_LIMIT = 7.0
SWIGLU_ALPHA = 1.702

FORCE = 1e4
NEG = -1e30

VMEM_LIMIT_BYTES = 56 * 1024 * 1024

ROW_TILE = 512
EXPERT_TILE = 256
DISPATCH_TILE = 256
ROUTE_TILE = 512
GLA_STEP = 512
SLC_KEY_TILE = 512
WIN_PAD = WIN + NSA_Q_BLOCK
MASK_BIG = 1e30


def _params(*sem):
    return pltpu.CompilerParams(dimension_semantics=sem, vmem_limit_bytes=VMEM_LIMIT_BYTES)


def _norm_mod(x, g, shift, scale):
    y = x * lax.rsqrt(jnp.mean(x * x, axis=-1, keepdims=True) + RMS_EPS)
    return (y * g) * (1.0 + scale) + shift


def _dot(a, b):
    return jnp.dot(a, b, preferred_element_type=F32)


def _dot_nt(a, b, precision=None):
    return lax.dot_general(a, b, (((1,), (1,)), ((), ())), precision=precision,
                           preferred_element_type=F32)


def _ada_kernel(c_ref, w_ref, b_ref, o_ref):
    c = c_ref[...]
    cs = c * jax.nn.sigmoid(c)
    o_ref[0] = jnp.dot(cs, w_ref[0], precision=HIGHEST, preferred_element_type=F32) + b_ref[0]


def _ada_vectors(c, w, b):
    n_l, d, m = w.shape
    bsz = c.shape[0]
    tn = 1024
    return pl.pallas_call(
        _ada_kernel,
        grid=(n_l, m // tn),
        in_specs=[
            pl.BlockSpec((bsz, d), lambda l, j: (0, 0)),
            pl.BlockSpec((1, d, tn), lambda l, j: (l, 0, j)),
            pl.BlockSpec((1, 1, tn), lambda l, j: (l, 0, j)),
        ],
        out_specs=pl.BlockSpec((1, bsz, tn), lambda l, j: (l, 0, j)),
        out_shape=jax.ShapeDtypeStruct((n_l, bsz, m), F32),
        compiler_params=_params("parallel", "parallel"),
        name="ada_vectors",
    )(c, w, b.reshape(n_l, 1, m))


def _gla_proj_kernel(x_ref, g_ref, sh_ref, sc_ref, wq_ref, wk_ref, wv_ref, wr_ref, wlr_ref,
                     wg2_ref, bg_ref, q_ref, k_ref, v_ref, r_ref, la_ref):
    h = _norm_mod(x_ref[0], g_ref[...], sh_ref[0], sc_ref[0]).astype(BF16)
    q_ref[0] = _dot(h, wq_ref[...]).astype(BF16)
    k_ref[0] = _dot(h, wk_ref[...]).astype(BF16)
    v_ref[0] = _dot(h, wv_ref[...]).astype(BF16)
    r_ref[0] = _dot(h, wr_ref[...]).astype(BF16)
    g_lr = _dot(h, wlr_ref[...])
    z = jnp.dot(g_lr, wg2_ref[...], precision=HIGHEST, preferred_element_type=F32) + bg_ref[...]
    log_sig = jnp.minimum(z, 0.0) - jnp.log(1.0 + jnp.exp(-jnp.abs(z)))
    la_ref[0] = log_sig / GLA_TAU


def _gla_proj(x, g, shift, scale, w_in, w_gate2, b_gate):
    bsz, seq, d = x.shape
    tm = min(ROW_TILE, seq)
    kd, vd = GLA_KEY_DIM, GLA_VAL_DIM
    wq = w_in[:, :kd].astype(BF16)
    wk = w_in[:, kd:2 * kd].astype(BF16)
    wv = w_in[:, 2 * kd:2 * kd + vd].astype(BF16)
    wr = w_in[:, 2 * kd + vd:2 * kd + 2 * vd].astype(BF16)
    wlr = w_in[:, 2 * kd + 2 * vd:].astype(BF16)
    full = lambda shape: pl.BlockSpec(shape, lambda b, i: (0,) * len(shape))
    row = lambda n: pl.BlockSpec((1, tm, n), lambda b, i: (b, i, 0))
    vec = pl.BlockSpec((1, 1, d), lambda b, i: (b, 0, 0))
    return pl.pallas_call(
        _gla_proj_kernel,
        grid=(bsz, seq // tm),
        in_specs=[row(d), full((1, d)), vec, vec, full((d, kd)), full((d, kd)), full((d, vd)),
                  full((d, vd)), full((d, GLA_RANK)), full((GLA_RANK, kd)), full((1, kd))],
        out_specs=[row(kd), row(kd), row(vd), row(vd), row(kd)],
        out_shape=[jax.ShapeDtypeStruct((bsz, seq, kd), BF16),
                   jax.ShapeDtypeStruct((bsz, seq, kd), BF16),
                   jax.ShapeDtypeStruct((bsz, seq, vd), BF16),
                   jax.ShapeDtypeStruct((bsz, seq, vd), BF16),
                   jax.ShapeDtypeStruct((bsz, seq, kd), F32)],
        compiler_params=_params("parallel", "parallel"),
        name="gla_proj",
    )(x, g.reshape(1, d), shift, scale, wq, wk, wv, wr, wlr, w_gate2, b_gate.reshape(1, kd))


def _gla_core_kernel(q_ref, k_ref, v_ref, r_ref, la_ref, ng_ref, o_ref, state_ref, *, n_chunks):
    @pl.when(pl.program_id(2) == 0)
    def _():
        state_ref[...] = jnp.zeros_like(state_ref)

    c_len = GLA_CHUNK
    row = lax.broadcasted_iota(jnp.int32, (c_len, c_len), 0)
    col = lax.broadcasted_iota(jnp.int32, (c_len, c_len), 1)
    causal = col <= row
    tril = causal.astype(F32)
    ng = ng_ref[...]

    def chunk(c, carry):
        c0 = pl.multiple_of(c * c_len, c_len)
        la = la_ref[0, pl.ds(c0, c_len), :]
        b = jnp.dot(tril, la, precision=HIGHEST, preferred_element_type=F32)
        q = q_ref[0, pl.ds(c0, c_len), :].astype(F32) * (GLA_HK ** -0.5)
        k = k_ref[0, pl.ds(c0, c_len), :].astype(F32)
        v = v_ref[0, pl.ds(c0, c_len), :]
        q_dec = (q * jnp.exp(b)).astype(BF16)
        k_intra = (k * jnp.exp(-b)).astype(BF16)
        b_t = b.T
        bl_t = b_t[:, c_len - 1:c_len]
        k_inter_t = (k.T * jnp.exp(bl_t - b_t)).astype(BF16)
        att = jnp.where(causal, _dot_nt(q_dec, k_intra), 0.0).astype(BF16)
        state = state_ref[...]
        o = _dot(att, v) + _dot(q_dec, state.astype(BF16))
        state_ref[...] = state * jnp.exp(bl_t) + _dot(k_inter_t, v)
        o = o * lax.rsqrt(jnp.mean(o * o, axis=-1, keepdims=True) + RMS_EPS)
        r = r_ref[0, pl.ds(c0, c_len), :].astype(F32)
        o_ref[0, pl.ds(c0, c_len), :] = ((o * ng) * (r * jax.nn.sigmoid(r))).astype(BF16)
        return carry

    lax.fori_loop(0, n_chunks, chunk, 0)


def _gla_core(q, k, v, r, la, norm_g):
    bsz, seq, _ = q.shape
    ts = min(GLA_STEP, seq)
    kern = functools.partial(_gla_core_kernel, n_chunks=ts // GLA_CHUNK)
    hk = lambda: pl.BlockSpec((1, ts, GLA_HK), lambda b, h, s: (b, s, h))
    hv = lambda: pl.BlockSpec((1, ts, GLA_HV), lambda b, h, s: (b, s, h))
    return pl.pallas_call(
        kern,
        grid=(bsz, GLA_HEADS, seq // ts),
        in_specs=[hk(), hk(), hv(), hv(), hk(), pl.BlockSpec((1, GLA_HV), lambda b, h, s: (0, h))],
        out_specs=hv(),
        out_shape=jax.ShapeDtypeStruct((bsz, seq, GLA_VAL_DIM), BF16),
        scratch_shapes=[pltpu.VMEM((GLA_HK, GLA_HV), F32)],
        compiler_params=_params("parallel", "parallel", "arbitrary"),
        name="gla_core",
    )(q, k, v, r, la, norm_g.reshape(1, GLA_VAL_DIM))


def _res_matmul_kernel(a_ref, w_ref, x_ref, gate_ref, o_ref):
    o_ref[0] = x_ref[0] + gate_ref[0] * _dot(a_ref[0], w_ref[...])


def _res_matmul(a, w, x, gate):
    bsz, seq, d = x.shape
    kdim = a.shape[-1]
    tm = min(ROW_TILE, seq)
    return pl.pallas_call(
        _res_matmul_kernel,
        grid=(bsz, seq // tm),
        in_specs=[pl.BlockSpec((1, tm, kdim), lambda b, i: (b, i, 0)),
                  pl.BlockSpec((kdim, d), lambda b, i: (0, 0)),
                  pl.BlockSpec((1, tm, d), lambda b, i: (b, i, 0)),
                  pl.BlockSpec((1, 1, d), lambda b, i: (b, 0, 0))],
        out_specs=pl.BlockSpec((1, tm, d), lambda b, i: (b, i, 0)),
        out_shape=jax.ShapeDtypeStruct((bsz, seq, d), F32),
        compiler_params=_params("parallel", "parallel"),
        name="res_matmul",
    )(a, w.astype(BF16), x, gate)


def _route_kernel(x_ref, g_ref, sh_ref, sc_ref, rwt_ref, rb_ref,
                  h_ref, idx_ref, w_ref, rank_ref, cnt_ref, run_ref):
    @pl.when(pl.program_id(0) == 0)
    def _():
        run_ref[...] = jnp.zeros_like(run_ref)

    tm = x_ref.shape[0]
    h = _norm_mod(x_ref[...], g_ref[...], sh_ref[0], sc_ref[0])
    h_ref[...] = h
    logits = _dot_nt(rwt_ref[...], h, precision=HIGHEST) + rb_ref[...]
    e_iota = lax.broadcasted_iota(jnp.int32, logits.shape, 0)
    vals, idxs, hots = [], [], []
    for _ in range(TOP_K):
        m = jnp.max(logits, axis=0, keepdims=True)
        idx = jnp.min(jnp.where(logits == m, e_iota, N_EXPERTS), axis=0, keepdims=True)
        hot = e_iota == idx
        vals.append(m)
        idxs.append(idx)
        hots.append(hot)
        logits = jnp.where(hot, -jnp.inf, logits)
    exps = [jnp.exp(v - vals[0]) for v in vals]
    denom = exps[0] + exps[1] + exps[2] + exps[3]
    sel = (hots[0] | hots[1] | hots[2] | hots[3]).astype(F32)
    s_iota = lax.broadcasted_iota(jnp.int32, (tm, tm), 0)
    t_iota = lax.broadcasted_iota(jnp.int32, (tm, tm), 1)
    before = (s_iota < t_iota).astype(BF16)
    prefix = _dot(sel.astype(BF16), before) + run_ref[:, 0:1]
    for kk in range(TOP_K):
        idx_ref[kk:kk + 1, :] = idxs[kk]
        w_ref[kk:kk + 1, :] = exps[kk] / denom
        rank_ref[kk:kk + 1, :] = jnp.sum(jnp.where(hots[kk], prefix, 0.0), axis=0,
                                         keepdims=True).astype(jnp.int32)
    run_ref[...] = run_ref[...] + jnp.sum(sel, axis=1, keepdims=True)
    cnt_ref[...] = run_ref[...].astype(jnp.int32)


def _route(x2d, g, shift, scale, router_w, router_b, seq):
    n_tok, d = x2d.shape
    tm = min(ROUTE_TILE, seq)
    per_b = seq // tm
    vec = pl.BlockSpec((1, 1, d), lambda i: (i // per_b, 0, 0))
    tok = lambda: pl.BlockSpec((TOP_K, tm), lambda i: (0, i))
    return pl.pallas_call(
        _route_kernel,
        grid=(n_tok // tm,),
        in_specs=[pl.BlockSpec((tm, d), lambda i: (i, 0)),
                  pl.BlockSpec((1, d), lambda i: (0, 0)), vec, vec,
                  pl.BlockSpec((N_EXPERTS, d), lambda i: (0, 0)),
                  pl.BlockSpec((N_EXPERTS, 1), lambda i: (0, 0))],
        out_specs=[pl.BlockSpec((tm, d), lambda i: (i, 0)), tok(), tok(), tok(),
                   pl.BlockSpec((N_EXPERTS, 128), lambda i: (0, 0))],
        out_shape=[jax.ShapeDtypeStruct((n_tok, d), F32),
                   jax.ShapeDtypeStruct((TOP_K, n_tok), jnp.int32),
                   jax.ShapeDtypeStruct((TOP_K, n_tok), F32),
                   jax.ShapeDtypeStruct((TOP_K, n_tok), jnp.int32),
                   jax.ShapeDtypeStruct((N_EXPERTS, 128), jnp.int32)],
        scratch_shapes=[pltpu.VMEM((N_EXPERTS, 128), F32)],
        compiler_params=_params("arbitrary"),
        name="moe_route",
    )(x2d, g.reshape(1, d), shift, scale, router_w.T, router_b.reshape(N_EXPERTS, 1))


def _dispatch_kernel(dest_hbm, h_ref, xs_in, xs_out, dest_smem, idx_sem, row_sem):
    del xs_in
    i = pl.program_id(0)
    tm = h_ref.shape[0]
    cp = pltpu.make_async_copy(dest_hbm.at[i], dest_smem, idx_sem)
    cp.start()
    cp.wait()

    def row_copy(j, kk):
        return pltpu.make_async_copy(h_ref.at[pl.ds(j, 1), :],
                                     xs_out.at[pl.ds(dest_smem[kk * tm + j], 1), :], row_sem)

    def issue(j, carry):
        for kk in range(TOP_K):
            row_copy(j, kk).start()
        return carry

    def drain(j, carry):
        for kk in range(TOP_K):
            row_copy(j, kk).wait()
        return carry

    lax.fori_loop(0, tm, issue, 0)
    lax.fori_loop(0, tm, drain, 0)


def _dispatch(h2d, dest_tiles, xs_zero):
    n_tok, d = h2d.shape
    tm = DISPATCH_TILE
    return pl.pallas_call(
        _dispatch_kernel,
        grid=(n_tok // tm,),
        in_specs=[pl.BlockSpec(memory_space=pl.ANY),
                  pl.BlockSpec((tm, d), lambda i: (i, 0)),
                  pl.BlockSpec(memory_space=pl.ANY)],
        out_specs=pl.BlockSpec(memory_space=pl.ANY),
        out_shape=jax.ShapeDtypeStruct(xs_zero.shape, xs_zero.dtype),
        input_output_aliases={2: 0},
        scratch_shapes=[pltpu.SMEM((TOP_K * tm,), jnp.int32),
                        pltpu.SemaphoreType.DMA, pltpu.SemaphoreType.DMA],
        compiler_params=_params("arbitrary"),
        name="moe_dispatch",
    )(dest_tiles, h2d, xs_zero)


def _expert_kernel(te_ref, nu_ref, xs_ref, wgu_ref, wd_ref, bg_ref, bu_ref, bd_ref, ys_ref,
                   wg_s, wu_s, wd_s, tr_s):
    i = pl.program_id(0)
    used = i < nu_ref[0]
    fresh = (i == 0) | (te_ref[i] != te_ref[jnp.maximum(i - 1, 0)])

    @pl.when(used & fresh)
    def _():
        n_slab, chunk, lanes = tr_s.shape
        half = chunk // 2
        for c in range(wgu_ref.shape[2] // chunk):
            t = wgu_ref[0, :, c * chunk:(c + 1) * chunk].T
            for j in range(n_slab):
                tr_s[j] = t[:, j * lanes:(j + 1) * lanes]
            for j in range(n_slab):
                wg_s[c * half:(c + 1) * half, j * lanes:(j + 1) * lanes] = (
                    tr_s[j, pl.ds(0, half, stride=2), :].astype(BF16))
                wu_s[c * half:(c + 1) * half, j * lanes:(j + 1) * lanes] = (
                    tr_s[j, pl.ds(1, half, stride=2), :].astype(BF16))
        wd_s[...] = wd_ref[0].astype(BF16)

    @pl.when(used)
    def _():
        xb = xs_ref[...].astype(BF16)
        gate = jnp.minimum(_dot_nt(xb, wg_s[...]) + bg_ref[0], SWIGLU_LIMIT)
        up = jnp.clip(_dot_nt(xb, wu_s[...]) + bu_ref[0], -SWIGLU_LIMIT, SWIGLU_LIMIT)
        act = (up + 1.0) * gate * jax.nn.sigmoid(SWIGLU_ALPHA * gate)
        ys_ref[...] = _dot(act.astype(BF16), wd_s[...]) + bd_ref[0]

    @pl.when(jnp.logical_not(used))
    def _():
        ys_ref[...] = jnp.zeros_like(ys_ref)


def _experts(xs, tile_expert, n_used, w_gu, w_down, bg, bu, bd):
    cap, d = xs.shape
    tm = EXPERT_TILE
    ff = w_down.shape[1]
    rows = lambda i, te, nu: (jnp.minimum(i, nu[0] - 1), 0)
    wsel = lambda i, te, nu: (te[i], 0, 0)
    return pl.pallas_call(
        _expert_kernel,
        grid_spec=pltpu.PrefetchScalarGridSpec(
            num_scalar_prefetch=2,
            grid=(cap // tm,),
            in_specs=[pl.BlockSpec((tm, d), rows),
                      pl.BlockSpec((1, d, 2 * ff), wsel),
                      pl.BlockSpec((1, ff, d), wsel),
                      pl.BlockSpec((1, 1, ff), wsel), pl.BlockSpec((1, 1, ff), wsel),
                      pl.BlockSpec((1, 1, d), wsel)],
            out_specs=pl.BlockSpec((tm, d), lambda i, te, nu: (i, 0)),
            scratch_shapes=[pltpu.VMEM((ff, d), BF16), pltpu.VMEM((ff, d), BF16),
                            pltpu.VMEM((ff, d), BF16), pltpu.VMEM((d // 128, 256, 128), F32)]),
        out_shape=jax.ShapeDtypeStruct((cap, d), F32),
        compiler_params=_params("arbitrary"),
        name="moe_experts",
    )(tile_expert, n_used, xs, w_gu, w_down, bg, bu, bd)


def _combine_kernel(dest_hbm, ys_hbm, x_ref, w_ref, gate_ref, o_ref, buf, dest_smem, idx_sem, row_sem):
    i = pl.program_id(0)
    tm = x_ref.shape[0]
    cp = pltpu.make_async_copy(dest_hbm.at[i], dest_smem, idx_sem)
    cp.start()
    cp.wait()

    def row_copy(j, kk):
        return pltpu.make_async_copy(ys_hbm.at[pl.ds(dest_smem[kk * tm + j], 1), :],
                                     buf.at[kk, pl.ds(j, 1), :], row_sem)

    def issue(j, carry):
        for kk in range(TOP_K):
            row_copy(j, kk).start()
        return carry

    def drain(j, carry):
        for kk in range(TOP_K):
            row_copy(j, kk).wait()
        return carry

    lax.fori_loop(0, tm, issue, 0)
    lax.fori_loop(0, tm, drain, 0)
    w = w_ref[...]
    y = w[:, 0:1] * buf[0]
    for kk in range(1, TOP_K):
        y = y + w[:, kk:kk + 1] * buf[kk]
    o_ref[...] = x_ref[...] + gate_ref[0] * y


def _combine(ys, dest_tiles, x2d, w_tok, gate, seq):
    n_tok, d = x2d.shape
    tm = DISPATCH_TILE
    per_b = seq // tm
    return pl.pallas_call(
        _combine_kernel,
        grid=(n_tok // tm,),
        in_specs=[pl.BlockSpec(memory_space=pl.ANY),
                  pl.BlockSpec(memory_space=pl.ANY),
                  pl.BlockSpec((tm, d), lambda i: (i, 0)),
                  pl.BlockSpec((tm, TOP_K), lambda i: (i, 0)),
                  pl.BlockSpec((1, 1, d), lambda i: (i // per_b, 0, 0))],
        out_specs=pl.BlockSpec((tm, d), lambda i: (i, 0)),
        out_shape=jax.ShapeDtypeStruct((n_tok, d), F32),
        scratch_shapes=[pltpu.VMEM((TOP_K, tm, d), F32),
                        pltpu.SMEM((TOP_K * tm,), jnp.int32),
                        pltpu.SemaphoreType.DMA, pltpu.SemaphoreType.DMA],
        compiler_params=_params("arbitrary"),
        name="moe_combine",
    )(dest_tiles, ys, x2d, w_tok, gate)


def _moe_layer(x, norm_g, shift, scale, gate, router_w, router_b, w_gu, b_gu, w_down, b_down):
    bsz, seq, d = x.shape
    n_tok = bsz * seq
    x2d = x.reshape(n_tok, d)
    h2d, idx_t, w_t, rank_t, counts = _route(x2d, norm_g, shift, scale, router_w, router_b, seq)

    counts = counts[:, 0]
    te = EXPERT_TILE
    padded = (counts + te - 1) // te * te
    ends = jnp.cumsum(padded)
    starts = ends - padded
    cap = n_tok * TOP_K + N_EXPERTS * te
    n_tiles = cap // te
    tile_row0 = jnp.arange(n_tiles, dtype=jnp.int32) * te
    tile_expert = jnp.minimum(jnp.sum((ends[None, :] <= tile_row0[:, None]).astype(jnp.int32), axis=1),
                              N_EXPERTS - 1).astype(jnp.int32)
    n_used = (ends[-1:] // te).astype(jnp.int32)
    e_ids = jnp.arange(N_EXPERTS, dtype=jnp.int32)[:, None, None]
    start_of = jnp.sum(jnp.where(idx_t[None] == e_ids, starts.astype(jnp.int32)[:, None, None], 0), axis=0)
    dest = start_of + rank_t
    tm = DISPATCH_TILE
    dest_tiles = dest.reshape(TOP_K, n_tok // tm, tm).transpose(1, 0, 2).reshape(n_tok // tm, TOP_K * tm)

    xs = _dispatch(h2d, dest_tiles, jnp.zeros((cap, d), F32))
    ys = _experts(xs, tile_expert, n_used, w_gu, w_down, b_gu[:, None, 0::2], b_gu[:, None, 1::2],
                  b_down[:, None, :])
    out = _combine(ys, dest_tiles, x2d, w_t.T, gate, seq)
    return out.reshape(bsz, seq, d)


def _kv_proj_kernel(x_ref, g_ref, sh_ref, sc_ref, w_ref, cmp_ref, kv_ref):
    h = _norm_mod(x_ref[0], g_ref[...], sh_ref[0], sc_ref[0]).astype(BF16)
    kv = _dot(h, w_ref[...])
    for t in range(6):
        for g in range(NSA_GROUPS):
            c0 = (t * NSA_GROUPS + g) * NSA_HD
            piece = kv[:, c0:c0 + NSA_HD]
            if t < 2:
                cmp_ref[t, 0, g] = piece
            else:
                kv_ref[t - 2, 0, g] = piece.astype(BF16)


def _kv_proj(x, g, shift, scale, kv_w):
    bsz, seq, d = x.shape
    tm = min(ROW_TILE, seq)
    n_out = kv_w.shape[1]
    vec = pl.BlockSpec((1, 1, d), lambda b, i: (b, 0, 0))
    return pl.pallas_call(
        _kv_proj_kernel,
        grid=(bsz, seq // tm),
        in_specs=[pl.BlockSpec((1, tm, d), lambda b, i: (b, i, 0)),
                  pl.BlockSpec((1, d), lambda b, i: (0, 0)), vec, vec,
                  pl.BlockSpec((d, n_out), lambda b, i: (0, 0))],
        out_specs=[pl.BlockSpec((2, 1, NSA_GROUPS, tm, NSA_HD), lambda b, i: (0, b, 0, i, 0)),
                   pl.BlockSpec((4, 1, NSA_GROUPS, tm, NSA_HD), lambda b, i: (0, b, 0, i, 0))],
        out_shape=[jax.ShapeDtypeStruct((2, bsz, NSA_GROUPS, seq, NSA_HD), F32),
                   jax.ShapeDtypeStruct((4, bsz, NSA_GROUPS, seq, NSA_HD), BF16)],
        compiler_params=_params("parallel", "parallel"),
        name="nsa_kv_proj",
    )(x, g.reshape(1, d), shift, scale, kv_w.astype(BF16))


def _compress_kernel(x_ref, pos_ref, w1_ref, b1_ref, w2_ref, b2_ref, o_ref, *, n_rows):
    for g in range(NSA_GROUPS):
        first = jnp.zeros((n_rows, CMP_HIDDEN), F32)
        second = jnp.zeros((n_rows, CMP_HIDDEN), F32)
        for l in range(CMP_STRIDE):
            rows = x_ref[0, 0, g, pl.ds(l, n_rows, stride=CMP_STRIDE), :]
            first = first + _dot((rows + pos_ref[0, l:l + 1, :]).astype(BF16), w1_ref[0, l])
            second = second + _dot((rows + pos_ref[0, CMP_STRIDE + l:CMP_STRIDE + l + 1, :]).astype(BF16),
                                   w1_ref[0, CMP_STRIDE + l])
        pre = first + pltpu.roll(second, n_rows - 1, 0) + b1_ref[0]
        hid = 0.5 * pre * (1.0 + jnp.tanh(0.7978845608028654 * (pre + 0.044715 * pre * pre * pre)))
        o_ref[0, 0, g] = (_dot(hid.astype(BF16), w2_ref[0]) + b2_ref[0]).astype(BF16)


def _compress(kv_cmp, cmp_pos, cmp_w1, cmp_b1, cmp_w2, cmp_b2):
    _, bsz, _, seq, _ = kv_cmp.shape
    n_rows = seq // CMP_STRIDE
    kern = functools.partial(_compress_kernel, n_rows=n_rows)
    w1 = cmp_w1.reshape(2, CMP_LEN, NSA_HD, CMP_HIDDEN).astype(BF16)
    return pl.pallas_call(
        kern,
        grid=(2, bsz),
        in_specs=[pl.BlockSpec((1, 1, NSA_GROUPS, seq, NSA_HD), lambda t, b: (t, b, 0, 0, 0)),
                  pl.BlockSpec((1, CMP_LEN, NSA_HD), lambda t, b: (t, 0, 0)),
                  pl.BlockSpec((1, CMP_LEN, NSA_HD, CMP_HIDDEN), lambda t, b: (t, 0, 0, 0)),
                  pl.BlockSpec((1, 1, CMP_HIDDEN), lambda t, b: (t, 0, 0)),
                  pl.BlockSpec((1, CMP_HIDDEN, NSA_HD), lambda t, b: (t, 0, 0)),
                  pl.BlockSpec((1, 1, NSA_HD), lambda t, b: (t, 0, 0))],
        out_specs=pl.BlockSpec((1, 1, NSA_GROUPS, n_rows, NSA_HD), lambda t, b: (t, b, 0, 0, 0)),
        out_shape=jax.ShapeDtypeStruct((2, bsz, NSA_GROUPS, n_rows, NSA_HD), BF16),
        compiler_params=_params("parallel", "parallel"),
        name="nsa_compress",
    )(kv_cmp, cmp_pos, w1, cmp_b1[:, None, :], cmp_w2.astype(BF16), cmp_b2[:, None, :])


def _nsa_proj_kernel(x_ref, g_ref, sh_ref, sc_ref, wq_ref, wg_ref, bg_ref, q_ref, gate_ref):
    h = _norm_mod(x_ref[0], g_ref[...], sh_ref[0], sc_ref[0]).astype(BF16)
    q_ref[0] = (_dot(h, wq_ref[...]) * (NSA_HD ** -0.5)).astype(BF16)
    gate_ref[0] = jax.nn.sigmoid(_dot(h, wg_ref[...]) + bg_ref[...])


def _nsa_proj(x, g, shift, scale, w_in, b_gate):
    bsz, seq, d = x.shape
    tm = min(ROW_TILE, seq)
    nq = NSA_HEADS * NSA_HD
    ng = 3 * NSA_HEADS
    vec = pl.BlockSpec((1, 1, d), lambda b, i: (b, 0, 0))
    return pl.pallas_call(
        _nsa_proj_kernel,
        grid=(bsz, seq // tm),
        in_specs=[pl.BlockSpec((1, tm, d), lambda b, i: (b, i, 0)),
                  pl.BlockSpec((1, d), lambda b, i: (0, 0)), vec, vec,
                  pl.BlockSpec((d, nq), lambda b, i: (0, 0)),
                  pl.BlockSpec((d, ng), lambda b, i: (0, 0)),
                  pl.BlockSpec((1, ng), lambda b, i: (0, 0))],
        out_specs=[pl.BlockSpec((1, tm, nq), lambda b, i: (b, i, 0)),
                   pl.BlockSpec((1, tm, ng), lambda b, i: (b, i, 0))],
        out_shape=[jax.ShapeDtypeStruct((bsz, seq, nq), BF16),
                   jax.ShapeDtypeStruct((bsz, seq, ng), F32)],
        compiler_params=_params("parallel", "parallel"),
        name="nsa_proj",
    )(x, g.reshape(1, d), shift, scale, w_in[:, :nq].astype(BF16), w_in[:, nq:].astype(BF16),
      b_gate.reshape(1, ng))


def _nsa_attn_kernel(q_ref, gate_ref, qc_ref, slope_ref, ovl_ref, kconst_ref, cw_ref, kc_ref, vc_ref,
                     ks_ref, vs_ref, kw_ref, vw_ref, o_ref, ksa, vsa, kwa, vwa, score_s):
    qi = pl.program_id(2)
    qb = NSA_Q_BLOCK
    hd = NSA_HD
    rows = NSA_HPG * qb
    seq = ks_ref.shape[2]
    q0 = pl.multiple_of(qi * qb, qb)

    @pl.when(qi == 0)
    def _():
        ksa[...] = kconst_ref[...]
        ksa[:, 0:hd] = ks_ref[0, 0]
        one_col = (lax.broadcasted_iota(jnp.int32, (seq, 128), 1) == hd).astype(BF16)
        vsa[...] = one_col
        vsa[:, 0:hd] = vs_ref[0, 0]
        flag_col = (lax.broadcasted_iota(jnp.int32, (WIN_PAD, 128), 1) == hd).astype(BF16)
        kwa[0:WIN_PAD, :] = flag_col
        kwa[WIN_PAD:, :] = jnp.zeros((seq, 128), BF16)
        kwa[WIN_PAD:, 0:hd] = kw_ref[0, 0]
        vwa[0:WIN_PAD, :] = jnp.zeros((WIN_PAD, 128), BF16)
        vwa[WIN_PAD:, :] = one_col
        vwa[WIN_PAD:, 0:hd] = vw_ref[0, 0]

    qt = q_ref[0].astype(F32)
    q32 = jnp.concatenate([qt[:, hh * hd:(hh + 1) * hd] for hh in range(NSA_HPG)], axis=0)
    q = q32.astype(BF16)
    slope = slope_ref[0]
    t_row = q0 + lax.broadcasted_iota(jnp.int32, (rows, 1), 0) % qb

    n_cmp = kc_ref.shape[2]
    cmp_end = lax.broadcasted_iota(jnp.int32, (1, n_cmp), 1) * CMP_STRIDE + (CMP_LEN - 1)
    ok_c = cmp_end <= t_row
    s_c = jnp.where(ok_c, _dot_nt(q, kc_ref[0, 0]) + slope * cmp_end.astype(F32), NEG)
    m_c = jnp.max(s_c, axis=-1, keepdims=True)
    e_c = jnp.where(ok_c, jnp.exp(s_c - m_c), 0.0)
    l_c = jnp.sum(e_c, axis=-1, keepdims=True)
    p_c = e_c * (1.0 / jnp.where(l_c > 0.0, l_c, 1.0))
    o_c = _dot(p_c.astype(BF16), vc_ref[0, 0])

    p_sum = p_c[0:qb]
    for hh in range(1, NSA_HPG):
        p_sum = p_sum + p_c[hh * qb:(hh + 1) * qb]
    imp_t = _dot_nt(ovl_ref[...], p_sum, precision=HIGHEST)
    n_slc = imp_t.shape[0]
    blk = lax.broadcasted_iota(jnp.int32, (n_slc, 2 * qb), 0)
    forced = (blk == 0) | (blk == qi) | (blk == qi - 1)
    imp2 = jnp.concatenate([imp_t, imp_t], axis=1)
    score = jnp.where(blk <= qi, imp2 + jnp.where(forced, FORCE, 0.0), -FORCE)
    score_s[...] = score
    upper = lax.broadcasted_iota(jnp.int32, (1, 2 * qb), 1) >= qb

    def rank_step(p, rank):
        other = jnp.where(upper, score_s[pl.ds(2 * p + 1, 1), :], score_s[pl.ds(2 * p, 1), :])
        other_blk = 2 * p + upper.astype(jnp.int32)
        ahead = (other > score) | ((other == score) & (blk > other_blk))
        return rank + ahead.astype(F32)

    n_pairs = jnp.where(qi >= SLC_TOPK, qi // 2 + 1, 0)
    rank2 = lax.fori_loop(0, n_pairs, rank_step, jnp.zeros((n_slc, 2 * qb), F32))
    rank = rank2[:, 0:qb] + rank2[:, qb:2 * qb]
    sel_t = (rank < float(min(SLC_TOPK, n_slc))).astype(BF16)
    eye = (lax.broadcasted_iota(jnp.int32, (qb, qb), 0) ==
           lax.broadcasted_iota(jnp.int32, (qb, qb), 1)).astype(BF16)
    sel = _dot_nt(eye, sel_t)
    drop = (sel - 1.0) * MASK_BIG
    if n_slc < hd:
        drop = jnp.concatenate([drop, jnp.zeros((qb, hd - n_slc), F32)], axis=1)
    q_aug = jnp.concatenate([q32, jnp.concatenate([drop] * NSA_HPG, axis=0), qc_ref[0]],
                            axis=1).astype(BF16)

    tk = SLC_KEY_TILE

    def slc_tile(k0, carry, causal):
        m_i, acc = carry
        s = _dot_nt(q_aug, ksa[pl.ds(k0, tk), :])
        if causal:
            kpos = k0 + lax.broadcasted_iota(jnp.int32, (1, tk), 1)
            s = jnp.where(kpos <= t_row, s, -MASK_BIG)
        m_new = jnp.maximum(m_i, jnp.max(s, axis=-1, keepdims=True))
        p = jnp.exp(s - m_new).astype(BF16)
        acc = jnp.exp(m_i - m_new) * acc + _dot(p, vsa[pl.ds(k0, tk), :])
        return m_new, acc

    n_full = q0 // tk
    init = (jnp.full((rows, 1), -MASK_BIG, F32), jnp.zeros((rows, 128), F32))
    carry = lax.fori_loop(0, n_full, lambda kt, c: slc_tile(pl.multiple_of(kt * tk, tk), c, False), init)
    _, acc_s = slc_tile(pl.multiple_of(n_full * tk, tk), carry, True)
    o_s = acc_s[:, 0:hd] * (1.0 / acc_s[:, hd:hd + 1])

    nw = WIN_PAD + qb
    q_w = jnp.concatenate([q32, jnp.full((rows, hd), -MASK_BIG, F32)], axis=1).astype(BF16)
    s_w = _dot_nt(q_w, kwa[pl.ds(q0, nw), :]) + cw_ref[0]
    m_w = jnp.max(s_w, axis=-1, keepdims=True)
    p_w = jnp.exp(s_w - m_w).astype(BF16)
    acc_w = _dot(p_w, vwa[pl.ds(q0, nw), :])
    o_w = acc_w[:, 0:hd] * (1.0 / acc_w[:, hd:hd + 1])

    gates = gate_ref[0, 0]
    outs = []
    for hh in range(NSA_HPG):
        r0 = hh * qb
        outs.append(gates[:, hh:hh + 1] * o_c[r0:r0 + qb]
                    + gates[:, NSA_HPG + hh:NSA_HPG + hh + 1] * o_s[r0:r0 + qb]
                    + gates[:, 2 * NSA_HPG + hh:2 * NSA_HPG + hh + 1] * o_w[r0:r0 + qb])
    o_ref[0] = jnp.concatenate(outs, axis=1).astype(BF16)


def _bf16_pieces(x):
    x = np.asarray(x, np.float32)
    out = []
    for _ in range(3):
        p = x.astype(BF16).astype(np.float32)
        out.append(p)
        x = x - p
    return out


def _nsa_constants(seq):
    qb, hd = NSA_Q_BLOCK, NSA_HD
    n_cmp_rows = seq // CMP_STRIDE
    n_slc = seq // SLC_LEN
    cmp_start = np.arange(n_cmp_rows) * CMP_STRIDE
    cmp_end = cmp_start + CMP_LEN - 1
    slc_start = np.arange(n_slc) * SLC_LEN
    overlap = ((cmp_start[:, None] <= slc_start[None, :] + SLC_LEN - 1)
               & (cmp_end[:, None] >= slc_start[None, :])).astype(np.float32)
    start = 2.0 ** (-8.0 / NSA_HEADS)
    slopes = np.asarray(start ** np.arange(1, NSA_HEADS + 1), np.float32).reshape(NSA_GROUPS, NSA_HPG)
    slope_rows = np.repeat(slopes, qb, axis=1)

    pos = np.arange(seq)
    kconst = np.zeros((seq, 256), np.float32)
    kconst[pos, hd + pos // SLC_LEN] = 1.0
    kconst[:, 2 * hd + 0:2 * hd + 3] = (pos // 64 * 64)[:, None]
    kconst[:, 2 * hd + 3:2 * hd + 6] = (pos % 64)[:, None]
    qconst = np.zeros((NSA_GROUPS, NSA_HPG * qb, 128), np.float32)
    for j, piece in enumerate(_bf16_pieces(slope_rows)):
        qconst[:, :, j] = piece
        qconst[:, :, 3 + j] = piece
    dist = (np.arange(NSA_HPG * qb) % qb)[:, None] + WIN_PAD - np.arange(WIN_PAD + qb)[None, :]
    cw = np.where((dist >= 0) & (dist < WIN), -slope_rows[:, :, None] * dist[None].astype(np.float32),
                  -MASK_BIG).astype(np.float32)
    return (jnp.asarray(overlap.T), jnp.asarray(kconst, BF16), jnp.asarray(qconst),
            jnp.asarray(slope_rows[:, :, None]), jnp.asarray(cw))


def _nsa_attn(q, gates, kc, vc, kv):
    bsz, seq, _ = q.shape
    qb = NSA_Q_BLOCK
    rows = NSA_HPG * qb
    gw = NSA_HPG * NSA_HD
    n_cmp_rows = seq // CMP_STRIDE
    n_slc = seq // SLC_LEN
    nw = WIN_PAD + qb
    ovl_t, kconst, qconst, slope_rows, cw = _nsa_constants(seq)
    cmp_spec = lambda: pl.BlockSpec((1, 1, n_cmp_rows, NSA_HD), lambda b, g, i: (b, g, 0, 0))
    kv_spec = lambda t: pl.BlockSpec((None, 1, 1, seq, NSA_HD), lambda b, g, i, t=t: (t, b, g, 0, 0))
    return pl.pallas_call(
        _nsa_attn_kernel,
        grid=(bsz, NSA_GROUPS, seq // qb),
        in_specs=[pl.BlockSpec((1, qb, gw), lambda b, g, i: (b, i, g)),
                  pl.BlockSpec((1, 1, qb, 3 * NSA_HPG), lambda b, g, i: (b, g, i, 0)),
                  pl.BlockSpec((1, rows, 128), lambda b, g, i: (g, 0, 0)),
                  pl.BlockSpec((1, rows, 1), lambda b, g, i: (g, 0, 0)),
                  pl.BlockSpec((n_slc, n_cmp_rows), lambda b, g, i: (0, 0)),
                  pl.BlockSpec((seq, 256), lambda b, g, i: (0, 0)),
                  pl.BlockSpec((1, rows, nw), lambda b, g, i: (g, 0, 0)),
                  cmp_spec(), cmp_spec(), kv_spec(0), kv_spec(1), kv_spec(2), kv_spec(3)],
        out_specs=pl.BlockSpec((1, qb, gw), lambda b, g, i: (b, i, g)),
        out_shape=jax.ShapeDtypeStruct((bsz, seq, NSA_HEADS * NSA_HD), BF16),
        scratch_shapes=[pltpu.VMEM((seq, 256), BF16), pltpu.VMEM((seq, 128), BF16),
                        pltpu.VMEM((WIN_PAD + seq, 128), BF16), pltpu.VMEM((WIN_PAD + seq, 128), BF16),
                        pltpu.VMEM((n_slc, 2 * qb), F32)],
        compiler_params=_params("parallel", "parallel", "arbitrary"),
        name="nsa_attn",
    )(q, gates, qconst, slope_rows, ovl_t, kconst, cw, kc, vc, kv, kv, kv, kv)


def _nsa_mixer(x, norm_g, shift, scale, shared, w_in, b_gate):
    kcv, kv = shared
    bsz, seq, _ = x.shape
    q, gates = _nsa_proj(x, norm_g, shift, scale, w_in, b_gate)
    gates = gates.reshape(bsz, seq, 3, NSA_GROUPS, NSA_HPG).transpose(0, 3, 1, 2, 4)
    gates = gates.reshape(bsz, NSA_GROUPS, seq, 3 * NSA_HPG)
    return _nsa_attn(q, gates, kcv[0], kcv[1], kv)


def _final_norm_kernel(x_ref, g_ref, o_ref):
    x = x_ref[...]
    o_ref[...] = x * lax.rsqrt(jnp.mean(x * x, axis=-1, keepdims=True) + RMS_EPS) * g_ref[...]


def _final_norm(x, g):
    bsz, seq, d = x.shape
    x2d = x.reshape(bsz * seq, d)
    tm = min(ROW_TILE, seq)
    out = pl.pallas_call(
        _final_norm_kernel,
        grid=(x2d.shape[0] // tm,),
        in_specs=[pl.BlockSpec((tm, d), lambda i: (i, 0)), pl.BlockSpec((1, d), lambda i: (0, 0))],
        out_specs=pl.BlockSpec((tm, d), lambda i: (i, 0)),
        out_shape=jax.ShapeDtypeStruct(x2d.shape, F32),
        compiler_params=_params("parallel"),
        name="final_norm",
    )(x2d, g.reshape(1, d))
    return out.reshape(bsz, seq, d)


def kernel(x, c, ada_w, ada_b, norm1_g, norm2_g, gla_w_in, gla_w_gate2, gla_b_gate, gla_norm_g, gla_w_out, kv_norm_g, kv_ada_w, kv_ada_b, kv_w, cmp_pos, cmp_w1, cmp_b1, cmp_w2, cmp_b2, nsa_w_in, nsa_b_gate, nsa_w_out, router_w, router_b, moe_w_gate_up, moe_b_gate_up, moe_w_down, moe_b_down, final_g):
    bsz, seq, d = x.shape
    mod = _ada_vectors(c, ada_w, ada_b)
    kv_mod = _ada_vectors(c, kv_ada_w[None], kv_ada_b[None])[0]
    vec = lambda m, j: m[:, None, j * d:(j + 1) * d]
    shared = None
    for layer in range(DEPTH):
        m = mod[layer]
        sh1, sc1, g1, sh2, sc2, g2 = (vec(m, j) for j in range(6))
        if layer < N_A_LAYERS:
            i = layer
            q, k, v, r, la = _gla_proj(x, norm1_g[layer], sh1, sc1, gla_w_in[i], gla_w_gate2[i], gla_b_gate[i])
            o = _gla_core(q, k, v, r, la, gla_norm_g[i])
            x = _res_matmul(o, gla_w_out[i], x, g1)
        else:
            i = layer - N_A_LAYERS
            o = _nsa_mixer(x, norm1_g[layer], sh1, sc1, shared, nsa_w_in[i], nsa_b_gate[i])
            x = _res_matmul(o, nsa_w_out[i], x, g1)
        x = _moe_layer(x, norm2_g[layer], sh2, sc2, g2, router_w[layer], router_b[layer],
                       moe_w_gate_up[layer], moe_b_gate_up[layer], moe_w_down[layer], moe_b_down[layer])
        if layer == N_A_LAYERS - 1:
            kv_cmp, kv = _kv_proj(x, kv_norm_g, vec(kv_mod, 0), vec(kv_mod, 1), kv_w)
            shared = (_compress(kv_cmp, cmp_pos, cmp_w1, cmp_b1, cmp_w2, cmp_b2), kv)
    return _final_norm(x, final_g)
```

```python
import functools

import numpy as np
import jax
import jax.numpy as jnp
from jax import lax
from jax.experimental import pallas as pl
from jax.experimental.pallas import tpu as pltpu

F32 = jnp.float32
BF16 = jnp.bfloat16
HIGHEST = lax.Precision.HIGHEST

D_MODEL = 1024
DEPTH = 4
N_A_LAYERS = DEPTH // 2
RMS_EPS = 1e-5

GLA_HEADS = 4
GLA_KEY_DIM = D_MODEL // 2
GLA_VAL_DIM = D_MODEL
GLA_HK = GLA_KEY_DIM // GLA_HEADS
GLA_HV = GLA_VAL_DIM // GLA_HEADS
GLA_RANK = 16
GLA_TAU = 16.0
GLA_CHUNK = 64

NSA_HEADS = 16
NSA_GROUPS = 4
NSA_HPG = NSA_HEADS // NSA_GROUPS
NSA_HD = D_MODEL // NSA_HEADS
CMP_LEN = 32
CMP_STRIDE = 16
CMP_HIDDEN = 2 * NSA_HD
SLC_LEN = 64
SLC_TOPK = 16
WIN = 512
NSA_Q_BLOCK = 64

N_EXPERTS = 32
TOP_K = 4
EXPERT_FF = D_MODEL
SWIGLU_LIMIT = 7.0
SWIGLU_ALPHA = 1.702

FORCE = 1e4
NEG = -1e30

VMEM_LIMIT_BYTES = 56 * 1024 * 1024

ROW_TILE = 512
EXPERT_TILE = 256
COMBINE_TILE = 512
LANES = 128
ROUTE_TILE = 512
GLA_STEP = 512
SLC_KEY_TILE = 512
NSA_STEP = 256
NSA_ROW_CHUNK = 64
WIN_PAD = WIN
MASK_BIG = 1e30


def _params(*sem):
    return pltpu.CompilerParams(dimension_semantics=sem, vmem_limit_bytes=VMEM_LIMIT_BYTES)


def _norm_mod(x, g, shift, scale):
    y = x * lax.rsqrt(jnp.mean(x * x, axis=-1, keepdims=True) + RMS_EPS)
    return (y * g) * (1.0 + scale) + shift


def _dot(a, b):
    return jnp.dot(a, b, preferred_element_type=F32)


def _store_rows(ref, x):
    for s in range(x.shape[1] // LANES):
        ref[:, s, :] = x[:, s * LANES:(s + 1) * LANES]


def _load_rows(ref):
    return jnp.concatenate([ref[:, s, :] for s in range(ref.shape[1])], axis=1)


def _dot_nt(a, b, precision=None):
    return lax.dot_general(a, b, (((1,), (1,)), ((), ())), precision=precision,
                           preferred_element_type=F32)


def _ada_kernel(c_ref, w_ref, b_ref, o_ref):
    c = c_ref[...]
    cs = c * jax.nn.sigmoid(c)
    o_ref[0] = jnp.dot(cs, w_ref[0], precision=HIGHEST, preferred_element_type=F32) + b_ref[0]


def _ada_vectors(c, w, b):
    n_l, d, m = w.shape
    bsz = c.shape[0]
    tn = 1024
    return pl.pallas_call(
        _ada_kernel,
        grid=(n_l, m // tn),
        in_specs=[
            pl.BlockSpec((bsz, d), lambda l, j: (0, 0)),
            pl.BlockSpec((1, d, tn), lambda l, j: (l, 0, j)),
            pl.BlockSpec((1, 1, tn), lambda l, j: (l, 0, j)),
        ],
        out_specs=pl.BlockSpec((1, bsz, tn), lambda l, j: (l, 0, j)),
        out_shape=jax.ShapeDtypeStruct((n_l, bsz, m), F32),
        compiler_params=_params("parallel", "parallel"),
        name="ada_vectors",
    )(c, w, b.reshape(n_l, 1, m))


def _gla_proj_kernel(x_ref, g_ref, sh_ref, sc_ref, wq_ref, wk_ref, wv_ref, wr_ref, wlr_ref,
                     wg2_ref, bg_ref, q_ref, k_ref, v_ref, r_ref, la_ref):
    h = _norm_mod(x_ref[0], g_ref[...], sh_ref[0], sc_ref[0]).astype(BF16)
    q_ref[0] = _dot(h, wq_ref[...]).astype(BF16)
    k_ref[0] = _dot(h, wk_ref[...]).astype(BF16)
    v_ref[0] = _dot(h, wv_ref[...]).astype(BF16)
    r_ref[0] = _dot(h, wr_ref[...]).astype(BF16)
    g_lr = _dot(h, wlr_ref[...])
    z = jnp.dot(g_lr, wg2_ref[...], precision=HIGHEST, preferred_element_type=F32) + bg_ref[...]
    log_sig = jnp.minimum(z, 0.0) - jnp.log(1.0 + jnp.exp(-jnp.abs(z)))
    la_ref[0] = log_sig / GLA_TAU


def _gla_proj(x, g, shift, scale, w_in, w_gate2, b_gate):
    bsz, seq, d = x.shape
    tm = min(ROW_TILE, seq)
    kd, vd = GLA_KEY_DIM, GLA_VAL_DIM
    wq = w_in[:, :kd].astype(BF16)
    wk = w_in[:, kd:2 * kd].astype(BF16)
    wv = w_in[:, 2 * kd:2 * kd + vd].astype(BF16)
    wr = w_in[:, 2 * kd + vd:2 * kd + 2 * vd].astype(BF16)
    wlr = w_in[:, 2 * kd + 2 * vd:].astype(BF16)
    full = lambda shape: pl.BlockSpec(shape, lambda b, i: (0,) * len(shape))
    row = lambda n: pl.BlockSpec((1, tm, n), lambda b, i: (b, i, 0))
    vec = pl.BlockSpec((1, 1, d), lambda b, i: (b, 0, 0))
    return pl.pallas_call(
        _gla_proj_kernel,
        grid=(bsz, seq // tm),
        in_specs=[row(d), full((1, d)), vec, vec, full((d, kd)), full((d, kd)), full((d, vd)),
                  full((d, vd)), full((d, GLA_RANK)), full((GLA_RANK, kd)), full((1, kd))],
        out_specs=[row(kd), row(kd), row(vd), row(vd), row(kd)],
        out_shape=[jax.ShapeDtypeStruct((bsz, seq, kd), BF16),
                   jax.ShapeDtypeStruct((bsz, seq, kd), BF16),
                   jax.ShapeDtypeStruct((bsz, seq, vd), BF16),
                   jax.ShapeDtypeStruct((bsz, seq, vd), BF16),
                   jax.ShapeDtypeStruct((bsz, seq, kd), F32)],
        compiler_params=_params("parallel", "parallel"),
        name="gla_proj",
    )(x, g.reshape(1, d), shift, scale, wq, wk, wv, wr, wlr, w_gate2, b_gate.reshape(1, kd))


def _gla_core_kernel(q_ref, k_ref, v_ref, r_ref, la_ref, ng_ref, o_ref, state_ref, *, n_chunks):
    @pl.when(pl.program_id(2) == 0)
    def _():
        state_ref[...] = jnp.zeros_like(state_ref)

    c_len = GLA_CHUNK
    row = lax.broadcasted_iota(jnp.int32, (c_len, c_len), 0)
    col = lax.broadcasted_iota(jnp.int32, (c_len, c_len), 1)
    causal = col <= row
    tril = causal.astype(F32)
    ng = ng_ref[...]

    def chunk(c, carry):
        c0 = pl.multiple_of(c * c_len, c_len)
        la = la_ref[0, pl.ds(c0, c_len), :]
        b = jnp.dot(tril, la, precision=HIGHEST, preferred_element_type=F32)
        q = q_ref[0, pl.ds(c0, c_len), :].astype(F32) * (GLA_HK ** -0.5)
        k = k_ref[0, pl.ds(c0, c_len), :].astype(F32)
        v = v_ref[0, pl.ds(c0, c_len), :]
        q_dec = (q * jnp.exp(b)).astype(BF16)
        k_intra = (k * jnp.exp(-b)).astype(BF16)
        b_t = b.T
        bl_t = b_t[:, c_len - 1:c_len]
        k_inter_t = (k.T * jnp.exp(bl_t - b_t)).astype(BF16)
        att = jnp.where(causal, _dot_nt(q_dec, k_intra), 0.0).astype(BF16)
        state = state_ref[...]
        o = _dot(att, v) + _dot(q_dec, state.astype(BF16))
        state_ref[...] = state * jnp.exp(bl_t) + _dot(k_inter_t, v)
        o = o * lax.rsqrt(jnp.mean(o * o, axis=-1, keepdims=True) + RMS_EPS)
        r = r_ref[0, pl.ds(c0, c_len), :].astype(F32)
        o_ref[0, pl.ds(c0, c_len), :] = ((o * ng) * (r * jax.nn.sigmoid(r))).astype(BF16)
        return carry

    lax.fori_loop(0, n_chunks, chunk, 0)


def _gla_core(q, k, v, r, la, norm_g):
    bsz, seq, _ = q.shape
    ts = min(GLA_STEP, seq)
    kern = functools.partial(_gla_core_kernel, n_chunks=ts // GLA_CHUNK)
    hk = lambda: pl.BlockSpec((1, ts, GLA_HK), lambda b, h, s: (b, s, h))
    hv = lambda: pl.BlockSpec((1, ts, GLA_HV), lambda b, h, s: (b, s, h))
    return pl.pallas_call(
        kern,
        grid=(bsz, GLA_HEADS, seq // ts),
        in_specs=[hk(), hk(), hv(), hv(), hk(), pl.BlockSpec((1, GLA_HV), lambda b, h, s: (0, h))],
        out_specs=hv(),
        out_shape=jax.ShapeDtypeStruct((bsz, seq, GLA_VAL_DIM), BF16),
        scratch_shapes=[pltpu.VMEM((GLA_HK, GLA_HV), F32)],
        compiler_params=_params("parallel", "parallel", "arbitrary"),
        name="gla_core",
    )(q, k, v, r, la, norm_g.reshape(1, GLA_VAL_DIM))


def _res_matmul_kernel(a_ref, w_ref, x_ref, gate_ref, o_ref):
    o_ref[0] = x_ref[0] + gate_ref[0] * _dot(a_ref[0], w_ref[...])


def _res_matmul(a, w, x, gate):
    bsz, seq, d = x.shape
    kdim = a.shape[-1]
    tm = min(ROW_TILE, seq)
    return pl.pallas_call(
        _res_matmul_kernel,
        grid=(bsz, seq // tm),
        in_specs=[pl.BlockSpec((1, tm, kdim), lambda b, i: (b, i, 0)),
                  pl.BlockSpec((kdim, d), lambda b, i: (0, 0)),
                  pl.BlockSpec((1, tm, d), lambda b, i: (b, i, 0)),
                  pl.BlockSpec((1, 1, d), lambda b, i: (b, 0, 0))],
        out_specs=pl.BlockSpec((1, tm, d), lambda b, i: (b, i, 0)),
        out_shape=jax.ShapeDtypeStruct((bsz, seq, d), F32),
        compiler_params=_params("parallel", "parallel"),
        name="res_matmul",
    )(a, w.astype(BF16), x, gate)


def _route_kernel(x_ref, g_ref, sh_ref, sc_ref, rwt_ref, rb_ref,
                  h_ref, idx_ref, w_ref, rank_ref, cnt_ref, run_ref):
    @pl.when(pl.program_id(0) == 0)
    def _():
        run_ref[...] = jnp.zeros_like(run_ref)

    tm = x_ref.shape[0]
    h = _norm_mod(x_ref[...], g_ref[...], sh_ref[0], sc_ref[0])
    _store_rows(h_ref, h)
    logits = _dot_nt(rwt_ref[...], h, precision=HIGHEST) + rb_ref[...]
    e_iota = lax.broadcasted_iota(jnp.int32, logits.shape, 0)
    vals, idxs, hots = [], [], []
    for _ in range(TOP_K):
        m = jnp.max(logits, axis=0, keepdims=True)
        idx = jnp.min(jnp.where(logits == m, e_iota, N_EXPERTS), axis=0, keepdims=True)
        hot = e_iota == idx
        vals.append(m)
        idxs.append(idx)
        hots.append(hot)
        logits = jnp.where(hot, -jnp.inf, logits)
    exps = [jnp.exp(v - vals[0]) for v in vals]
    denom = exps[0] + exps[1] + exps[2] + exps[3]
    sel = (hots[0] | hots[1] | hots[2] | hots[3]).astype(F32)
    s_iota = lax.broadcasted_iota(jnp.int32, (tm, tm), 0)
    t_iota = lax.broadcasted_iota(jnp.int32, (tm, tm), 1)
    before = (s_iota < t_iota).astype(BF16)
    prefix = _dot(sel.astype(BF16), before) + run_ref[:, 0:1]
    for kk in range(TOP_K):
        idx_ref[kk:kk + 1, :] = idxs[kk]
        w_ref[kk:kk + 1, :] = exps[kk] / denom
        rank_ref[kk:kk + 1, :] = jnp.sum(jnp.where(hots[kk], prefix, 0.0), axis=0,
                                         keepdims=True).astype(jnp.int32)
    run_ref[...] = run_ref[...] + jnp.sum(sel, axis=1, keepdims=True)
    cnt_ref[...] = run_ref[...].astype(jnp.int32)


def _route(x2d, g, shift, scale, router_w, router_b, seq):
    n_tok, d = x2d.shape
    tm = min(ROUTE_TILE, seq)
    per_b = seq // tm
    vec = pl.BlockSpec((1, 1, d), lambda i: (i // per_b, 0, 0))
    tok = lambda: pl.BlockSpec((TOP_K, tm), lambda i: (0, i))
    return pl.pallas_call(
        _route_kernel,
        grid=(n_tok // tm,),
        in_specs=[pl.BlockSpec((tm, d), lambda i: (i, 0)),
                  pl.BlockSpec((1, d), lambda i: (0, 0)), vec, vec,
                  pl.BlockSpec((N_EXPERTS, d), lambda i: (0, 0)),
                  pl.BlockSpec((N_EXPERTS, 1), lambda i: (0, 0))],
        out_specs=[pl.BlockSpec((tm, d // LANES, LANES), lambda i: (i, 0, 0)), tok(), tok(), tok(),
                   pl.BlockSpec((N_EXPERTS, 128), lambda i: (0, 0))],
        out_shape=[jax.ShapeDtypeStruct((n_tok, d // LANES, LANES), F32),
                   jax.ShapeDtypeStruct((TOP_K, n_tok), jnp.int32),
                   jax.ShapeDtypeStruct((TOP_K, n_tok), F32),
                   jax.ShapeDtypeStruct((TOP_K, n_tok), jnp.int32),
                   jax.ShapeDtypeStruct((N_EXPERTS, 128), jnp.int32)],
        scratch_shapes=[pltpu.VMEM((N_EXPERTS, 128), F32)],
        compiler_params=_params("arbitrary"),
        name="moe_route",
    )(x2d, g.reshape(1, d), shift, scale, router_w.T, router_b.reshape(N_EXPERTS, 1))


def _expert_kernel(te_ref, nu_ref, tok_hbm, out_hbm, h_hbm, wgu_ref, wd_ref, bg_ref, bu_ref, bd_ref, y_hbm,
                   wg_s, wu_s, wd_s, tr_s, xbuf, ybuf, tok_s, out_s, tok_sem, out_sem, g_sem, s_sem):
    i = pl.program_id(0)
    n_tiles = pl.num_programs(0)
    tm = xbuf.shape[1]
    used = i < nu_ref[0]
    fresh = (i == 0) | (te_ref[i] != te_ref[jnp.maximum(i - 1, 0)])
    slot = i % 2
    other = 1 - slot

    def tok_copy(t, sl):
        return pltpu.make_async_copy(tok_hbm.at[t], tok_s.at[sl], tok_sem.at[sl])

    def out_copy(t, sl):
        return pltpu.make_async_copy(out_hbm.at[t], out_s.at[sl], out_sem.at[sl])

    def gather_start(sl):
        for r in range(tm):
            pltpu.make_async_copy(h_hbm.at[tok_s[sl, r]], xbuf.at[sl, r], g_sem.at[sl]).start()

    def gather_wait(sl):
        pltpu.make_async_copy(h_hbm.at[pl.ds(0, tm)], xbuf.at[sl], g_sem.at[sl]).wait()

    def scatter_start(sl):
        for r in range(tm):
            pltpu.make_async_copy(ybuf.at[sl, r], y_hbm.at[out_s[sl, r]], s_sem.at[sl]).start()

    def scatter_wait(sl):
        pltpu.make_async_copy(ybuf.at[sl], y_hbm.at[pl.ds(0, tm)], s_sem.at[sl]).wait()

    @pl.when(i == 0)
    def _():
        tok_copy(0, 0).start()
        out_copy(0, 0).start()
        ybuf[1] = jnp.zeros(ybuf.shape[1:], F32)
        spare = y_hbm.shape[0] - 2 * tm
        zero_even = pltpu.make_async_copy(ybuf.at[1], y_hbm.at[pl.ds(spare, tm)], s_sem.at[0])
        zero_even.start()
        pltpu.make_async_copy(ybuf.at[1], y_hbm.at[pl.ds(spare + tm, tm)], s_sem.at[1]).start()
        tok_copy(0, 0).wait()
        gather_start(0)
        tok_copy(jnp.minimum(1, n_tiles - 1), 1).start()
        zero_even.wait()

    @pl.when(used & fresh)
    def _():
        n_slab, chunk, lanes = tr_s.shape
        half = chunk // 2
        for c in range(wgu_ref.shape[2] // chunk):
            t = wgu_ref[0, :, c * chunk:(c + 1) * chunk].T
            for j in range(n_slab):
                tr_s[j] = t[:, j * lanes:(j + 1) * lanes]
            for j in range(n_slab):
                wg_s[c * half:(c + 1) * half, j * lanes:(j + 1) * lanes] = (
                    tr_s[j, pl.ds(0, half, stride=2), :].astype(BF16))
                wu_s[c * half:(c + 1) * half, j * lanes:(j + 1) * lanes] = (
                    tr_s[j, pl.ds(1, half, stride=2), :].astype(BF16))
        wd_s[...] = wd_ref[0].astype(BF16)

    @pl.when(used)
    def _():
        nxt = jnp.minimum(i + 1, n_tiles - 1)
        tok_copy(nxt, other).wait()
        gather_start(other)
        tok_copy(jnp.minimum(i + 2, n_tiles - 1), slot).start()
        gather_wait(slot)
        xb = _load_rows(xbuf.at[slot]).astype(BF16)
        gate = jnp.minimum(_dot_nt(xb, wg_s[...]) + bg_ref[0], SWIGLU_LIMIT)
        up = jnp.clip(_dot_nt(xb, wu_s[...]) + bu_ref[0], -SWIGLU_LIMIT, SWIGLU_LIMIT)
        act = (up + 1.0) * gate * jax.nn.sigmoid(SWIGLU_ALPHA * gate)
        _store_rows(ybuf.at[slot], _dot(act.astype(BF16), wd_s[...]) + bd_ref[0])
        out_copy(i, slot).wait()
        scatter_start(slot)
        out_copy(nxt, other).start()
        scatter_wait(other)

    @pl.when(i == nu_ref[0] - 1)
    def _():
        gather_wait(other)
        scatter_wait(slot)
        tok_copy(0, slot).wait()
        out_copy(0, other).wait()


def _experts(h3d, tok_tiles, out_tiles, tile_expert, n_used, layer, w_gu, w_down, bg, bu, bd):
    n_tok = h3d.shape[0]
    d = h3d.shape[1] * h3d.shape[2]
    n_tiles, tm = tok_tiles.shape
    ff = w_down.shape[2]
    wsel = lambda i, te, nu: (te[i], 0, 0)
    lsel = lambda i, te, nu: (layer, te[i], 0, 0)
    hbm = pl.BlockSpec(memory_space=pl.ANY)
    return pl.pallas_call(
        _expert_kernel,
        grid_spec=pltpu.PrefetchScalarGridSpec(
            num_scalar_prefetch=2,
            grid=(n_tiles,),
            in_specs=[hbm, hbm, hbm,
                      pl.BlockSpec((None, 1, d, 2 * ff), lsel),
                      pl.BlockSpec((None, 1, ff, d), lsel),
                      pl.BlockSpec((1, 1, ff), wsel), pl.BlockSpec((1, 1, ff), wsel),
                      pl.BlockSpec((1, 1, d), wsel)],
            out_specs=hbm,
            scratch_shapes=[pltpu.VMEM((ff, d), BF16), pltpu.VMEM((ff, d), BF16),
                            pltpu.VMEM((ff, d), BF16), pltpu.VMEM((d // 128, 256, 128), F32),
                            pltpu.VMEM((2, tm, d // LANES, LANES), F32),
                            pltpu.VMEM((2, tm, d // LANES, LANES), F32),
                            pltpu.SMEM((2, tm), jnp.int32), pltpu.SMEM((2, tm), jnp.int32),
                            pltpu.SemaphoreType.DMA((2,)), pltpu.SemaphoreType.DMA((2,)),
                            pltpu.SemaphoreType.DMA((2,)), pltpu.SemaphoreType.DMA((2,))]),
        out_shape=jax.ShapeDtypeStruct((TOP_K * n_tok + 2 * tm, d // LANES, LANES), F32),
        compiler_params=_params("arbitrary"),
        name="moe_experts",
    )(tile_expert, n_used, tok_tiles, out_tiles, h3d, w_gu, w_down, bg, bu, bd)


def _combine_kernel(y0_ref, y1_ref, y2_ref, y3_ref, x_ref, w_ref, gate_ref, o_ref):
    w = w_ref[...]
    y = w[:, 0:1] * _load_rows(y0_ref)
    for kk, y_ref in enumerate((y1_ref, y2_ref, y3_ref), start=1):
        y = y + w[:, kk:kk + 1] * _load_rows(y_ref)
    o_ref[...] = x_ref[...] + gate_ref[0] * y


def _combine(y, x2d, w_tok, gate, seq):
    n_tok, d = x2d.shape
    tm = COMBINE_TILE
    per_b = seq // tm
    n_blk = n_tok // tm
    plane = lambda kk: pl.BlockSpec((tm, d // LANES, LANES), lambda i, kk=kk: (kk * n_blk + i, 0, 0))
    return pl.pallas_call(
        _combine_kernel,
        grid=(n_blk,),
        in_specs=[plane(0), plane(1), plane(2), plane(3),
                  pl.BlockSpec((tm, d), lambda i: (i, 0)),
                  pl.BlockSpec((tm, TOP_K), lambda i: (i, 0)),
                  pl.BlockSpec((1, 1, d), lambda i: (i // per_b, 0, 0))],
        out_specs=pl.BlockSpec((tm, d), lambda i: (i, 0)),
        out_shape=jax.ShapeDtypeStruct((n_tok, d), F32),
        compiler_params=_params("parallel"),
        name="moe_combine",
    )(y, y, y, y, x2d, w_tok, gate)


def _moe_layer(x, norm_g, shift, scale, gate, router_w, router_b, layer, w_gu, b_gu, w_down, b_down):
    bsz, seq, d = x.shape
    n_tok = bsz * seq
    x2d = x.reshape(n_tok, d)
    h3d, idx_t, w_t, rank_t, counts = _route(x2d, norm_g, shift, scale, router_w, router_b, seq)

    counts = counts[:, 0]
    te = EXPERT_TILE
    padded = (counts + te - 1) // te * te
    ends = jnp.cumsum(padded)
    starts = ends - padded
    cap = n_tok * TOP_K + N_EXPERTS * te
    n_tiles = cap // te
    tile_row0 = jnp.arange(n_tiles, dtype=jnp.int32) * te
    tile_expert = jnp.minimum(jnp.sum((ends[None, :] <= tile_row0[:, None]).astype(jnp.int32), axis=1),
                              N_EXPERTS - 1).astype(jnp.int32)
    n_used = (ends[-1:] // te).astype(jnp.int32)
    e_ids = jnp.arange(N_EXPERTS, dtype=jnp.int32)[:, None, None]
    start_of = jnp.sum(jnp.where(idx_t[None] == e_ids, starts.astype(jnp.int32)[:, None, None], 0), axis=0)
    dest = start_of + rank_t
    n_asg = TOP_K * n_tok
    out_row = (jnp.arange(TOP_K, dtype=jnp.int32)[:, None] * n_tok
               + jnp.arange(n_tok, dtype=jnp.int32)[None, :])
    row = jnp.arange(cap, dtype=jnp.int32)
    spare = n_asg + (row // te % 2) * te + row % te
    out_rows = spare.at[dest.reshape(-1)].set(out_row.reshape(-1), unique_indices=True)
    tok_rows = jnp.where(out_rows < n_asg, out_rows % n_tok, 0)

    y = _experts(h3d, tok_rows.reshape(n_tiles, te), out_rows.reshape(n_tiles, te), tile_expert, n_used,
                 layer, w_gu, w_down, b_gu[:, None, 0::2], b_gu[:, None, 1::2], b_down[:, None, :])
    out = _combine(y, x2d, w_t.T, gate, seq)
    return out.reshape(bsz, seq, d)


def _kv_proj_kernel(x_ref, g_ref, sh_ref, sc_ref, w_ref, cmp_ref, kv_ref):
    h = _norm_mod(x_ref[0], g_ref[...], sh_ref[0], sc_ref[0]).astype(BF16)
    kv = _dot(h, w_ref[...])
    for t in range(6):
        for g in range(NSA_GROUPS):
            c0 = (t * NSA_GROUPS + g) * NSA_HD
            piece = kv[:, c0:c0 + NSA_HD]
            if t < 2:
                cmp_ref[t, 0, g] = piece
            else:
                kv_ref[t - 2, 0, g] = piece.astype(BF16)


def _kv_proj(x, g, shift, scale, kv_w):
    bsz, seq, d = x.shape
    tm = min(ROW_TILE, seq)
    n_out = kv_w.shape[1]
    vec = pl.BlockSpec((1, 1, d), lambda b, i: (b, 0, 0))
    return pl.pallas_call(
        _kv_proj_kernel,
        grid=(bsz, seq // tm),
        in_specs=[pl.BlockSpec((1, tm, d), lambda b, i: (b, i, 0)),
                  pl.BlockSpec((1, d), lambda b, i: (0, 0)), vec, vec,
                  pl.BlockSpec((d, n_out), lambda b, i: (0, 0))],
        out_specs=[pl.BlockSpec((2, 1, NSA_GROUPS, tm, NSA_HD), lambda b, i: (0, b, 0, i, 0)),
                   pl.BlockSpec((4, 1, NSA_GROUPS, tm, NSA_HD), lambda b, i: (0, b, 0, i, 0))],
        out_shape=[jax.ShapeDtypeStruct((2, bsz, NSA_GROUPS, seq, NSA_HD), F32),
                   jax.ShapeDtypeStruct((4, bsz, NSA_GROUPS, seq, NSA_HD), BF16)],
        compiler_params=_params("parallel", "parallel"),
        name="nsa_kv_proj",
    )(x, g.reshape(1, d), shift, scale, kv_w.astype(BF16))


def _compress_kernel(x_ref, pos_ref, w1_ref, b1_ref, w2_ref, b2_ref, o_ref, *, n_rows):
    for g in range(NSA_GROUPS):
        first = jnp.zeros((n_rows, CMP_HIDDEN), F32)
        second = jnp.zeros((n_rows, CMP_HIDDEN), F32)
        for l in range(CMP_STRIDE):
            rows = x_ref[0, 0, g, pl.ds(l, n_rows, stride=CMP_STRIDE), :]
            first = first + _dot((rows + pos_ref[0, l:l + 1, :]).astype(BF16), w1_ref[0, l])
            second = second + _dot((rows + pos_ref[0, CMP_STRIDE + l:CMP_STRIDE + l + 1, :]).astype(BF16),
                                   w1_ref[0, CMP_STRIDE + l])
        pre = first + pltpu.roll(second, n_rows - 1, 0) + b1_ref[0]
        hid = 0.5 * pre * (1.0 + jnp.tanh(0.7978845608028654 * (pre + 0.044715 * pre * pre * pre)))
        o_ref[0, 0, g] = (_dot(hid.astype(BF16), w2_ref[0]) + b2_ref[0]).astype(BF16)


def _compress(kv_cmp, cmp_pos, cmp_w1, cmp_b1, cmp_w2, cmp_b2):
    _, bsz, _, seq, _ = kv_cmp.shape
    n_rows = seq // CMP_STRIDE
    kern = functools.partial(_compress_kernel, n_rows=n_rows)
    w1 = cmp_w1.reshape(2, CMP_LEN, NSA_HD, CMP_HIDDEN).astype(BF16)
    return pl.pallas_call(
        kern,
        grid=(2, bsz),
        in_specs=[pl.BlockSpec((1, 1, NSA_GROUPS, seq, NSA_HD), lambda t, b: (t, b, 0, 0, 0)),
                  pl.BlockSpec((1, CMP_LEN, NSA_HD), lambda t, b: (t, 0, 0)),
                  pl.BlockSpec((1, CMP_LEN, NSA_HD, CMP_HIDDEN), lambda t, b: (t, 0, 0, 0)),
                  pl.BlockSpec((1, 1, CMP_HIDDEN), lambda t, b: (t, 0, 0)),
                  pl.BlockSpec((1, CMP_HIDDEN, NSA_HD), lambda t, b: (t, 0, 0)),
                  pl.BlockSpec((1, 1, NSA_HD), lambda t, b: (t, 0, 0))],
        out_specs=pl.BlockSpec((1, 1, NSA_GROUPS, n_rows, NSA_HD), lambda t, b: (t, b, 0, 0, 0)),
        out_shape=jax.ShapeDtypeStruct((2, bsz, NSA_GROUPS, n_rows, NSA_HD), BF16),
        compiler_params=_params("parallel", "parallel"),
        name="nsa_compress",
    )(kv_cmp, cmp_pos, w1, cmp_b1[:, None, :], cmp_w2.astype(BF16), cmp_b2[:, None, :])


def _nsa_proj_kernel(x_ref, g_ref, sh_ref, sc_ref, wq_ref, wg_ref, bg_ref, q_ref, gate_ref):
    h = _norm_mod(x_ref[0], g_ref[...], sh_ref[0], sc_ref[0]).astype(BF16)
    q_ref[0] = (_dot(h, wq_ref[...]) * (NSA_HD ** -0.5)).astype(BF16)
    gate_ref[0] = jax.nn.sigmoid(_dot(h, wg_ref[...]) + bg_ref[...])


def _nsa_proj(x, g, shift, scale, w_in, b_gate):
    bsz, seq, d = x.shape
    tm = min(ROW_TILE, seq)
    nq = NSA_HEADS * NSA_HD
    ng = 3 * NSA_HEADS
    vec = pl.BlockSpec((1, 1, d), lambda b, i: (b, 0, 0))
    return pl.pallas_call(
        _nsa_proj_kernel,
        grid=(bsz, seq // tm),
        in_specs=[pl.BlockSpec((1, tm, d), lambda b, i: (b, i, 0)),
                  pl.BlockSpec((1, d), lambda b, i: (0, 0)), vec, vec,
                  pl.BlockSpec((d, nq), lambda b, i: (0, 0)),
                  pl.BlockSpec((d, ng), lambda b, i: (0, 0)),
                  pl.BlockSpec((1, ng), lambda b, i: (0, 0))],
        out_specs=[pl.BlockSpec((1, tm, nq), lambda b, i: (b, i, 0)),
                   pl.BlockSpec((1, tm, ng), lambda b, i: (b, i, 0))],
        out_shape=[jax.ShapeDtypeStruct((bsz, seq, nq), BF16),
                   jax.ShapeDtypeStruct((bsz, seq, ng), F32)],
        compiler_params=_params("parallel", "parallel"),
        name="nsa_proj",
    )(x, g.reshape(1, d), shift, scale, w_in[:, :nq].astype(BF16), w_in[:, nq:].astype(BF16),
      b_gate.reshape(1, ng))


def _nsa_attn_kernel(q_ref, gate_ref, qc_ref, slope_ref, ovl_ref, kconst_ref, cw_ref, kc_ref, vc_ref,
                     ks_ref, vs_ref, kw_ref, vw_ref, o_ref, ksa, vsa, kwa, vwa, score_s,
                     sc_s, pc_s, psum_s, qa_s, m_s, acc_s, sa_s, sb_s, ps_s, sw_s, pw_s):
    qi = pl.program_id(2)
    qs = NSA_STEP
    hd = NSA_HD
    rows = NSA_HPG * qs
    seq = ks_ref.shape[2]
    q0 = pl.multiple_of(qi * qs, qs)

    @pl.when(qi == 0)
    def _():
        ksa[...] = kconst_ref[...]
        ksa[:, 0:hd] = ks_ref[0, 0]
        one_col = (lax.broadcasted_iota(jnp.int32, (seq, 128), 1) == hd).astype(BF16)
        vsa[...] = one_col
        vsa[:, 0:hd] = vs_ref[0, 0]
        flag_col = (lax.broadcasted_iota(jnp.int32, (WIN_PAD, 128), 1) == hd).astype(BF16)
        kwa[0:WIN_PAD, :] = flag_col
        kwa[WIN_PAD:, :] = jnp.zeros((seq, 128), BF16)
        kwa[WIN_PAD:, 0:hd] = kw_ref[0, 0]
        vwa[0:WIN_PAD, :] = jnp.zeros((WIN_PAD, 128), BF16)
        vwa[WIN_PAD:, :] = one_col
        vwa[WIN_PAD:, 0:hd] = vw_ref[0, 0]

    qt = q_ref[0].astype(F32)
    q32 = jnp.concatenate([qt[:, hh * hd:(hh + 1) * hd] for hh in range(NSA_HPG)], axis=0)
    q = q32.astype(BF16)
    slope = slope_ref[0]
    t_row = q0 + lax.broadcasted_iota(jnp.int32, (rows, 1), 0) % qs

    n_cmp = kc_ref.shape[2]
    cmp_end = lax.broadcasted_iota(jnp.int32, (1, n_cmp), 1) * CMP_STRIDE + (CMP_LEN - 1)
    cmp_bias = cmp_end.astype(F32)
    sc_s[...] = _dot_nt(q, kc_ref[0, 0])
    rc = NSA_ROW_CHUNK
    for c in range(rows // rc):
        r = slice(c * rc, (c + 1) * rc)
        ok_c = cmp_end <= t_row[r]
        s_c = jnp.where(ok_c, sc_s[r, :] + slope[r] * cmp_bias, NEG)
        m_c = jnp.max(s_c, axis=-1, keepdims=True)
        e_c = jnp.where(ok_c, jnp.exp(s_c - m_c), 0.0)
        l_c = jnp.sum(e_c, axis=-1, keepdims=True)
        p_c = e_c * (1.0 / jnp.where(l_c > 0.0, l_c, 1.0))
        pc_s[r, :] = p_c.astype(BF16)
        qr = slice((c * rc) % qs, (c * rc) % qs + rc)
        if c * rc < qs:
            psum_s[qr, :] = p_c
        else:
            psum_s[qr, :] = psum_s[qr, :] + p_c
    o_c = _dot(pc_s[...], vc_ref[0, 0])

    imp_t = _dot_nt(ovl_ref[...], psum_s[...], precision=HIGHEST)
    n_slc = imp_t.shape[0]
    blk = lax.broadcasted_iota(jnp.int32, (n_slc, qs), 0)
    cur = (q0 + lax.broadcasted_iota(jnp.int32, (1, qs), 1)) // SLC_LEN
    forced = (blk == 0) | (blk == cur) | (blk == cur - 1)
    score = jnp.where(blk <= cur, imp_t + jnp.where(forced, FORCE, 0.0), -FORCE)
    score_s[...] = score

    def rank_step(ip, rank):
        for i in (2 * ip, 2 * ip + 1):
            other = score_s[pl.ds(i, 1), :]
            ahead = (other > score) | ((other == score) & (blk > i))
            rank = rank + ahead.astype(F32)
        return rank

    last_blk = (q0 + qs - 1) // SLC_LEN
    n_rank = jnp.where(last_blk >= SLC_TOPK, last_blk // 2 + 1, 0)
    rank = lax.fori_loop(0, n_rank, rank_step, jnp.zeros((n_slc, qs), F32))
    sel_t = (rank < float(min(SLC_TOPK, n_slc))).astype(BF16)
    eye = (lax.broadcasted_iota(jnp.int32, (qs, qs), 0) ==
           lax.broadcasted_iota(jnp.int32, (qs, qs), 1)).astype(BF16)
    sel = _dot_nt(eye, sel_t)
    drop = (sel - 1.0) * MASK_BIG
    if n_slc < hd:
        drop = jnp.concatenate([drop, jnp.zeros((qs, hd - n_slc), F32)], axis=1)
    q_aug = jnp.concatenate([q32, jnp.concatenate([drop] * NSA_HPG, axis=0), qc_ref[0]],
                            axis=1).astype(BF16)

    tk = SLC_KEY_TILE

    n_full = q0 // tk
    qa_s[...] = q_aug
    m_s[...] = jnp.full((rows, 1), -MASK_BIG, F32)
    acc_s[...] = jnp.zeros((rows, 128), F32)

    def scores(kt, s_out):
        k0 = pl.multiple_of(kt * tk, tk)
        s_out[...] = _dot_nt(qa_s[...], ksa[pl.ds(k0, tk), :])

    def absorb(s_in, kt, causal):
        k0 = pl.multiple_of(kt * tk, tk)
        kpos = k0 + lax.broadcasted_iota(jnp.int32, (1, tk), 1)
        for c in range(rows // rc):
            r = slice(c * rc, (c + 1) * rc)
            s = s_in[r, :]
            if causal:
                s = jnp.where(kpos <= t_row[r], s, -MASK_BIG)
            m_old = m_s[r, :]
            m_new = jnp.maximum(m_old, jnp.max(s, axis=-1, keepdims=True))
            m_s[r, :] = m_new
            ps_s[r, :] = jnp.exp(s - m_new).astype(BF16)
            acc_s[r, :] = jnp.exp(m_old - m_new) * acc_s[r, :]
        acc_s[...] = acc_s[...] + _dot(ps_s[...], vsa[pl.ds(k0, tk), :])

    scores(0, sa_s)

    def slc_pair(j, carry):
        scores(2 * j + 1, sb_s)
        absorb(sa_s, 2 * j, False)

        @pl.when(2 * j + 1 < n_full)
        def _():
            scores(2 * j + 2, sa_s)
            absorb(sb_s, 2 * j + 1, False)
        return carry

    lax.fori_loop(0, (n_full + 1) // 2, slc_pair, 0)

    @pl.when(n_full % 2 == 0)
    def _():
        absorb(sa_s, n_full, True)

    @pl.when(n_full % 2 == 1)
    def _():
        absorb(sb_s, n_full, True)

    o_s = acc_s[:, 0:hd] * (1.0 / acc_s[:, hd:hd + 1])

    nw = WIN_PAD + qs
    q_w = jnp.concatenate([q32, jnp.full((rows, hd), -MASK_BIG, F32)], axis=1).astype(BF16)
    sw_s[...] = _dot_nt(q_w, kwa[pl.ds(q0, nw), :])
    for c in range(rows // rc):
        r = slice(c * rc, (c + 1) * rc)
        s_w = sw_s[r, :] + cw_ref[0, r, :]
        m_w = jnp.max(s_w, axis=-1, keepdims=True)
        pw_s[r, :] = jnp.exp(s_w - m_w).astype(BF16)
    acc_w = _dot(pw_s[...], vwa[pl.ds(q0, nw), :])
    o_w = acc_w[:, 0:hd] * (1.0 / acc_w[:, hd:hd + 1])

    gates = gate_ref[0, 0]
    outs = []
    for hh in range(NSA_HPG):
        r0 = hh * qs
        outs.append(gates[:, hh:hh + 1] * o_c[r0:r0 + qs]
                    + gates[:, NSA_HPG + hh:NSA_HPG + hh + 1] * o_s[r0:r0 + qs]
                    + gates[:, 2 * NSA_HPG + hh:2 * NSA_HPG + hh + 1] * o_w[r0:r0 + qs])
    o_ref[0] = jnp.concatenate(outs, axis=1).astype(BF16)


def _bf16_pieces(x):
    x = np.asarray(x, np.float32)
    out = []
    for _ in range(3):
        p = x.astype(BF16).astype(np.float32)
        out.append(p)
        x = x - p
    return out


def _nsa_constants(seq):
    qb, hd = NSA_STEP, NSA_HD
    n_cmp_rows = seq // CMP_STRIDE
    n_slc = seq // SLC_LEN
    cmp_start = np.arange(n_cmp_rows) * CMP_STRIDE
    cmp_end = cmp_start + CMP_LEN - 1
    slc_start = np.arange(n_slc) * SLC_LEN
    overlap = ((cmp_start[:, None] <= slc_start[None, :] + SLC_LEN - 1)
               & (cmp_end[:, None] >= slc_start[None, :])).astype(np.float32)
    start = 2.0 ** (-8.0 / NSA_HEADS)
    slopes = np.asarray(start ** np.arange(1, NSA_HEADS + 1), np.float32).reshape(NSA_GROUPS, NSA_HPG)
    slope_rows = np.repeat(slopes, qb, axis=1)

    pos = np.arange(seq)
    kconst = np.zeros((seq, 256), np.float32)
    kconst[pos, hd + pos // SLC_LEN] = 1.0
    kconst[:, 2 * hd + 0:2 * hd + 3] = (pos // 64 * 64)[:, None]
    kconst[:, 2 * hd + 3:2 * hd + 6] = (pos % 64)[:, None]
    qconst = np.zeros((NSA_GROUPS, NSA_HPG * qb, 128), np.float32)
    for j, piece in enumerate(_bf16_pieces(slope_rows)):
        qconst[:, :, j] = piece
        qconst[:, :, 3 + j] = piece
    dist = (np.arange(NSA_HPG * qb) % qb)[:, None] + WIN_PAD - np.arange(WIN_PAD + qb)[None, :]
    cw = np.where((dist >= 0) & (dist < WIN), -slope_rows[:, :, None] * dist[None].astype(np.float32),
                  -MASK_BIG).astype(np.float32)
    return (jnp.asarray(overlap.T), jnp.asarray(kconst, BF16), jnp.asarray(qconst),
            jnp.asarray(slope_rows[:, :, None]), jnp.asarray(cw))


def _nsa_attn(q, gates, kc, vc, kv):
    bsz, seq, _ = q.shape
    qb = NSA_STEP
    rows = NSA_HPG * qb
    gw = NSA_HPG * NSA_HD
    n_cmp_rows = seq // CMP_STRIDE
    n_slc = seq // SLC_LEN
    nw = WIN_PAD + qb
    ovl_t, kconst, qconst, slope_rows, cw = _nsa_constants(seq)
    cmp_spec = lambda: pl.BlockSpec((1, 1, n_cmp_rows, NSA_HD), lambda b, g, i: (b, g, 0, 0))
    kv_spec = lambda t: pl.BlockSpec((None, 1, 1, seq, NSA_HD), lambda b, g, i, t=t: (t, b, g, 0, 0))
    return pl.pallas_call(
        _nsa_attn_kernel,
        grid=(bsz, NSA_GROUPS, seq // qb),
        in_specs=[pl.BlockSpec((1, qb, gw), lambda b, g, i: (b, i, g)),
                  pl.BlockSpec((1, 1, qb, 3 * NSA_HPG), lambda b, g, i: (b, g, i, 0)),
                  pl.BlockSpec((1, rows, 128), lambda b, g, i: (g, 0, 0)),
                  pl.BlockSpec((1, rows, 1), lambda b, g, i: (g, 0, 0)),
                  pl.BlockSpec((n_slc, n_cmp_rows), lambda b, g, i: (0, 0)),
                  pl.BlockSpec((seq, 256), lambda b, g, i: (0, 0)),
                  pl.BlockSpec((1, rows, nw), lambda b, g, i: (g, 0, 0)),
                  cmp_spec(), cmp_spec(), kv_spec(0), kv_spec(1), kv_spec(2), kv_spec(3)],
        out_specs=pl.BlockSpec((1, qb, gw), lambda b, g, i: (b, i, g)),
        out_shape=jax.ShapeDtypeStruct((bsz, seq, NSA_HEADS * NSA_HD), BF16),
        scratch_shapes=[pltpu.VMEM((seq, 256), BF16), pltpu.VMEM((seq, 128), BF16),
                        pltpu.VMEM((WIN_PAD + seq, 128), BF16), pltpu.VMEM((WIN_PAD + seq, 128), BF16),
                        pltpu.VMEM((n_slc, qb), F32),
                        pltpu.VMEM((rows, n_cmp_rows), F32), pltpu.VMEM((rows, n_cmp_rows), BF16),
                        pltpu.VMEM((qb, n_cmp_rows), F32), pltpu.VMEM((rows, 256), BF16),
                        pltpu.VMEM((rows, 1), F32), pltpu.VMEM((rows, 128), F32),
                        pltpu.VMEM((rows, SLC_KEY_TILE), F32), pltpu.VMEM((rows, SLC_KEY_TILE), F32),
                        pltpu.VMEM((rows, SLC_KEY_TILE), BF16),
                        pltpu.VMEM((rows, nw), F32), pltpu.VMEM((rows, nw), BF16)],
        compiler_params=_params("parallel", "parallel", "arbitrary"),
        name="nsa_attn",
    )(q, gates, qconst, slope_rows, ovl_t, kconst, cw, kc, vc, kv, kv, kv, kv)


def _nsa_mixer(x, norm_g, shift, scale, shared, w_in, b_gate):
    kcv, kv = shared
    bsz, seq, _ = x.shape
    q, gates = _nsa_proj(x, norm_g, shift, scale, w_in, b_gate)
    gates = gates.reshape(bsz, seq, 3, NSA_GROUPS, NSA_HPG).transpose(0, 3, 1, 2, 4)
    gates = gates.reshape(bsz, NSA_GROUPS, seq, 3 * NSA_HPG)
    return _nsa_attn(q, gates, kcv[0], kcv[1], kv)


def _final_norm_kernel(x_ref, g_ref, o_ref):
    x = x_ref[...]
    o_ref[...] = x * lax.rsqrt(jnp.mean(x * x, axis=-1, keepdims=True) + RMS_EPS) * g_ref[...]


def _final_norm(x, g):
    bsz, seq, d = x.shape
    x2d = x.reshape(bsz * seq, d)
    tm = min(ROW_TILE, seq)
    out = pl.pallas_call(
        _final_norm_kernel,
        grid=(x2d.shape[0] // tm,),
        in_specs=[pl.BlockSpec((tm, d), lambda i: (i, 0)), pl.BlockSpec((1, d), lambda i: (0, 0))],
        out_specs=pl.BlockSpec((tm, d), lambda i: (i, 0)),
        out_shape=jax.ShapeDtypeStruct(x2d.shape, F32),
        compiler_params=_params("parallel"),
        name="final_norm",
    )(x2d, g.reshape(1, d))
    return out.reshape(bsz, seq, d)


def kernel(x, c, ada_w, ada_b, norm1_g, norm2_g, gla_w_in, gla_w_gate2, gla_b_gate, gla_norm_g, gla_w_out, kv_norm_g, kv_ada_w, kv_ada_b, kv_w, cmp_pos, cmp_w1, cmp_b1, cmp_w2, cmp_b2, nsa_w_in, nsa_b_gate, nsa_w_out, router_w, router_b, moe_w_gate_up, moe_b_gate_up, moe_w_down, moe_b_down, final_g):
    bsz, seq, d = x.shape
    mod = _ada_vectors(c, ada_w, ada_b)
    kv_mod = _ada_vectors(c, kv_ada_w[None], kv_ada_b[None])[0]
    vec = lambda m, j: m[:, None, j * d:(j + 1) * d]
    shared = None
    for layer in range(DEPTH):
        m = mod[layer]
        sh1, sc1, g1, sh2, sc2, g2 = (vec(m, j) for j in range(6))
        if layer < N_A_LAYERS:
            i = layer
            q, k, v, r, la = _gla_proj(x, norm1_g[layer], sh1, sc1, gla_w_in[i], gla_w_gate2[i], gla_b_gate[i])
            o = _gla_core(q, k, v, r, la, gla_norm_g[i])
            x = _res_matmul(o, gla_w_out[i], x, g1)
        else:
            i = layer - N_A_LAYERS
            o = _nsa_mixer(x, norm1_g[layer], sh1, sc1, shared, nsa_w_in[i], nsa_b_gate[i])
            x = _res_matmul(o, nsa_w_out[i], x, g1)
        x = _moe_layer(x, norm2_g[layer], sh2, sc2, g2, router_w[layer], router_b[layer],
                       layer, moe_w_gate_up, moe_b_gate_up[layer], moe_w_down, moe_b_down[layer])
        if layer == N_A_LAYERS - 1:
            kv_cmp, kv = _kv_proj(x, kv_norm_g, vec(kv_mod, 0), vec(kv_mod, 1), kv_w)
            shared = (_compress(kv_cmp, cmp_pos, cmp_w1, cmp_b1, cmp_w2, cmp_b2), kv)
    return _final_norm(x, final_g)
```

```python
import functools

import numpy as np
import jax
import jax.numpy as jnp
from jax import lax
from jax.experimental import pallas as pl
from jax.experimental.pallas import tpu as pltpu

F32 = jnp.float32
BF16 = jnp.bfloat16
HIGHEST = lax.Precision.HIGHEST

D_MODEL = 1024
DEPTH = 4
N_A_LAYERS = DEPTH // 2
RMS_EPS = 1e-5

GLA_HEADS = 4
GLA_KEY_DIM = D_MODEL // 2
GLA_VAL_DIM = D_MODEL
GLA_HK = GLA_KEY_DIM // GLA_HEADS
GLA_HV = GLA_VAL_DIM // GLA_HEADS
GLA_RANK = 16
GLA_TAU = 16.0
GLA_CHUNK = 64

NSA_HEADS = 16
NSA_GROUPS = 4
NSA_HPG = NSA_HEADS // NSA_GROUPS
NSA_HD = D_MODEL // NSA_HEADS
CMP_LEN = 32
CMP_STRIDE = 16
CMP_HIDDEN = 2 * NSA_HD
SLC_LEN = 64
SLC_TOPK = 16
WIN = 512
NSA_Q_BLOCK = 64

N_EXPERTS = 32
TOP_K = 4
EXPERT_FF = D_MODEL
SWIGLU_LIMIT = 7.0
SWIGLU_ALPHA = 1.702

FORCE = 1e4
NEG = -1e30

VMEM_LIMIT_BYTES = 56 * 1024 * 1024

ROW_TILE = 512
EXPERT_TILE = 256
COMBINE_TILE = 512
LANES = 128
ROW_CHUNK = 8
ROUTE_TILE = 512
GLA_STEP = 512
SLC_KEY_TILE = 512
NSA_STEP = 256
NSA_ROW_CHUNK = 64
WIN_PAD = WIN
MASK_BIG = 1e30


def _params(*sem):
    return pltpu.CompilerParams(dimension_semantics=sem, vmem_limit_bytes=VMEM_LIMIT_BYTES)


def _norm_mod(x, g, shift, scale):
    y = x * lax.rsqrt(jnp.mean(x * x, axis=-1, keepdims=True) + RMS_EPS)
    return (y * g) * (1.0 + scale) + shift


def _dot(a, b):
    return jnp.dot(a, b, preferred_element_type=F32)


def _store_rows(ref, x):
    for s in range(x.shape[1] // LANES):
        ref[:, s, :] = x[:, s * LANES:(s + 1) * LANES]


def _load_rows(ref):
    return jnp.concatenate([ref[:, s, :] for s in range(ref.shape[1])], axis=1)


def _dot_nt(a, b, precision=None):
    return lax.dot_general(a, b, (((1,), (1,)), ((), ())), precision=precision,
                           preferred_element_type=F32)


def _ada_kernel(c_ref, w_ref, b_ref, o_ref):
    c = c_ref[...]
    cs = c * jax.nn.sigmoid(c)
    o_ref[0] = jnp.dot(cs, w_ref[0], precision=HIGHEST, preferred_element_type=F32) + b_ref[0]


def _ada_vectors(c, w, b):
    n_l, d, m = w.shape
    bsz = c.shape[0]
    tn = 1024
    return pl.pallas_call(
        _ada_kernel,
        grid=(n_l, m // tn),
        in_specs=[
            pl.BlockSpec((bsz, d), lambda l, j: (0, 0)),
            pl.BlockSpec((1, d, tn), lambda l, j: (l, 0, j)),
            pl.BlockSpec((1, 1, tn), lambda l, j: (l, 0, j)),
        ],
        out_specs=pl.BlockSpec((1, bsz, tn), lambda l, j: (l, 0, j)),
        out_shape=jax.ShapeDtypeStruct((n_l, bsz, m), F32),
        compiler_params=_params("parallel", "parallel"),
        name="ada_vectors",
    )(c, w, b.reshape(n_l, 1, m))


def _gla_proj_kernel(x_ref, g_ref, sh_ref, sc_ref, wq_ref, wk_ref, wv_ref, wr_ref, wlr_ref,
                     wg2_ref, bg_ref, q_ref, k_ref, v_ref, r_ref, la_ref):
    h = _norm_mod(x_ref[0], g_ref[...], sh_ref[0], sc_ref[0]).astype(BF16)
    q_ref[0] = _dot(h, wq_ref[...]).astype(BF16)
    k_ref[0] = _dot(h, wk_ref[...]).astype(BF16)
    v_ref[0] = _dot(h, wv_ref[...]).astype(BF16)
    r_ref[0] = _dot(h, wr_ref[...]).astype(BF16)
    g_lr = _dot(h, wlr_ref[...])
    z = jnp.dot(g_lr, wg2_ref[...], precision=HIGHEST, preferred_element_type=F32) + bg_ref[...]
    log_sig = jnp.minimum(z, 0.0) - jnp.log(1.0 + jnp.exp(-jnp.abs(z)))
    la_ref[0] = log_sig / GLA_TAU


def _gla_proj(x, g, shift, scale, w_in, w_gate2, b_gate):
    bsz, seq, d = x.shape
    tm = min(ROW_TILE, seq)
    kd, vd = GLA_KEY_DIM, GLA_VAL_DIM
    wq = w_in[:, :kd].astype(BF16)
    wk = w_in[:, kd:2 * kd].astype(BF16)
    wv = w_in[:, 2 * kd:2 * kd + vd].astype(BF16)
    wr = w_in[:, 2 * kd + vd:2 * kd + 2 * vd].astype(BF16)
    wlr = w_in[:, 2 * kd + 2 * vd:].astype(BF16)
    full = lambda shape: pl.BlockSpec(shape, lambda b, i: (0,) * len(shape))
    row = lambda n: pl.BlockSpec((1, tm, n), lambda b, i: (b, i, 0))
    vec = pl.BlockSpec((1, 1, d), lambda b, i: (b, 0, 0))
    return pl.pallas_call(
        _gla_proj_kernel,
        grid=(bsz, seq // tm),
        in_specs=[row(d), full((1, d)), vec, vec, full((d, kd)), full((d, kd)), full((d, vd)),
                  full((d, vd)), full((d, GLA_RANK)), full((GLA_RANK, kd)), full((1, kd))],
        out_specs=[row(kd), row(kd), row(vd), row(vd), row(kd)],
        out_shape=[jax.ShapeDtypeStruct((bsz, seq, kd), BF16),
                   jax.ShapeDtypeStruct((bsz, seq, kd), BF16),
                   jax.ShapeDtypeStruct((bsz, seq, vd), BF16),
                   jax.ShapeDtypeStruct((bsz, seq, vd), BF16),
                   jax.ShapeDtypeStruct((bsz, seq, kd), F32)],
        compiler_params=_params("parallel", "parallel"),
        name="gla_proj",
    )(x, g.reshape(1, d), shift, scale, wq, wk, wv, wr, wlr, w_gate2, b_gate.reshape(1, kd))


def _gla_core_kernel(q_ref, k_ref, v_ref, r_ref, la_ref, ng_ref, o_ref, state_ref, *, n_chunks):
    @pl.when(pl.program_id(1) == 0)
    def _():
        state_ref[...] = jnp.zeros_like(state_ref)

    c_len = GLA_CHUNK
    row = lax.broadcasted_iota(jnp.int32, (c_len, c_len), 0)
    col = lax.broadcasted_iota(jnp.int32, (c_len, c_len), 1)
    causal = col <= row
    tril = causal.astype(F32)

    def chunk(c, carry):
        c0 = pl.multiple_of(c * c_len, c_len)
        for hh in range(GLA_HEADS):
            kc = slice(hh * GLA_HK, (hh + 1) * GLA_HK)
            vc = slice(hh * GLA_HV, (hh + 1) * GLA_HV)
            la = la_ref[0, pl.ds(c0, c_len), kc]
            b = jnp.dot(tril, la, precision=HIGHEST, preferred_element_type=F32)
            q = q_ref[0, pl.ds(c0, c_len), kc].astype(F32) * (GLA_HK ** -0.5)
            k = k_ref[0, pl.ds(c0, c_len), kc].astype(F32)
            v = v_ref[0, pl.ds(c0, c_len), vc]
            q_dec = (q * jnp.exp(b)).astype(BF16)
            k_intra = (k * jnp.exp(-b)).astype(BF16)
            b_t = b.T
            bl_t = b_t[:, c_len - 1:c_len]
            k_inter_t = (k.T * jnp.exp(bl_t - b_t)).astype(BF16)
            att = jnp.where(causal, _dot_nt(q_dec, k_intra), 0.0).astype(BF16)
            state = state_ref[hh]
            o = _dot(att, v) + _dot(q_dec, state.astype(BF16))
            state_ref[hh] = state * jnp.exp(bl_t) + _dot(k_inter_t, v)
            o = o * lax.rsqrt(jnp.mean(o * o, axis=-1, keepdims=True) + RMS_EPS)
            r = r_ref[0, pl.ds(c0, c_len), vc].astype(F32)
            o_ref[0, pl.ds(c0, c_len), vc] = ((o * ng_ref[:, vc]) * (r * jax.nn.sigmoid(r))).astype(BF16)
        return carry

    lax.fori_loop(0, n_chunks, chunk, 0)


def _gla_core(q, k, v, r, la, norm_g):
    bsz, seq, _ = q.shape
    ts = min(GLA_STEP, seq)
    kern = functools.partial(_gla_core_kernel, n_chunks=ts // GLA_CHUNK)
    kd = lambda: pl.BlockSpec((1, ts, GLA_KEY_DIM), lambda b, s: (b, s, 0))
    vd = lambda: pl.BlockSpec((1, ts, GLA_VAL_DIM), lambda b, s: (b, s, 0))
    return pl.pallas_call(
        kern,
        grid=(bsz, seq // ts),
        in_specs=[kd(), kd(), vd(), vd(), kd(), pl.BlockSpec((1, GLA_VAL_DIM), lambda b, s: (0, 0))],
        out_specs=vd(),
        out_shape=jax.ShapeDtypeStruct((bsz, seq, GLA_VAL_DIM), BF16),
        scratch_shapes=[pltpu.VMEM((GLA_HEADS, GLA_HK, GLA_HV), F32)],
        compiler_params=_params("parallel", "arbitrary"),
        name="gla_core",
    )(q, k, v, r, la, norm_g.reshape(1, GLA_VAL_DIM))


def _res_matmul_kernel(a_ref, w_ref, x_ref, gate_ref, o_ref):
    o_ref[0] = x_ref[0] + gate_ref[0] * _dot(a_ref[0], w_ref[...])


def _res_matmul(a, w, x, gate):
    bsz, seq, d = x.shape
    kdim = a.shape[-1]
    tm = min(ROW_TILE, seq)
    return pl.pallas_call(
        _res_matmul_kernel,
        grid=(bsz, seq // tm),
        in_specs=[pl.BlockSpec((1, tm, kdim), lambda b, i: (b, i, 0)),
                  pl.BlockSpec((kdim, d), lambda b, i: (0, 0)),
                  pl.BlockSpec((1, tm, d), lambda b, i: (b, i, 0)),
                  pl.BlockSpec((1, 1, d), lambda b, i: (b, 0, 0))],
        out_specs=pl.BlockSpec((1, tm, d), lambda b, i: (b, i, 0)),
        out_shape=jax.ShapeDtypeStruct((bsz, seq, d), F32),
        compiler_params=_params("parallel", "parallel"),
        name="res_matmul",
    )(a, w.astype(BF16), x, gate)


def _route_kernel(x_ref, g_ref, sh_ref, sc_ref, rwt_ref, rb_ref, xs_ref, w_ref, oid_ref, cnt_ref, *, n_tok):
    tm = x_ref.shape[0]
    n_slot = xs_ref.shape[0]
    h = _norm_mod(x_ref[...], g_ref[...], sh_ref[0], sc_ref[0])
    logits = _dot_nt(rwt_ref[...], h, precision=HIGHEST) + rb_ref[...]
    e_iota = lax.broadcasted_iota(jnp.int32, logits.shape, 0)
    vals, hots = [], []
    for _ in range(TOP_K):
        m = jnp.max(logits, axis=0, keepdims=True)
        idx = jnp.min(jnp.where(logits == m, e_iota, N_EXPERTS), axis=0, keepdims=True)
        hot = e_iota == idx
        vals.append(m)
        hots.append(hot)
        logits = jnp.where(hot, -jnp.inf, logits)
    exps = [jnp.exp(v - vals[0]) for v in vals]
    denom = exps[0] + exps[1] + exps[2] + exps[3]
    for kk in range(TOP_K):
        w_ref[kk:kk + 1, :] = exps[kk] / denom

    sel = (hots[0] | hots[1] | hots[2] | hots[3]).astype(F32)
    s_iota = lax.broadcasted_iota(jnp.int32, (tm, tm), 0)
    t_iota = lax.broadcasted_iota(jnp.int32, (tm, tm), 1)
    before = (s_iota < t_iota).astype(BF16)
    prefix = _dot(sel.astype(BF16), before)
    cnt = jnp.sum(sel, axis=1, keepdims=True)
    chunks = jnp.floor((cnt + (ROW_CHUNK - 1.0)) * (1.0 / ROW_CHUNK))
    lower = (lax.broadcasted_iota(jnp.int32, (N_EXPERTS, N_EXPERTS), 0) >
             lax.broadcasted_iota(jnp.int32, (N_EXPERTS, N_EXPERTS), 1)).astype(BF16)
    seg0 = _dot(lower, jnp.broadcast_to(chunks, (N_EXPERTS, LANES)).astype(BF16))[:, 0:1] * ROW_CHUNK
    slots = [jnp.sum(jnp.where(hot, prefix + seg0, 0.0), axis=0, keepdims=True).astype(jnp.int32)
             for hot in hots]
    r_iota = lax.broadcasted_iota(jnp.int32, (n_slot, tm), 0)
    q = jnp.zeros((n_slot, tm), F32)
    for kk in range(TOP_K):
        q = q + jnp.where(r_iota == slots[kk], kk + 1.0, 0.0)
    p = (q > 0.0).astype(BF16)
    xs_ref[...] = _dot(p, h.astype(BF16))

    t_idx = lax.broadcasted_iota(jnp.int32, (tm, LANES), 0)
    col = lax.broadcasted_iota(jnp.int32, (tm, LANES), 1)
    digits = jnp.where(col == 0, t_idx // 16, jnp.where(col == 1, t_idx % 16, 0)).astype(BF16)
    td = _dot(p, digits)
    kp1 = _dot(q.astype(BF16), jnp.ones((tm, LANES), BF16))[:, 0:1]
    tok = (pl.program_id(0) * tm).astype(F32) + 16.0 * td[:, 0:1] + td[:, 1:2]
    oid_ref[...] = jnp.where(kp1 > 0.0, (kp1 - 1.0) * n_tok + tok, -1.0).astype(jnp.int32)
    cnt_ref[0] = jnp.broadcast_to(cnt, (N_EXPERTS, LANES)).astype(jnp.int32)


def _route(x2d, g, shift, scale, router_w, router_b, seq):
    n_tok, d = x2d.shape
    tm = min(ROUTE_TILE, seq)
    per_b = seq // tm
    n_win = n_tok // tm
    n_slot = tm * TOP_K + N_EXPERTS * ROW_CHUNK
    vec = pl.BlockSpec((1, 1, d), lambda i: (i // per_b, 0, 0))
    return pl.pallas_call(
        functools.partial(_route_kernel, n_tok=n_tok),
        grid=(n_win,),
        in_specs=[pl.BlockSpec((tm, d), lambda i: (i, 0)),
                  pl.BlockSpec((1, d), lambda i: (0, 0)), vec, vec,
                  pl.BlockSpec((N_EXPERTS, d), lambda i: (0, 0)),
                  pl.BlockSpec((N_EXPERTS, 1), lambda i: (0, 0))],
        out_specs=[pl.BlockSpec((n_slot, d), lambda i: (i, 0)),
                   pl.BlockSpec((TOP_K, tm), lambda i: (0, i)),
                   pl.BlockSpec((n_slot, 1), lambda i: (i, 0)),
                   pl.BlockSpec((1, N_EXPERTS, LANES), lambda i: (i, 0, 0))],
        out_shape=[jax.ShapeDtypeStruct((n_win * n_slot, d), F32),
                   jax.ShapeDtypeStruct((TOP_K, n_tok), F32),
                   jax.ShapeDtypeStruct((n_win * n_slot, 1), jnp.int32),
                   jax.ShapeDtypeStruct((n_win, N_EXPERTS, LANES), jnp.int32)],
        compiler_params=_params("parallel"),
        name="moe_route",
    )(x2d, g.reshape(1, d), shift, scale, router_w.T, router_b.reshape(N_EXPERTS, 1))


def _expert_kernel(te_ref, nu_ref, chunk_hbm, out_hbm, xs_hbm, wgu_ref, wd_ref, bg_ref, bu_ref, bd_ref, y_hbm,
                   wg_s, wu_s, wd_s, tr_s, xbuf, ybuf, tok_s, out_s, tok_sem, out_sem, g_sem, s_sem):
    i = pl.program_id(0)
    n_tiles = pl.num_programs(0)
    tm = xbuf.shape[1]
    used = i < nu_ref[0]
    fresh = (i == 0) | (te_ref[i] != te_ref[jnp.maximum(i - 1, 0)])
    slot = i % 2
    other = 1 - slot

    def tok_copy(t, sl):
        return pltpu.make_async_copy(chunk_hbm.at[t], tok_s.at[sl], tok_sem.at[sl])

    def out_copy(t, sl):
        return pltpu.make_async_copy(out_hbm.at[t], out_s.at[sl], out_sem.at[sl])

    def gather_start(sl):
        for c in range(tm // ROW_CHUNK):
            src = pl.multiple_of(tok_s[sl, c] * ROW_CHUNK, ROW_CHUNK)
            pltpu.make_async_copy(xs_hbm.at[pl.ds(src, ROW_CHUNK), :],
                                  xbuf.at[sl, pl.ds(c * ROW_CHUNK, ROW_CHUNK), :], g_sem.at[sl]).start()

    def gather_wait(sl):
        pltpu.make_async_copy(xs_hbm.at[pl.ds(0, tm), :], xbuf.at[sl], g_sem.at[sl]).wait()

    def scatter_start(sl):
        for r in range(tm):
            pltpu.make_async_copy(ybuf.at[sl, r], y_hbm.at[out_s[sl, r]], s_sem.at[sl]).start()

    def scatter_wait(sl):
        pltpu.make_async_copy(ybuf.at[sl], y_hbm.at[pl.ds(0, tm)], s_sem.at[sl]).wait()

    @pl.when(i == 0)
    def _():
        tok_copy(0, 0).start()
        out_copy(0, 0).start()
        ybuf[1] = jnp.zeros(ybuf.shape[1:], F32)
        spare = y_hbm.shape[0] - 2 * tm
        zero_even = pltpu.make_async_copy(ybuf.at[1], y_hbm.at[pl.ds(spare, tm)], s_sem.at[0])
        zero_even.start()
        pltpu.make_async_copy(ybuf.at[1], y_hbm.at[pl.ds(spare + tm, tm)], s_sem.at[1]).start()
        tok_copy(0, 0).wait()
        gather_start(0)
        tok_copy(jnp.minimum(1, n_tiles - 1), 1).start()
        zero_even.wait()

    @pl.when(used & fresh)
    def _():
        n_slab, chunk, lanes = tr_s.shape
        half = chunk // 2
        for c in range(wgu_ref.shape[2] // chunk):
            t = wgu_ref[0, :, c * chunk:(c + 1) * chunk].T
            for j in range(n_slab):
                tr_s[j] = t[:, j * lanes:(j + 1) * lanes]
            for j in range(n_slab):
                wg_s[c * half:(c + 1) * half, j * lanes:(j + 1) * lanes] = (
                    tr_s[j, pl.ds(0, half, stride=2), :].astype(BF16))
                wu_s[c * half:(c + 1) * half, j * lanes:(j + 1) * lanes] = (
                    tr_s[j, pl.ds(1, half, stride=2), :].astype(BF16))
        wd_s[...] = wd_ref[0].astype(BF16)

    @pl.when(used)
    def _():
        nxt = jnp.minimum(i + 1, n_tiles - 1)
        tok_copy(nxt, other).wait()
        gather_start(other)
        tok_copy(jnp.minimum(i + 2, n_tiles - 1), slot).start()
        gather_wait(slot)
        xb = xbuf[slot].astype(BF16)
        gate = jnp.minimum(_dot_nt(xb, wg_s[...]) + bg_ref[0], SWIGLU_LIMIT)
        up = jnp.clip(_dot_nt(xb, wu_s[...]) + bu_ref[0], -SWIGLU_LIMIT, SWIGLU_LIMIT)
        act = (up + 1.0) * gate * jax.nn.sigmoid(SWIGLU_ALPHA * gate)
        _store_rows(ybuf.at[slot], _dot(act.astype(BF16), wd_s[...]) + bd_ref[0])
        out_copy(i, slot).wait()
        scatter_start(slot)
        out_copy(nxt, other).start()
        scatter_wait(other)

    @pl.when(i == nu_ref[0] - 1)
    def _():
        gather_wait(other)
        scatter_wait(slot)
        tok_copy(0, slot).wait()
        out_copy(0, other).wait()


def _experts(xs, n_tok, chunk_tiles, out_tiles, tile_expert, n_used, layer, w_gu, w_down, bg, bu, bd):
    d = xs.shape[1]
    n_tiles, tm = out_tiles.shape
    ff = w_down.shape[2]
    wsel = lambda i, te, nu: (te[i], 0, 0)
    lsel = lambda i, te, nu: (layer, te[i], 0, 0)
    hbm = pl.BlockSpec(memory_space=pl.ANY)
    return pl.pallas_call(
        _expert_kernel,
        grid_spec=pltpu.PrefetchScalarGridSpec(
            num_scalar_prefetch=2,
            grid=(n_tiles,),
            in_specs=[hbm, hbm, hbm,
                      pl.BlockSpec((None, 1, d, 2 * ff), lsel),
                      pl.BlockSpec((None, 1, ff, d), lsel),
                      pl.BlockSpec((1, 1, ff), wsel), pl.BlockSpec((1, 1, ff), wsel),
                      pl.BlockSpec((1, 1, d), wsel)],
            out_specs=hbm,
            scratch_shapes=[pltpu.VMEM((ff, d), BF16), pltpu.VMEM((ff, d), BF16),
                            pltpu.VMEM((ff, d), BF16), pltpu.VMEM((d // 128, 256, 128), F32),
                            pltpu.VMEM((2, tm, d), F32),
                            pltpu.VMEM((2, tm, d // LANES, LANES), F32),
                            pltpu.SMEM((2, tm // ROW_CHUNK), jnp.int32), pltpu.SMEM((2, tm), jnp.int32),
                            pltpu.SemaphoreType.DMA((2,)), pltpu.SemaphoreType.DMA((2,)),
                            pltpu.SemaphoreType.DMA((2,)), pltpu.SemaphoreType.DMA((2,))]),
        out_shape=jax.ShapeDtypeStruct((TOP_K * n_tok + 2 * tm, d // LANES, LANES), F32),
        compiler_params=_params("arbitrary"),
        name="moe_experts",
    )(tile_expert, n_used, chunk_tiles, out_tiles, xs, w_gu, w_down, bg, bu, bd)


def _combine_kernel(y0_ref, y1_ref, y2_ref, y3_ref, x_ref, w_ref, gate_ref, o_ref):
    w = w_ref[...]
    y = w[:, 0:1] * _load_rows(y0_ref)
    for kk, y_ref in enumerate((y1_ref, y2_ref, y3_ref), start=1):
        y = y + w[:, kk:kk + 1] * _load_rows(y_ref)
    o_ref[...] = x_ref[...] + gate_ref[0] * y


def _combine(y, x2d, w_tok, gate, seq):
    n_tok, d = x2d.shape
    tm = COMBINE_TILE
    per_b = seq // tm
    n_blk = n_tok // tm
    plane = lambda kk: pl.BlockSpec((tm, d // LANES, LANES), lambda i, kk=kk: (kk * n_blk + i, 0, 0))
    return pl.pallas_call(
        _combine_kernel,
        grid=(n_blk,),
        in_specs=[plane(0), plane(1), plane(2), plane(3),
                  pl.BlockSpec((tm, d), lambda i: (i, 0)),
                  pl.BlockSpec((tm, TOP_K), lambda i: (i, 0)),
                  pl.BlockSpec((1, 1, d), lambda i: (i // per_b, 0, 0))],
        out_specs=pl.BlockSpec((tm, d), lambda i: (i, 0)),
        out_shape=jax.ShapeDtypeStruct((n_tok, d), F32),
        compiler_params=_params("parallel"),
        name="moe_combine",
    )(y, y, y, y, x2d, w_tok, gate)


def _moe_layer(x, norm_g, shift, scale, gate, router_w, router_b, layer, w_gu, b_gu, w_down, b_down):
    bsz, seq, d = x.shape
    n_tok = bsz * seq
    x2d = x.reshape(n_tok, d)
    xs, w_t, oid, cnt = _route(x2d, norm_g, shift, scale, router_w, router_b, seq)

    cnt = cnt[:, :, 0]
    n_win = cnt.shape[0]
    win_chunks = xs.shape[0] // n_win // ROW_CHUNK
    tile_chunks = EXPERT_TILE // ROW_CHUNK
    c8 = (cnt + ROW_CHUNK - 1) // ROW_CHUNK
    seg_off = jnp.cumsum(c8, axis=1) - c8
    cum_w = jnp.cumsum(c8, axis=0)
    total = cum_w[-1]
    padded = (total + tile_chunks - 1) // tile_chunks * tile_chunks
    ends = jnp.cumsum(padded)
    starts = ends - padded
    n_chunk = n_tok * TOP_K // ROW_CHUNK + n_win * N_EXPERTS + N_EXPERTS * tile_chunks
    n_tiles = n_chunk // tile_chunks
    gq = jnp.arange(n_chunk, dtype=jnp.int32)
    e_of = jnp.minimum(jnp.sum((ends[None, :] <= gq[:, None]).astype(jnp.int32), axis=1), N_EXPERTS - 1)
    e_hot = (e_of[:, None] == jnp.arange(N_EXPERTS, dtype=jnp.int32)[None, :]).astype(jnp.int32)
    rank = gq - jnp.sum(e_hot * starts[None, :], axis=1)
    valid = rank < jnp.sum(e_hot * total[None, :], axis=1)
    pick = lambda tab: jnp.sum(e_hot[:, :, None] * tab.T[None, :, :], axis=1)
    cum_g = pick(cum_w)
    w_of = jnp.minimum(jnp.sum((cum_g <= rank[:, None]).astype(jnp.int32), axis=1), n_win - 1)
    w_hot = (w_of[:, None] == jnp.arange(n_win, dtype=jnp.int32)[None, :]).astype(jnp.int32)
    seg_first = jnp.sum(w_hot * (cum_g - pick(c8)), axis=1)
    src = w_of * win_chunks + jnp.sum(w_hot * pick(seg_off), axis=1) + rank - seg_first
    src = jnp.where(valid, src, win_chunks - 1)
    tile_expert = e_of[::tile_chunks]
    n_used = (ends[-1:] // tile_chunks).astype(jnp.int32)

    n_asg = TOP_K * n_tok
    out_rows = oid.reshape(-1, ROW_CHUNK)[src].reshape(-1)
    row = jnp.arange(n_chunk * ROW_CHUNK, dtype=jnp.int32)
    te = EXPERT_TILE
    out_rows = jnp.where(out_rows >= 0, out_rows, n_asg + (row // te % 2) * te + row % te)

    y = _experts(xs, n_tok, src.reshape(n_tiles, tile_chunks), out_rows.reshape(n_tiles, te), tile_expert,
                 n_used, layer, w_gu, w_down, b_gu[:, None, 0::2], b_gu[:, None, 1::2], b_down[:, None, :])
    out = _combine(y, x2d, w_t.T, gate, seq)
    return out.reshape(bsz, seq, d)


def _kv_proj_kernel(x_ref, g_ref, sh_ref, sc_ref, w_ref, cmp_ref, kv_ref):
    h = _norm_mod(x_ref[0], g_ref[...], sh_ref[0], sc_ref[0]).astype(BF16)
    kv = _dot(h, w_ref[...])
    for t in range(6):
        for g in range(NSA_GROUPS):
            c0 = (t * NSA_GROUPS + g) * NSA_HD
            piece = kv[:, c0:c0 + NSA_HD]
            if t < 2:
                cmp_ref[t, 0, g] = piece
            else:
                kv_ref[t - 2, 0, g] = piece.astype(BF16)


def _kv_proj(x, g, shift, scale, kv_w):
    bsz, seq, d = x.shape
    tm = min(ROW_TILE, seq)
    n_out = kv_w.shape[1]
    vec = pl.BlockSpec((1, 1, d), lambda b, i: (b, 0, 0))
    return pl.pallas_call(
        _kv_proj_kernel,
        grid=(bsz, seq // tm),
        in_specs=[pl.BlockSpec((1, tm, d), lambda b, i: (b, i, 0)),
                  pl.BlockSpec((1, d), lambda b, i: (0, 0)), vec, vec,
                  pl.BlockSpec((d, n_out), lambda b, i: (0, 0))],
        out_specs=[pl.BlockSpec((2, 1, NSA_GROUPS, tm, NSA_HD), lambda b, i: (0, b, 0, i, 0)),
                   pl.BlockSpec((4, 1, NSA_GROUPS, tm, NSA_HD), lambda b, i: (0, b, 0, i, 0))],
        out_shape=[jax.ShapeDtypeStruct((2, bsz, NSA_GROUPS, seq, NSA_HD), F32),
                   jax.ShapeDtypeStruct((4, bsz, NSA_GROUPS, seq, NSA_HD), BF16)],
        compiler_params=_params("parallel", "parallel"),
        name="nsa_kv_proj",
    )(x, g.reshape(1, d), shift, scale, kv_w.astype(BF16))


def _compress_kernel(x_ref, pos_ref, w1_ref, b1_ref, w2_ref, b2_ref, o_ref, *, n_rows):
    for g in range(NSA_GROUPS):
        first = jnp.zeros((n_rows, CMP_HIDDEN), F32)
        second = jnp.zeros((n_rows, CMP_HIDDEN), F32)
        for l in range(CMP_STRIDE):
            rows = x_ref[0, 0, g, pl.ds(l, n_rows, stride=CMP_STRIDE), :]
            first = first + _dot((rows + pos_ref[0, l:l + 1, :]).astype(BF16), w1_ref[0, l])
            second = second + _dot((rows + pos_ref[0, CMP_STRIDE + l:CMP_STRIDE + l + 1, :]).astype(BF16),
                                   w1_ref[0, CMP_STRIDE + l])
        pre = first + pltpu.roll(second, n_rows - 1, 0) + b1_ref[0]
        hid = 0.5 * pre * (1.0 + jnp.tanh(0.7978845608028654 * (pre + 0.044715 * pre * pre * pre)))
        o_ref[0, 0, g] = (_dot(hid.astype(BF16), w2_ref[0]) + b2_ref[0]).astype(BF16)


def _compress(kv_cmp, cmp_pos, cmp_w1, cmp_b1, cmp_w2, cmp_b2):
    _, bsz, _, seq, _ = kv_cmp.shape
    n_rows = seq // CMP_STRIDE
    kern = functools.partial(_compress_kernel, n_rows=n_rows)
    w1 = cmp_w1.reshape(2, CMP_LEN, NSA_HD, CMP_HIDDEN).astype(BF16)
    return pl.pallas_call(
        kern,
        grid=(2, bsz),
        in_specs=[pl.BlockSpec((1, 1, NSA_GROUPS, seq, NSA_HD), lambda t, b: (t, b, 0, 0, 0)),
                  pl.BlockSpec((1, CMP_LEN, NSA_HD), lambda t, b: (t, 0, 0)),
                  pl.BlockSpec((1, CMP_LEN, NSA_HD, CMP_HIDDEN), lambda t, b: (t, 0, 0, 0)),
                  pl.BlockSpec((1, 1, CMP_HIDDEN), lambda t, b: (t, 0, 0)),
                  pl.BlockSpec((1, CMP_HIDDEN, NSA_HD), lambda t, b: (t, 0, 0)),
                  pl.BlockSpec((1, 1, NSA_HD), lambda t, b: (t, 0, 0))],
        out_specs=pl.BlockSpec((1, 1, NSA_GROUPS, n_rows, NSA_HD), lambda t, b: (t, b, 0, 0, 0)),
        out_shape=jax.ShapeDtypeStruct((2, bsz, NSA_GROUPS, n_rows, NSA_HD), BF16),
        compiler_params=_params("parallel", "parallel"),
        name="nsa_compress",
    )(kv_cmp, cmp_pos, w1, cmp_b1[:, None, :], cmp_w2.astype(BF16), cmp_b2[:, None, :])


def _nsa_proj_kernel(x_ref, g_ref, sh_ref, sc_ref, wq_ref, wg_ref, bg_ref, q_ref, gate_ref):
    h = _norm_mod(x_ref[0], g_ref[...], sh_ref[0], sc_ref[0]).astype(BF16)
    q_ref[0] = (_dot(h, wq_ref[...]) * (NSA_HD ** -0.5)).astype(BF16)
    gate_ref[0] = jax.nn.sigmoid(_dot(h, wg_ref[...]) + bg_ref[...])


def _nsa_proj(x, g, shift, scale, w_in, b_gate):
    bsz, seq, d = x.shape
    tm = min(ROW_TILE, seq)
    nq = NSA_HEADS * NSA_HD
    ng = 3 * NSA_HEADS
    vec = pl.BlockSpec((1, 1, d), lambda b, i: (b, 0, 0))
    return pl.pallas_call(
        _nsa_proj_kernel,
        grid=(bsz, seq // tm),
        in_specs=[pl.BlockSpec((1, tm, d), lambda b, i: (b, i, 0)),
                  pl.BlockSpec((1, d), lambda b, i: (0, 0)), vec, vec,
                  pl.BlockSpec((d, nq), lambda b, i: (0, 0)),
                  pl.BlockSpec((d, ng), lambda b, i: (0, 0)),
                  pl.BlockSpec((1, ng), lambda b, i: (0, 0))],
        out_specs=[pl.BlockSpec((1, tm, nq), lambda b, i: (b, i, 0)),
                   pl.BlockSpec((1, tm, ng), lambda b, i: (b, i, 0))],
        out_shape=[jax.ShapeDtypeStruct((bsz, seq, nq), BF16),
                   jax.ShapeDtypeStruct((bsz, seq, ng), F32)],
        compiler_params=_params("parallel", "parallel"),
        name="nsa_proj",
    )(x, g.reshape(1, d), shift, scale, w_in[:, :nq].astype(BF16), w_in[:, nq:].astype(BF16),
      b_gate.reshape(1, ng))


def _nsa_attn_kernel(q_ref, gate_ref, qc_ref, slope_ref, ovl_ref, kconst_ref, cw_ref, kc_ref, vc_ref,
                     ks_ref, vs_ref, kw_ref, vw_ref, o_ref, ksa, vsa, kwa, vwa, score_s,
                     sc_s, pc_s, psum_s, qa_s, m_s, acc_s, sa_s, sb_s, ps_s, sw_s, pw_s):
    qi = pl.program_id(2)
    qs = NSA_STEP
    hd = NSA_HD
    rows = NSA_HPG * qs
    seq = ks_ref.shape[2]
    q0 = pl.multiple_of(qi * qs, qs)

    @pl.when(qi == 0)
    def _():
        ksa[...] = kconst_ref[...]
        ksa[:, 0:hd] = ks_ref[0, 0]
        one_col = (lax.broadcasted_iota(jnp.int32, (seq, 128), 1) == hd).astype(BF16)
        vsa[...] = one_col
        vsa[:, 0:hd] = vs_ref[0, 0]
        flag_col = (lax.broadcasted_iota(jnp.int32, (WIN_PAD, 128), 1) == hd).astype(BF16)
        kwa[0:WIN_PAD, :] = flag_col
        kwa[WIN_PAD:, :] = jnp.zeros((seq, 128), BF16)
        kwa[WIN_PAD:, 0:hd] = kw_ref[0, 0]
        vwa[0:WIN_PAD, :] = jnp.zeros((WIN_PAD, 128), BF16)
        vwa[WIN_PAD:, :] = one_col
        vwa[WIN_PAD:, 0:hd] = vw_ref[0, 0]

    qt = q_ref[0].astype(F32)
    q32 = jnp.concatenate([qt[:, hh * hd:(hh + 1) * hd] for hh in range(NSA_HPG)], axis=0)
    q = q32.astype(BF16)
    slope = slope_ref[0]
    t_row = q0 + lax.broadcasted_iota(jnp.int32, (rows, 1), 0) % qs

    n_cmp = kc_ref.shape[2]
    cmp_end = lax.broadcasted_iota(jnp.int32, (1, n_cmp), 1) * CMP_STRIDE + (CMP_LEN - 1)
    cmp_bias = cmp_end.astype(F32)
    sc_s[...] = _dot_nt(q, kc_ref[0, 0])
    rc = NSA_ROW_CHUNK
    for c in range(rows // rc):
        r = slice(c * rc, (c + 1) * rc)
        ok_c = cmp_end <= t_row[r]
        s_c = jnp.where(ok_c, sc_s[r, :] + slope[r] * cmp_bias, NEG)
        m_c = jnp.max(s_c, axis=-1, keepdims=True)
        e_c = jnp.where(ok_c, jnp.exp(s_c - m_c), 0.0)
        l_c = jnp.sum(e_c, axis=-1, keepdims=True)
        p_c = e_c * (1.0 / jnp.where(l_c > 0.0, l_c, 1.0))
        pc_s[r, :] = p_c.astype(BF16)
        qr = slice((c * rc) % qs, (c * rc) % qs + rc)
        if c * rc < qs:
            psum_s[qr, :] = p_c
        else:
            psum_s[qr, :] = psum_s[qr, :] + p_c
    o_c = _dot(pc_s[...], vc_ref[0, 0])

    imp_t = _dot_nt(ovl_ref[...], psum_s[...], precision=HIGHEST)
    n_slc = imp_t.shape[0]
    blk = lax.broadcasted_iota(jnp.int32, (n_slc, qs), 0)
    cur = (q0 + lax.broadcasted_iota(jnp.int32, (1, qs), 1)) // SLC_LEN
    forced = (blk == 0) | (blk == cur) | (blk == cur - 1)
    score = jnp.where(blk <= cur, imp_t + jnp.where(forced, FORCE, 0.0), -FORCE)
    score_s[...] = score

    def rank_step(ip, rank):
        for i in (2 * ip, 2 * ip + 1):
            other = score_s[pl.ds(i, 1), :]
            ahead = (other > score) | ((other == score) & (blk > i))
            rank = rank + ahead.astype(F32)
        return rank

    last_blk = (q0 + qs - 1) // SLC_LEN
    n_rank = jnp.where(last_blk >= SLC_TOPK, last_blk // 2 + 1, 0)
    rank = lax.fori_loop(0, n_rank, rank_step, jnp.zeros((n_slc, qs), F32))
    sel_t = (rank < float(min(SLC_TOPK, n_slc))).astype(BF16)
    eye = (lax.broadcasted_iota(jnp.int32, (qs, qs), 0) ==
           lax.broadcasted_iota(jnp.int32, (qs, qs), 1)).astype(BF16)
    sel = _dot_nt(eye, sel_t)
    drop = (sel - 1.0) * MASK_BIG
    if n_slc < hd:
        drop = jnp.concatenate([drop, jnp.zeros((qs, hd - n_slc), F32)], axis=1)
    q_aug = jnp.concatenate([q32, jnp.concatenate([drop] * NSA_HPG, axis=0), qc_ref[0]],
                            axis=1).astype(BF16)

    tk = SLC_KEY_TILE

    n_full = q0 // tk
    qa_s[...] = q_aug
    m_s[...] = jnp.full((rows, 1), -MASK_BIG, F32)
    acc_s[...] = jnp.zeros((rows, 128), F32)

    def scores(kt, s_out):
        k0 = pl.multiple_of(kt * tk, tk)
        s_out[...] = _dot_nt(qa_s[...], ksa[pl.ds(k0, tk), :])

    def absorb(s_in, kt, causal):
        k0 = pl.multiple_of(kt * tk, tk)
        kpos = k0 + lax.broadcasted_iota(jnp.int32, (1, tk), 1)
        for c in range(rows // rc):
            r = slice(c * rc, (c + 1) * rc)
            s = s_in[r, :]
            if causal:
                s = jnp.where(kpos <= t_row[r], s, -MASK_BIG)
            m_old = m_s[r, :]
            m_new = jnp.maximum(m_old, jnp.max(s, axis=-1, keepdims=True))
            m_s[r, :] = m_new
            ps_s[r, :] = jnp.exp(s - m_new).astype(BF16)
            acc_s[r, :] = jnp.exp(m_old - m_new) * acc_s[r, :]
        acc_s[...] = acc_s[...] + _dot(ps_s[...], vsa[pl.ds(k0, tk), :])

    scores(0, sa_s)

    def slc_pair(j, carry):
        scores(2 * j + 1, sb_s)
        absorb(sa_s, 2 * j, False)

        @pl.when(2 * j + 1 < n_full)
        def _():
            scores(2 * j + 2, sa_s)
            absorb(sb_s, 2 * j + 1, False)
        return carry

    lax.fori_loop(0, (n_full + 1) // 2, slc_pair, 0)

    @pl.when(n_full % 2 == 0)
    def _():
        absorb(sa_s, n_full, True)

    @pl.when(n_full % 2 == 1)
    def _():
        absorb(sb_s, n_full, True)

    o_s = acc_s[:, 0:hd] * (1.0 / acc_s[:, hd:hd + 1])

    nw = WIN_PAD + qs
    q_w = jnp.concatenate([q32, jnp.full((rows, hd), -MASK_BIG, F32)], axis=1).astype(BF16)
    sw_s[...] = _dot_nt(q_w, kwa[pl.ds(q0, nw), :])
    for c in range(rows // rc):
        r = slice(c * rc, (c + 1) * rc)
        s_w = sw_s[r, :] + cw_ref[0, r, :]
        m_w = jnp.max(s_w, axis=-1, keepdims=True)
        pw_s[r, :] = jnp.exp(s_w - m_w).astype(BF16)
    acc_w = _dot(pw_s[...], vwa[pl.ds(q0, nw), :])
    o_w = acc_w[:, 0:hd] * (1.0 / acc_w[:, hd:hd + 1])

    gates = gate_ref[0, 0]
    outs = []
    for hh in range(NSA_HPG):
        r0 = hh * qs
        outs.append(gates[:, hh:hh + 1] * o_c[r0:r0 + qs]
                    + gates[:, NSA_HPG + hh:NSA_HPG + hh + 1] * o_s[r0:r0 + qs]
                    + gates[:, 2 * NSA_HPG + hh:2 * NSA_HPG + hh + 1] * o_w[r0:r0 + qs])
    o_ref[0] = jnp.concatenate(outs, axis=1).astype(BF16)


def _bf16_pieces(x):
    x = np.asarray(x, np.float32)
    out = []
    for _ in range(3):
        p = x.astype(BF16).astype(np.float32)
        out.append(p)
        x = x - p
    return out


def _nsa_constants(seq):
    qb, hd = NSA_STEP, NSA_HD
    n_cmp_rows = seq // CMP_STRIDE
    n_slc = seq // SLC_LEN
    cmp_start = np.arange(n_cmp_rows) * CMP_STRIDE
    cmp_end = cmp_start + CMP_LEN - 1
    slc_start = np.arange(n_slc) * SLC_LEN
    overlap = ((cmp_start[:, None] <= slc_start[None, :] + SLC_LEN - 1)
               & (cmp_end[:, None] >= slc_start[None, :])).astype(np.float32)
    start = 2.0 ** (-8.0 / NSA_HEADS)
    slopes = np.asarray(start ** np.arange(1, NSA_HEADS + 1), np.float32).reshape(NSA_GROUPS, NSA_HPG)
    slope_rows = np.repeat(slopes, qb, axis=1)

    pos = np.arange(seq)
    kconst = np.zeros((seq, 256), np.float32)
    kconst[pos, hd + pos // SLC_LEN] = 1.0
    kconst[:, 2 * hd + 0:2 * hd + 3] = (pos // 64 * 64)[:, None]
    kconst[:, 2 * hd + 3:2 * hd + 6] = (pos % 64)[:, None]
    qconst = np.zeros((NSA_GROUPS, NSA_HPG * qb, 128), np.float32)
    for j, piece in enumerate(_bf16_pieces(slope_rows)):
        qconst[:, :, j] = piece
        qconst[:, :, 3 + j] = piece
    dist = (np.arange(NSA_HPG * qb) % qb)[:, None] + WIN_PAD - np.arange(WIN_PAD + qb)[None, :]
    cw = np.where((dist >= 0) & (dist < WIN), -slope_rows[:, :, None] * dist[None].astype(np.float32),
                  -MASK_BIG).astype(np.float32)
    return (jnp.asarray(overlap.T), jnp.asarray(kconst, BF16), jnp.asarray(qconst),
            jnp.asarray(slope_rows[:, :, None]), jnp.asarray(cw))


def _nsa_attn(q, gates, kc, vc, kv):
    bsz, seq, _ = q.shape
    qb = NSA_STEP
    rows = NSA_HPG * qb
    gw = NSA_HPG * NSA_HD
    n_cmp_rows = seq // CMP_STRIDE
    n_slc = seq // SLC_LEN
    nw = WIN_PAD + qb
    ovl_t, kconst, qconst, slope_rows, cw = _nsa_constants(seq)
    cmp_spec = lambda: pl.BlockSpec((1, 1, n_cmp_rows, NSA_HD), lambda b, g, i: (b, g, 0, 0))
    kv_spec = lambda t: pl.BlockSpec((None, 1, 1, seq, NSA_HD), lambda b, g, i, t=t: (t, b, g, 0, 0))
    return pl.pallas_call(
        _nsa_attn_kernel,
        grid=(bsz, NSA_GROUPS, seq // qb),
        in_specs=[pl.BlockSpec((1, qb, gw), lambda b, g, i: (b, i, g)),
                  pl.BlockSpec((1, 1, qb, 3 * NSA_HPG), lambda b, g, i: (b, g, i, 0)),
                  pl.BlockSpec((1, rows, 128), lambda b, g, i: (g, 0, 0)),
                  pl.BlockSpec((1, rows, 1), lambda b, g, i: (g, 0, 0)),
                  pl.BlockSpec((n_slc, n_cmp_rows), lambda b, g, i: (0, 0)),
                  pl.BlockSpec((seq, 256), lambda b, g, i: (0, 0)),
                  pl.BlockSpec((1, rows, nw), lambda b, g, i: (g, 0, 0)),
                  cmp_spec(), cmp_spec(), kv_spec(0), kv_spec(1), kv_spec(2), kv_spec(3)],
        out_specs=pl.BlockSpec((1, qb, gw), lambda b, g, i: (b, i, g)),
        out_shape=jax.ShapeDtypeStruct((bsz, seq, NSA_HEADS * NSA_HD), BF16),
        scratch_shapes=[pltpu.VMEM((seq, 256), BF16), pltpu.VMEM((seq, 128), BF16),
                        pltpu.VMEM((WIN_PAD + seq, 128), BF16), pltpu.VMEM((WIN_PAD + seq, 128), BF16),
                        pltpu.VMEM((n_slc, qb), F32),
                        pltpu.VMEM((rows, n_cmp_rows), F32), pltpu.VMEM((rows, n_cmp_rows), BF16),
                        pltpu.VMEM((qb, n_cmp_rows), F32), pltpu.VMEM((rows, 256), BF16),
                        pltpu.VMEM((rows, 1), F32), pltpu.VMEM((rows, 128), F32),
                        pltpu.VMEM((rows, SLC_KEY_TILE), F32), pltpu.VMEM((rows, SLC_KEY_TILE), F32),
                        pltpu.VMEM((rows, SLC_KEY_TILE), BF16),
                        pltpu.VMEM((rows, nw), F32), pltpu.VMEM((rows, nw), BF16)],
        compiler_params=_params("parallel", "parallel", "arbitrary"),
        name="nsa_attn",
    )(q, gates, qconst, slope_rows, ovl_t, kconst, cw, kc, vc, kv, kv, kv, kv)


def _nsa_mixer(x, norm_g, shift, scale, shared, w_in, b_gate):
    kcv, kv = shared
    bsz, seq, _ = x.shape
    q, gates = _nsa_proj(x, norm_g, shift, scale, w_in, b_gate)
    gates = gates.reshape(bsz, seq, 3, NSA_GROUPS, NSA_HPG).transpose(0, 3, 1, 2, 4)
    gates = gates.reshape(bsz, NSA_GROUPS, seq, 3 * NSA_HPG)
    return _nsa_attn(q, gates, kcv[0], kcv[1], kv)


def _final_norm_kernel(x_ref, g_ref, o_ref):
    x = x_ref[...]
    o_ref[...] = x * lax.rsqrt(jnp.mean(x * x, axis=-1, keepdims=True) + RMS_EPS) * g_ref[...]


def _final_norm(x, g):
    bsz, seq, d = x.shape
    x2d = x.reshape(bsz * seq, d)
    tm = min(ROW_TILE, seq)
    out = pl.pallas_call(
        _final_norm_kernel,
        grid=(x2d.shape[0] // tm,),
        in_specs=[pl.BlockSpec((tm, d), lambda i: (i, 0)), pl.BlockSpec((1, d), lambda i: (0, 0))],
        out_specs=pl.BlockSpec((tm, d), lambda i: (i, 0)),
        out_shape=jax.ShapeDtypeStruct(x2d.shape, F32),
        compiler_params=_params("parallel"),
        name="final_norm",
    )(x2d, g.reshape(1, d))
    return out.reshape(bsz, seq, d)


def kernel(x, c, ada_w, ada_b, norm1_g, norm2_g, gla_w_in, gla_w_gate2, gla_b_gate, gla_norm_g, gla_w_out, kv_norm_g, kv_ada_w, kv_ada_b, kv_w, cmp_pos, cmp_w1, cmp_b1, cmp_w2, cmp_b2, nsa_w_in, nsa_b_gate, nsa_w_out, router_w, router_b, moe_w_gate_up, moe_b_gate_up, moe_w_down, moe_b_down, final_g):
    bsz, seq, d = x.shape
    mod = _ada_vectors(c, ada_w, ada_b)
    kv_mod = _ada_vectors(c, kv_ada_w[None], kv_ada_b[None])[0]
    vec = lambda m, j: m[:, None, j * d:(j + 1) * d]
    shared = None
    for layer in range(DEPTH):
        m = mod[layer]
        sh1, sc1, g1, sh2, sc2, g2 = (vec(m, j) for j in range(6))
        if layer < N_A_LAYERS:
            i = layer
            q, k, v, r, la = _gla_proj(x, norm1_g[layer], sh1, sc1, gla_w_in[i], gla_w_gate2[i], gla_b_gate[i])
            o = _gla_core(q, k, v, r, la, gla_norm_g[i])
            x = _res_matmul(o, gla_w_out[i], x, g1)
        else:
            i = layer - N_A_LAYERS
            o = _nsa_mixer(x, norm1_g[layer], sh1, sc1, shared, nsa_w_in[i], nsa_b_gate[i])
            x = _res_matmul(o, nsa_w_out[i], x, g1)
        x = _moe_layer(x, norm2_g[layer], sh2, sc2, g2, router_w[layer], router_b[layer],
                       layer, moe_w_gate_up, moe_b_gate_up[layer], moe_w_down, moe_b_down[layer])
        if layer == N_A_LAYERS - 1:
            kv_cmp, kv = _kv_proj(x, kv_norm_g, vec(kv_mod, 0), vec(kv_mod, 1), kv_w)
            shared = (_compress(kv_cmp, cmp_pos, cmp_w1, cmp_b1, cmp_w2, cmp_b2), kv)
    return _final_norm(x, final_g)
```

```python
import functools

import numpy as np
import jax
import jax.numpy as jnp
from jax import lax
from jax.experimental import pallas as pl
from jax.experimental.pallas import tpu as pltpu

F32 = jnp.float32
BF16 = jnp.bfloat16
HIGHEST = lax.Precision.HIGHEST

D_MODEL = 1024
DEPTH = 4
N_A_LAYERS = DEPTH // 2
RMS_EPS = 1e-5

GLA_HEADS = 4
GLA_KEY_DIM = D_MODEL // 2
GLA_VAL_DIM = D_MODEL
GLA_HK = GLA_KEY_DIM // GLA_HEADS
GLA_HV = GLA_VAL_DIM // GLA_HEADS
GLA_RANK = 16
GLA_TAU = 16.0
GLA_CHUNK = 64

NSA_HEADS = 16
NSA_GROUPS = 4
NSA_HPG = NSA_HEADS // NSA_GROUPS
NSA_HD = D_MODEL // NSA_HEADS
CMP_LEN = 32
CMP_STRIDE = 16
CMP_HIDDEN = 2 * NSA_HD
SLC_LEN = 64
SLC_TOPK = 16
WIN = 512
NSA_Q_BLOCK = 64

N_EXPERTS = 32
TOP_K = 4
EXPERT_FF = D_MODEL
SWIGLU_LIMIT = 7.0
SWIGLU_ALPHA = 1.702

FORCE = 1e4
NEG = -1e30

VMEM_LIMIT_BYTES = 56 * 1024 * 1024

ROW_TILE = 512
EXPERT_TILE = 256
LANES = 128
ROW_CHUNK = 8
ROUTE_SPARE_ROWS = 96
ROUTE_TILE = 512
GLA_STEP = 512
SLC_KEY_TILE = 512
NSA_STEP = 256
NSA_ROW_CHUNK = 64
WIN_PAD = WIN
MASK_BIG = 1e30


def _params(*sem):
    return pltpu.CompilerParams(dimension_semantics=sem, vmem_limit_bytes=VMEM_LIMIT_BYTES)


def _norm_mod(x, g, shift, scale):
    y = x * lax.rsqrt(jnp.mean(x * x, axis=-1, keepdims=True) + RMS_EPS)
    return (y * g) * (1.0 + scale) + shift


def _dot(a, b):
    return jnp.dot(a, b, preferred_element_type=F32)


def _dot_nt(a, b, precision=None):
    return lax.dot_general(a, b, (((1,), (1,)), ((), ())), precision=precision,
                           preferred_element_type=F32)


def _ada_kernel(c_ref, w_ref, b_ref, o_ref):
    c = c_ref[...]
    cs = c * jax.nn.sigmoid(c)
    o_ref[0] = jnp.dot(cs, w_ref[0], precision=HIGHEST, preferred_element_type=F32) + b_ref[0]


def _ada_vectors(c, w, b):
    n_l, d, m = w.shape
    bsz = c.shape[0]
    tn = 1024
    return pl.pallas_call(
        _ada_kernel,
        grid=(n_l, m // tn),
        in_specs=[
            pl.BlockSpec((bsz, d), lambda l, j: (0, 0)),
            pl.BlockSpec((1, d, tn), lambda l, j: (l, 0, j)),
            pl.BlockSpec((1, 1, tn), lambda l, j: (l, 0, j)),
        ],
        out_specs=pl.BlockSpec((1, bsz, tn), lambda l, j: (l, 0, j)),
        out_shape=jax.ShapeDtypeStruct((n_l, bsz, m), F32),
        compiler_params=_params("parallel", "parallel"),
        name="ada_vectors",
    )(c, w, b.reshape(n_l, 1, m))


def _gla_proj_kernel(x_ref, g_ref, sh_ref, sc_ref, wq_ref, wk_ref, wv_ref, wr_ref, wlr_ref,
                     wg2_ref, bg_ref, q_ref, k_ref, v_ref, r_ref, la_ref):
    h = _norm_mod(x_ref[0], g_ref[...], sh_ref[0], sc_ref[0]).astype(BF16)
    q_ref[0] = _dot(h, wq_ref[...]).astype(BF16)
    k_ref[0] = _dot(h, wk_ref[...]).astype(BF16)
    v_ref[0] = _dot(h, wv_ref[...]).astype(BF16)
    r_ref[0] = _dot(h, wr_ref[...]).astype(BF16)
    g_lr = _dot(h, wlr_ref[...])
    z = jnp.dot(g_lr, wg2_ref[...], precision=HIGHEST, preferred_element_type=F32) + bg_ref[...]
    log_sig = jnp.minimum(z, 0.0) - jnp.log(1.0 + jnp.exp(-jnp.abs(z)))
    la_ref[0] = log_sig / GLA_TAU


def _gla_proj(x, g, shift, scale, w_in, w_gate2, b_gate):
    bsz, seq, d = x.shape
    tm = min(ROW_TILE, seq)
    kd, vd = GLA_KEY_DIM, GLA_VAL_DIM
    wq = w_in[:, :kd].astype(BF16)
    wk = w_in[:, kd:2 * kd].astype(BF16)
    wv = w_in[:, 2 * kd:2 * kd + vd].astype(BF16)
    wr = w_in[:, 2 * kd + vd:2 * kd + 2 * vd].astype(BF16)
    wlr = w_in[:, 2 * kd + 2 * vd:].astype(BF16)
    full = lambda shape: pl.BlockSpec(shape, lambda b, i: (0,) * len(shape))
    row = lambda n: pl.BlockSpec((1, tm, n), lambda b, i: (b, i, 0))
    vec = pl.BlockSpec((1, 1, d), lambda b, i: (b, 0, 0))
    return pl.pallas_call(
        _gla_proj_kernel,
        grid=(bsz, seq // tm),
        in_specs=[row(d), full((1, d)), vec, vec, full((d, kd)), full((d, kd)), full((d, vd)),
                  full((d, vd)), full((d, GLA_RANK)), full((GLA_RANK, kd)), full((1, kd))],
        out_specs=[row(kd), row(kd), row(vd), row(vd), row(kd)],
        out_shape=[jax.ShapeDtypeStruct((bsz, seq, kd), BF16),
                   jax.ShapeDtypeStruct((bsz, seq, kd), BF16),
                   jax.ShapeDtypeStruct((bsz, seq, vd), BF16),
                   jax.ShapeDtypeStruct((bsz, seq, vd), BF16),
                   jax.ShapeDtypeStruct((bsz, seq, kd), F32)],
        compiler_params=_params("parallel", "parallel"),
        name="gla_proj",
    )(x, g.reshape(1, d), shift, scale, wq, wk, wv, wr, wlr, w_gate2, b_gate.reshape(1, kd))


def _gla_core_kernel(q_ref, k_ref, v_ref, r_ref, la_ref, ng_ref, o_ref, state_ref, *, n_chunks):
    @pl.when(pl.program_id(1) == 0)
    def _():
        state_ref[...] = jnp.zeros_like(state_ref)

    c_len = GLA_CHUNK
    row = lax.broadcasted_iota(jnp.int32, (c_len, c_len), 0)
    col = lax.broadcasted_iota(jnp.int32, (c_len, c_len), 1)
    causal = col <= row
    tril = causal.astype(F32)

    def chunk(c, carry):
        c0 = pl.multiple_of(c * c_len, c_len)
        for hh in range(GLA_HEADS):
            kc = slice(hh * GLA_HK, (hh + 1) * GLA_HK)
            vc = slice(hh * GLA_HV, (hh + 1) * GLA_HV)
            la = la_ref[0, pl.ds(c0, c_len), kc]
            b = jnp.dot(tril, la, precision=HIGHEST, preferred_element_type=F32)
            q = q_ref[0, pl.ds(c0, c_len), kc].astype(F32) * (GLA_HK ** -0.5)
            k = k_ref[0, pl.ds(c0, c_len), kc].astype(F32)
            v = v_ref[0, pl.ds(c0, c_len), vc]
            q_dec = (q * jnp.exp(b)).astype(BF16)
            k_intra = (k * jnp.exp(-b)).astype(BF16)
            b_t = b.T
            bl_t = b_t[:, c_len - 1:c_len]
            k_inter_t = (k.T * jnp.exp(bl_t - b_t)).astype(BF16)
            att = jnp.where(causal, _dot_nt(q_dec, k_intra), 0.0).astype(BF16)
            state = state_ref[hh]
            o = _dot(att, v) + _dot(q_dec, state.astype(BF16))
            state_ref[hh] = state * jnp.exp(bl_t) + _dot(k_inter_t, v)
            o = o * lax.rsqrt(jnp.mean(o * o, axis=-1, keepdims=True) + RMS_EPS)
            r = r_ref[0, pl.ds(c0, c_len), vc].astype(F32)
            o_ref[0, pl.ds(c0, c_len), vc] = ((o * ng_ref[:, vc]) * (r * jax.nn.sigmoid(r))).astype(BF16)
        return carry

    lax.fori_loop(0, n_chunks, chunk, 0)


def _gla_core(q, k, v, r, la, norm_g):
    bsz, seq, _ = q.shape
    ts = min(GLA_STEP, seq)
    kern = functools.partial(_gla_core_kernel, n_chunks=ts // GLA_CHUNK)
    kd = lambda: pl.BlockSpec((1, ts, GLA_KEY_DIM), lambda b, s: (b, s, 0))
    vd = lambda: pl.BlockSpec((1, ts, GLA_VAL_DIM), lambda b, s: (b, s, 0))
    return pl.pallas_call(
        kern,
        grid=(bsz, seq // ts),
        in_specs=[kd(), kd(), vd(), vd(), kd(), pl.BlockSpec((1, GLA_VAL_DIM), lambda b, s: (0, 0))],
        out_specs=vd(),
        out_shape=jax.ShapeDtypeStruct((bsz, seq, GLA_VAL_DIM), BF16),
        scratch_shapes=[pltpu.VMEM((GLA_HEADS, GLA_HK, GLA_HV), F32)],
        compiler_params=_params("parallel", "arbitrary"),
        name="gla_core",
    )(q, k, v, r, la, norm_g.reshape(1, GLA_VAL_DIM))


def _res_matmul_kernel(a_ref, w_ref, x_ref, gate_ref, o_ref):
    o_ref[0] = x_ref[0] + gate_ref[0] * _dot(a_ref[0], w_ref[...])


def _res_matmul(a, w, x, gate):
    bsz, seq, d = x.shape
    kdim = a.shape[-1]
    tm = min(ROW_TILE, seq)
    return pl.pallas_call(
        _res_matmul_kernel,
        grid=(bsz, seq // tm),
        in_specs=[pl.BlockSpec((1, tm, kdim), lambda b, i: (b, i, 0)),
                  pl.BlockSpec((kdim, d), lambda b, i: (0, 0)),
                  pl.BlockSpec((1, tm, d), lambda b, i: (b, i, 0)),
                  pl.BlockSpec((1, 1, d), lambda b, i: (b, 0, 0))],
        out_specs=pl.BlockSpec((1, tm, d), lambda b, i: (b, i, 0)),
        out_shape=jax.ShapeDtypeStruct((bsz, seq, d), F32),
        compiler_params=_params("parallel", "parallel"),
        name="res_matmul",
    )(a, w.astype(BF16), x, gate)


def _route_kernel(x_ref, g_ref, sh_ref, sc_ref, rwt_ref, rb_ref, xs_ref, w_ref, slot_ref, cnt_ref):
    tm = x_ref.shape[0]
    n_slot = xs_ref.shape[0]
    h = _norm_mod(x_ref[...], g_ref[...], sh_ref[0], sc_ref[0])
    logits = _dot_nt(rwt_ref[...], h, precision=HIGHEST) + rb_ref[...]
    e_iota = lax.broadcasted_iota(jnp.int32, logits.shape, 0)
    vals, hots = [], []
    for _ in range(TOP_K):
        m = jnp.max(logits, axis=0, keepdims=True)
        idx = jnp.min(jnp.where(logits == m, e_iota, N_EXPERTS), axis=0, keepdims=True)
        hot = e_iota == idx
        vals.append(m)
        hots.append(hot)
        logits = jnp.where(hot, -jnp.inf, logits)
    exps = [jnp.exp(v - vals[0]) for v in vals]
    denom = exps[0] + exps[1] + exps[2] + exps[3]
    for kk in range(TOP_K):
        w_ref[kk:kk + 1, :] = exps[kk] / denom

    sel = (hots[0] | hots[1] | hots[2] | hots[3]).astype(F32)
    s_iota = lax.broadcasted_iota(jnp.int32, (tm, tm), 0)
    t_iota = lax.broadcasted_iota(jnp.int32, (tm, tm), 1)
    before = (s_iota < t_iota).astype(BF16)
    prefix = _dot(sel.astype(BF16), before)
    cnt = jnp.sum(sel, axis=1, keepdims=True)
    chunks = jnp.floor((cnt + (ROW_CHUNK - 1.0)) * (1.0 / ROW_CHUNK))
    lower = (lax.broadcasted_iota(jnp.int32, (N_EXPERTS, N_EXPERTS), 0) >
             lax.broadcasted_iota(jnp.int32, (N_EXPERTS, N_EXPERTS), 1)).astype(BF16)
    seg0 = _dot(lower, jnp.broadcast_to(chunks, (N_EXPERTS, LANES)).astype(BF16))[:, 0:1] * ROW_CHUNK
    slots = [jnp.sum(jnp.where(hot, prefix + seg0, 0.0), axis=0, keepdims=True).astype(jnp.int32)
             for hot in hots]
    r_iota = lax.broadcasted_iota(jnp.int32, (n_slot, tm), 0)
    q = jnp.zeros((n_slot, tm), F32)
    for kk in range(TOP_K):
        q = q + jnp.where(r_iota == slots[kk], 1.0, 0.0)
    for kk in range(TOP_K):
        slot_ref[kk:kk + 1, :] = slots[kk]
    xs_ref[...] = _dot((q > 0.0).astype(BF16), h.astype(BF16))

    cnt_ref[0] = jnp.broadcast_to(cnt, (N_EXPERTS, LANES)).astype(jnp.int32)


def _route(x2d, g, shift, scale, router_w, router_b, seq):
    n_tok, d = x2d.shape
    tm = min(ROUTE_TILE, seq)
    per_b = seq // tm
    n_win = n_tok // tm
    n_slot = tm * TOP_K + N_EXPERTS * ROW_CHUNK + ROUTE_SPARE_ROWS
    vec = pl.BlockSpec((1, 1, d), lambda i: (i // per_b, 0, 0))
    return pl.pallas_call(
        _route_kernel,
        grid=(n_win,),
        in_specs=[pl.BlockSpec((tm, d), lambda i: (i, 0)),
                  pl.BlockSpec((1, d), lambda i: (0, 0)), vec, vec,
                  pl.BlockSpec((N_EXPERTS, d), lambda i: (0, 0)),
                  pl.BlockSpec((N_EXPERTS, 1), lambda i: (0, 0))],
        out_specs=[pl.BlockSpec((n_slot, d), lambda i: (i, 0)),
                   pl.BlockSpec((TOP_K, tm), lambda i: (0, i)),
                   pl.BlockSpec((TOP_K, tm), lambda i: (0, i)),
                   pl.BlockSpec((1, N_EXPERTS, LANES), lambda i: (i, 0, 0))],
        out_shape=[jax.ShapeDtypeStruct((n_win * n_slot, d), F32),
                   jax.ShapeDtypeStruct((TOP_K, n_tok), F32),
                   jax.ShapeDtypeStruct((TOP_K, n_tok), jnp.int32),
                   jax.ShapeDtypeStruct((n_win, N_EXPERTS, LANES), jnp.int32)],
        compiler_params=_params("parallel"),
        name="moe_route",
    )(x2d, g.reshape(1, d), shift, scale, router_w.T, router_b.reshape(N_EXPERTS, 1))


def _expert_kernel(te_ref, nu_ref, src_hbm, dst_hbm, xs_in, wgu_ref, wd_ref, bg_ref, bu_ref, bd_ref, xy_hbm,
                   wg_s, wu_s, wd_s, tr_s, xbuf, ybuf, tok_s, out_s, tok_sem, out_sem, g_sem, s_sem):
    del xs_in
    i = pl.program_id(0)
    n_tiles = pl.num_programs(0)
    tm = xbuf.shape[1]
    used = i < nu_ref[0]
    fresh = (i == 0) | (te_ref[i] != te_ref[jnp.maximum(i - 1, 0)])
    slot = i % 2
    other = 1 - slot

    def tok_copy(t, sl):
        return pltpu.make_async_copy(src_hbm.at[t], tok_s.at[sl], tok_sem.at[sl])

    def out_copy(t, sl):
        return pltpu.make_async_copy(dst_hbm.at[t], out_s.at[sl], out_sem.at[sl])

    def gather_start(sl):
        for c in range(tm // ROW_CHUNK):
            src = pl.multiple_of(tok_s[sl, c] * ROW_CHUNK, ROW_CHUNK)
            pltpu.make_async_copy(xy_hbm.at[pl.ds(src, ROW_CHUNK), :],
                                  xbuf.at[sl, pl.ds(c * ROW_CHUNK, ROW_CHUNK), :], g_sem.at[sl]).start()

    def gather_wait(sl):
        pltpu.make_async_copy(xy_hbm.at[pl.ds(0, tm), :], xbuf.at[sl], g_sem.at[sl]).wait()

    def scatter_start(sl):
        for c in range(tm // ROW_CHUNK):
            dst = pl.multiple_of(out_s[sl, c] * ROW_CHUNK, ROW_CHUNK)
            pltpu.make_async_copy(ybuf.at[sl, pl.ds(c * ROW_CHUNK, ROW_CHUNK), :],
                                  xy_hbm.at[pl.ds(dst, ROW_CHUNK), :], s_sem.at[sl]).start()

    def scatter_wait(sl):
        pltpu.make_async_copy(ybuf.at[sl], xy_hbm.at[pl.ds(0, tm), :], s_sem.at[sl]).wait()

    @pl.when(i == 0)
    def _():
        tok_copy(0, 0).start()
        out_copy(0, 0).start()
        tok_copy(0, 0).wait()
        gather_start(0)
        tok_copy(jnp.minimum(1, n_tiles - 1), 1).start()

    @pl.when(used & fresh)
    def _():
        n_slab, chunk, lanes = tr_s.shape
        half = chunk // 2
        for c in range(wgu_ref.shape[2] // chunk):
            t = wgu_ref[0, :, c * chunk:(c + 1) * chunk].T
            for j in range(n_slab):
                tr_s[j] = t[:, j * lanes:(j + 1) * lanes]
            for j in range(n_slab):
                wg_s[c * half:(c + 1) * half, j * lanes:(j + 1) * lanes] = (
                    tr_s[j, pl.ds(0, half, stride=2), :].astype(BF16))
                wu_s[c * half:(c + 1) * half, j * lanes:(j + 1) * lanes] = (
                    tr_s[j, pl.ds(1, half, stride=2), :].astype(BF16))
        wd_s[...] = wd_ref[0].astype(BF16)

    @pl.when(used)
    def _():
        nxt = jnp.minimum(i + 1, n_tiles - 1)
        tok_copy(nxt, other).wait()
        gather_start(other)
        tok_copy(jnp.minimum(i + 2, n_tiles - 1), slot).start()
        gather_wait(slot)
        xb = xbuf[slot].astype(BF16)
        gate = jnp.minimum(_dot_nt(xb, wg_s[...]) + bg_ref[0], SWIGLU_LIMIT)
        up = jnp.clip(_dot_nt(xb, wu_s[...]) + bu_ref[0], -SWIGLU_LIMIT, SWIGLU_LIMIT)
        act = (up + 1.0) * gate * jax.nn.sigmoid(SWIGLU_ALPHA * gate)
        ybuf[slot] = _dot(act.astype(BF16), wd_s[...]) + bd_ref[0]
        out_copy(i, slot).wait()
        scatter_start(slot)
        out_copy(nxt, other).start()

    @pl.when(used & (i > 0))
    def _():
        scatter_wait(other)

    @pl.when(i == nu_ref[0] - 1)
    def _():
        gather_wait(other)
        scatter_wait(slot)
        tok_copy(0, slot).wait()
        out_copy(0, other).wait()


def _experts(xs, src_tiles, dst_tiles, tile_expert, n_used, layer, w_gu, w_down, bg, bu, bd):
    d = xs.shape[1]
    n_tiles = src_tiles.shape[0]
    tm = EXPERT_TILE
    ff = w_down.shape[2]
    wsel = lambda i, te, nu: (te[i], 0, 0)
    lsel = lambda i, te, nu: (layer, te[i], 0, 0)
    hbm = pl.BlockSpec(memory_space=pl.ANY)
    return pl.pallas_call(
        _expert_kernel,
        grid_spec=pltpu.PrefetchScalarGridSpec(
            num_scalar_prefetch=2,
            grid=(n_tiles,),
            in_specs=[hbm, hbm, hbm,
                      pl.BlockSpec((None, 1, d, 2 * ff), lsel),
                      pl.BlockSpec((None, 1, ff, d), lsel),
                      pl.BlockSpec((1, 1, ff), wsel), pl.BlockSpec((1, 1, ff), wsel),
                      pl.BlockSpec((1, 1, d), wsel)],
            out_specs=hbm,
            scratch_shapes=[pltpu.VMEM((ff, d), BF16), pltpu.VMEM((ff, d), BF16),
                            pltpu.VMEM((ff, d), BF16), pltpu.VMEM((d // 128, 256, 128), F32),
                            pltpu.VMEM((2, tm, d), F32), pltpu.VMEM((2, tm, d), F32),
                            pltpu.SMEM((2, tm // ROW_CHUNK), jnp.int32),
                            pltpu.SMEM((2, tm // ROW_CHUNK), jnp.int32),
                            pltpu.SemaphoreType.DMA((2,)), pltpu.SemaphoreType.DMA((2,)),
                            pltpu.SemaphoreType.DMA((2,)), pltpu.SemaphoreType.DMA((2,))]),
        out_shape=jax.ShapeDtypeStruct(xs.shape, xs.dtype),
        input_output_aliases={4: 0},
        compiler_params=_params("arbitrary"),
        name="moe_experts",
    )(tile_expert, n_used, src_tiles, dst_tiles, xs, w_gu, w_down, bg, bu, bd)


def _combine_kernel(y_ref, slot_ref, w_ref, x_ref, gate_ref, o_ref):
    tm = x_ref.shape[0]
    n_slot = y_ref.shape[0]
    r_iota = lax.broadcasted_iota(jnp.int32, (tm, n_slot), 1)
    w = w_ref[...]
    slot = slot_ref[...]
    c = jnp.zeros((tm, n_slot), F32)
    for kk in range(TOP_K):
        c = c + jnp.where(r_iota == slot[:, kk:kk + 1], w[:, kk:kk + 1], 0.0)
    y = _dot(c.astype(BF16), y_ref[...].astype(BF16))
    o_ref[...] = x_ref[...] + gate_ref[0] * y


def _combine(y, slots, w_tok, x2d, gate, seq):
    n_tok, d = x2d.shape
    tm = min(ROUTE_TILE, seq)
    per_b = seq // tm
    n_win = n_tok // tm
    n_slot = y.shape[0] // n_win
    return pl.pallas_call(
        _combine_kernel,
        grid=(n_win,),
        in_specs=[pl.BlockSpec((n_slot, d), lambda i: (i, 0)),
                  pl.BlockSpec((tm, TOP_K), lambda i: (i, 0)),
                  pl.BlockSpec((tm, TOP_K), lambda i: (i, 0)),
                  pl.BlockSpec((tm, d), lambda i: (i, 0)),
                  pl.BlockSpec((1, 1, d), lambda i: (i // per_b, 0, 0))],
        out_specs=pl.BlockSpec((tm, d), lambda i: (i, 0)),
        out_shape=jax.ShapeDtypeStruct((n_tok, d), F32),
        compiler_params=_params("parallel"),
        name="moe_combine",
    )(y, slots, w_tok, x2d, gate)


def _moe_layer(x, norm_g, shift, scale, gate, router_w, router_b, layer, w_gu, b_gu, w_down, b_down):
    bsz, seq, d = x.shape
    n_tok = bsz * seq
    x2d = x.reshape(n_tok, d)
    xs, w_t, slot_t, cnt = _route(x2d, norm_g, shift, scale, router_w, router_b, seq)

    cnt = cnt[:, :, 0]
    n_win = cnt.shape[0]
    win_chunks = xs.shape[0] // n_win // ROW_CHUNK
    tile_chunks = EXPERT_TILE // ROW_CHUNK
    c8 = (cnt + ROW_CHUNK - 1) // ROW_CHUNK
    seg_off = jnp.cumsum(c8, axis=1) - c8
    cum_w = jnp.cumsum(c8, axis=0)
    total = cum_w[-1]
    padded = (total + tile_chunks - 1) // tile_chunks * tile_chunks
    ends = jnp.cumsum(padded)
    starts = ends - padded
    n_chunk = n_tok * TOP_K // ROW_CHUNK + n_win * N_EXPERTS + N_EXPERTS * tile_chunks
    n_tiles = n_chunk // tile_chunks
    gq = jnp.arange(n_chunk, dtype=jnp.int32)
    e_of = jnp.minimum(jnp.sum((ends[None, :] <= gq[:, None]).astype(jnp.int32), axis=1), N_EXPERTS - 1)
    e_hot = (e_of[:, None] == jnp.arange(N_EXPERTS, dtype=jnp.int32)[None, :]).astype(jnp.int32)
    rank = gq - jnp.sum(e_hot * starts[None, :], axis=1)
    valid = rank < jnp.sum(e_hot * total[None, :], axis=1)
    pick = lambda tab: jnp.sum(e_hot[:, :, None] * tab.T[None, :, :], axis=1)
    cum_g = pick(cum_w)
    w_of = jnp.minimum(jnp.sum((cum_g <= rank[:, None]).astype(jnp.int32), axis=1), n_win - 1)
    w_hot = (w_of[:, None] == jnp.arange(n_win, dtype=jnp.int32)[None, :]).astype(jnp.int32)
    seg_first = jnp.sum(w_hot * (cum_g - pick(c8)), axis=1)
    src = w_of * win_chunks + jnp.sum(w_hot * pick(seg_off), axis=1) + rank - seg_first
    spare0 = win_chunks - ROUTE_SPARE_ROWS // ROW_CHUNK
    u = (gq // tile_chunks % 2) * tile_chunks + gq % tile_chunks
    dst = jnp.where(valid, src, (u % n_win) * win_chunks + spare0 + u // n_win)
    src = jnp.where(valid, src, win_chunks - 1)
    tile_expert = e_of[::tile_chunks]
    n_used = (ends[-1:] // tile_chunks).astype(jnp.int32)

    y = _experts(xs, src.reshape(n_tiles, tile_chunks), dst.reshape(n_tiles, tile_chunks), tile_expert,
                 n_used, layer, w_gu, w_down, b_gu[:, None, 0::2], b_gu[:, None, 1::2], b_down[:, None, :])
    out = _combine(y, slot_t.T, w_t.T, x2d, gate, seq)
    return out.reshape(bsz, seq, d)


def _kv_proj_kernel(x_ref, g_ref, sh_ref, sc_ref, w_ref, cmp_ref, kv_ref):
    h = _norm_mod(x_ref[0], g_ref[...], sh_ref[0], sc_ref[0]).astype(BF16)
    kv = _dot(h, w_ref[...])
    for t in range(6):
        for g in range(NSA_GROUPS):
            c0 = (t * NSA_GROUPS + g) * NSA_HD
            piece = kv[:, c0:c0 + NSA_HD]
            if t < 2:
                cmp_ref[t, 0, g] = piece
            else:
                kv_ref[t - 2, 0, g] = piece.astype(BF16)


def _kv_proj(x, g, shift, scale, kv_w):
    bsz, seq, d = x.shape
    tm = min(ROW_TILE, seq)
    n_out = kv_w.shape[1]
    vec = pl.BlockSpec((1, 1, d), lambda b, i: (b, 0, 0))
    return pl.pallas_call(
        _kv_proj_kernel,
        grid=(bsz, seq // tm),
        in_specs=[pl.BlockSpec((1, tm, d), lambda b, i: (b, i, 0)),
                  pl.BlockSpec((1, d), lambda b, i: (0, 0)), vec, vec,
                  pl.BlockSpec((d, n_out), lambda b, i: (0, 0))],
        out_specs=[pl.BlockSpec((2, 1, NSA_GROUPS, tm, NSA_HD), lambda b, i: (0, b, 0, i, 0)),
                   pl.BlockSpec((4, 1, NSA_GROUPS, tm, NSA_HD), lambda b, i: (0, b, 0, i, 0))],
        out_shape=[jax.ShapeDtypeStruct((2, bsz, NSA_GROUPS, seq, NSA_HD), F32),
                   jax.ShapeDtypeStruct((4, bsz, NSA_GROUPS, seq, NSA_HD), BF16)],
        compiler_params=_params("parallel", "parallel"),
        name="nsa_kv_proj",
    )(x, g.reshape(1, d), shift, scale, kv_w.astype(BF16))


def _compress_kernel(x_ref, pos_ref, w1_ref, b1_ref, w2_ref, b2_ref, o_ref, *, n_rows):
    for g in range(NSA_GROUPS):
        first = jnp.zeros((n_rows, CMP_HIDDEN), F32)
        second = jnp.zeros((n_rows, CMP_HIDDEN), F32)
        for l in range(CMP_STRIDE):
            rows = x_ref[0, 0, g, pl.ds(l, n_rows, stride=CMP_STRIDE), :]
            first = first + _dot((rows + pos_ref[0, l:l + 1, :]).astype(BF16), w1_ref[0, l])
            second = second + _dot((rows + pos_ref[0, CMP_STRIDE + l:CMP_STRIDE + l + 1, :]).astype(BF16),
                                   w1_ref[0, CMP_STRIDE + l])
        pre = first + pltpu.roll(second, n_rows - 1, 0) + b1_ref[0]
        hid = 0.5 * pre * (1.0 + jnp.tanh(0.7978845608028654 * (pre + 0.044715 * pre * pre * pre)))
        o_ref[0, 0, g] = (_dot(hid.astype(BF16), w2_ref[0]) + b2_ref[0]).astype(BF16)


def _compress(kv_cmp, cmp_pos, cmp_w1, cmp_b1, cmp_w2, cmp_b2):
    _, bsz, _, seq, _ = kv_cmp.shape
    n_rows = seq // CMP_STRIDE
    kern = functools.partial(_compress_kernel, n_rows=n_rows)
    w1 = cmp_w1.reshape(2, CMP_LEN, NSA_HD, CMP_HIDDEN).astype(BF16)
    return pl.pallas_call(
        kern,
        grid=(2, bsz),
        in_specs=[pl.BlockSpec((1, 1, NSA_GROUPS, seq, NSA_HD), lambda t, b: (t, b, 0, 0, 0)),
                  pl.BlockSpec((1, CMP_LEN, NSA_HD), lambda t, b: (t, 0, 0)),
                  pl.BlockSpec((1, CMP_LEN, NSA_HD, CMP_HIDDEN), lambda t, b: (t, 0, 0, 0)),
                  pl.BlockSpec((1, 1, CMP_HIDDEN), lambda t, b: (t, 0, 0)),
                  pl.BlockSpec((1, CMP_HIDDEN, NSA_HD), lambda t, b: (t, 0, 0)),
                  pl.BlockSpec((1, 1, NSA_HD), lambda t, b: (t, 0, 0))],
        out_specs=pl.BlockSpec((1, 1, NSA_GROUPS, n_rows, NSA_HD), lambda t, b: (t, b, 0, 0, 0)),
        out_shape=jax.ShapeDtypeStruct((2, bsz, NSA_GROUPS, n_rows, NSA_HD), BF16),
        compiler_params=_params("parallel", "parallel"),
        name="nsa_compress",
    )(kv_cmp, cmp_pos, w1, cmp_b1[:, None, :], cmp_w2.astype(BF16), cmp_b2[:, None, :])


def _nsa_proj_kernel(x_ref, g_ref, sh_ref, sc_ref, wq_ref, wg_ref, bg_ref, q_ref, gate_ref):
    h = _norm_mod(x_ref[0], g_ref[...], sh_ref[0], sc_ref[0]).astype(BF16)
    q_ref[0] = (_dot(h, wq_ref[...]) * (NSA_HD ** -0.5)).astype(BF16)
    gate_ref[0] = jax.nn.sigmoid(_dot(h, wg_ref[...]) + bg_ref[...])


def _nsa_proj(x, g, shift, scale, w_in, b_gate):
    bsz, seq, d = x.shape
    tm = min(ROW_TILE, seq)
    nq = NSA_HEADS * NSA_HD
    ng = 3 * NSA_HEADS
    vec = pl.BlockSpec((1, 1, d), lambda b, i: (b, 0, 0))
    return pl.pallas_call(
        _nsa_proj_kernel,
        grid=(bsz, seq // tm),
        in_specs=[pl.BlockSpec((1, tm, d), lambda b, i: (b, i, 0)),
                  pl.BlockSpec((1, d), lambda b, i: (0, 0)), vec, vec,
                  pl.BlockSpec((d, nq), lambda b, i: (0, 0)),
                  pl.BlockSpec((d, ng), lambda b, i: (0, 0)),
                  pl.BlockSpec((1, ng), lambda b, i: (0, 0))],
        out_specs=[pl.BlockSpec((1, tm, nq), lambda b, i: (b, i, 0)),
                   pl.BlockSpec((1, tm, ng), lambda b, i: (b, i, 0))],
        out_shape=[jax.ShapeDtypeStruct((bsz, seq, nq), BF16),
                   jax.ShapeDtypeStruct((bsz, seq, ng), F32)],
        compiler_params=_params("parallel", "parallel"),
        name="nsa_proj",
    )(x, g.reshape(1, d), shift, scale, w_in[:, :nq].astype(BF16), w_in[:, nq:].astype(BF16),
      b_gate.reshape(1, ng))


def _nsa_attn_kernel(q_ref, gate_ref, qc_ref, slope_ref, ovl_ref, kconst_ref, cw_ref, kc_ref, vc_ref,
                     ks_ref, vs_ref, kw_ref, vw_ref, o_ref, ksa, vsa, kwa, vwa, score_s,
                     sc_s, pc_s, psum_s, qa_s, m_s, acc_s, sa_s, sb_s, ps_s, sw_s, pw_s):
    qi = pl.program_id(2)
    qs = NSA_STEP
    hd = NSA_HD
    rows = NSA_HPG * qs
    seq = ks_ref.shape[2]
    q0 = pl.multiple_of(qi * qs, qs)

    @pl.when(qi == 0)
    def _():
        ksa[...] = kconst_ref[...]
        ksa[:, 0:hd] = ks_ref[0, 0]
        one_col = (lax.broadcasted_iota(jnp.int32, (seq, 128), 1) == hd).astype(BF16)
        vsa[...] = one_col
        vsa[:, 0:hd] = vs_ref[0, 0]
        flag_col = (lax.broadcasted_iota(jnp.int32, (WIN_PAD, 128), 1) == hd).astype(BF16)
        kwa[0:WIN_PAD, :] = flag_col
        kwa[WIN_PAD:, :] = jnp.zeros((seq, 128), BF16)
        kwa[WIN_PAD:, 0:hd] = kw_ref[0, 0]
        vwa[0:WIN_PAD, :] = jnp.zeros((WIN_PAD, 128), BF16)
        vwa[WIN_PAD:, :] = one_col
        vwa[WIN_PAD:, 0:hd] = vw_ref[0, 0]

    qt = q_ref[0].astype(F32)
    q32 = jnp.concatenate([qt[:, hh * hd:(hh + 1) * hd] for hh in range(NSA_HPG)], axis=0)
    q = q32.astype(BF16)
    slope = slope_ref[0]
    t_row = q0 + lax.broadcasted_iota(jnp.int32, (rows, 1), 0) % qs

    n_cmp = kc_ref.shape[2]
    cmp_end = lax.broadcasted_iota(jnp.int32, (1, n_cmp), 1) * CMP_STRIDE + (CMP_LEN - 1)
    cmp_bias = cmp_end.astype(F32)
    sc_s[...] = _dot_nt(q, kc_ref[0, 0])
    rc = NSA_ROW_CHUNK
    for c in range(rows // rc):
        r = slice(c * rc, (c + 1) * rc)
        ok_c = cmp_end <= t_row[r]
        s_c = jnp.where(ok_c, sc_s[r, :] + slope[r] * cmp_bias, NEG)
        m_c = jnp.max(s_c, axis=-1, keepdims=True)
        e_c = jnp.where(ok_c, jnp.exp(s_c - m_c), 0.0)
        l_c = jnp.sum(e_c, axis=-1, keepdims=True)
        p_c = e_c * (1.0 / jnp.where(l_c > 0.0, l_c, 1.0))
        pc_s[r, :] = p_c.astype(BF16)
        qr = slice((c * rc) % qs, (c * rc) % qs + rc)
        if c * rc < qs:
            psum_s[qr, :] = p_c
        else:
            psum_s[qr, :] = psum_s[qr, :] + p_c
    o_c = _dot(pc_s[...], vc_ref[0, 0])

    imp_t = _dot_nt(ovl_ref[...], psum_s[...], precision=HIGHEST)
    n_slc = imp_t.shape[0]
    blk = lax.broadcasted_iota(jnp.int32, (n_slc, qs), 0)
    cur = (q0 + lax.broadcasted_iota(jnp.int32, (1, qs), 1)) // SLC_LEN
    forced = (blk == 0) | (blk == cur) | (blk == cur - 1)
    score = jnp.where(blk <= cur, imp_t + jnp.where(forced, FORCE, 0.0), -FORCE)
    score_s[...] = score

    def rank_step(ip, rank):
        for i in (2 * ip, 2 * ip + 1):
            other = score_s[pl.ds(i, 1), :]
            ahead = (other > score) | ((other == score) & (blk > i))
            rank = rank + ahead.astype(F32)
        return rank

    last_blk = (q0 + qs - 1) // SLC_LEN
    n_rank = jnp.where(last_blk >= SLC_TOPK, last_blk // 2 + 1, 0)
    rank = lax.fori_loop(0, n_rank, rank_step, jnp.zeros((n_slc, qs), F32))
    sel_t = (rank < float(min(SLC_TOPK, n_slc))).astype(BF16)
    eye = (lax.broadcasted_iota(jnp.int32, (qs, qs), 0) ==
           lax.broadcasted_iota(jnp.int32, (qs, qs), 1)).astype(BF16)
    sel = _dot_nt(eye, sel_t)
    drop = (sel - 1.0) * MASK_BIG
    if n_slc < hd:
        drop = jnp.concatenate([drop, jnp.zeros((qs, hd - n_slc), F32)], axis=1)
    q_aug = jnp.concatenate([q32, jnp.concatenate([drop] * NSA_HPG, axis=0), qc_ref[0]],
                            axis=1).astype(BF16)

    tk = SLC_KEY_TILE

    n_full = q0 // tk
    qa_s[...] = q_aug
    m_s[...] = jnp.full((rows, 1), -MASK_BIG, F32)
    acc_s[...] = jnp.zeros((rows, 128), F32)

    def scores(kt, s_out):
        k0 = pl.multiple_of(kt * tk, tk)
        s_out[...] = _dot_nt(qa_s[...], ksa[pl.ds(k0, tk), :])

    def absorb(s_in, kt, causal):
        k0 = pl.multiple_of(kt * tk, tk)
        kpos = k0 + lax.broadcasted_iota(jnp.int32, (1, tk), 1)
        for c in range(rows // rc):
            r = slice(c * rc, (c + 1) * rc)
            s = s_in[r, :]
            if causal:
                s = jnp.where(kpos <= t_row[r], s, -MASK_BIG)
            m_old = m_s[r, :]
            m_new = jnp.maximum(m_old, jnp.max(s, axis=-1, keepdims=True))
            m_s[r, :] = m_new
            ps_s[r, :] = jnp.exp(s - m_new).astype(BF16)
            acc_s[r, :] = jnp.exp(m_old - m_new) * acc_s[r, :]
        acc_s[...] = acc_s[...] + _dot(ps_s[...], vsa[pl.ds(k0, tk), :])

    scores(0, sa_s)

    def slc_pair(j, carry):
        scores(2 * j + 1, sb_s)
        absorb(sa_s, 2 * j, False)

        @pl.when(2 * j + 1 < n_full)
        def _():
            scores(2 * j + 2, sa_s)
            absorb(sb_s, 2 * j + 1, False)
        return carry

    lax.fori_loop(0, (n_full + 1) // 2, slc_pair, 0)

    @pl.when(n_full % 2 == 0)
    def _():
        absorb(sa_s, n_full, True)

    @pl.when(n_full % 2 == 1)
    def _():
        absorb(sb_s, n_full, True)

    o_s = acc_s[:, 0:hd] * (1.0 / acc_s[:, hd:hd + 1])

    nw = WIN_PAD + qs
    q_w = jnp.concatenate([q32, jnp.full((rows, hd), -MASK_BIG, F32)], axis=1).astype(BF16)
    sw_s[...] = _dot_nt(q_w, kwa[pl.ds(q0, nw), :])
    for c in range(rows // rc):
        r = slice(c * rc, (c + 1) * rc)
        s_w = sw_s[r, :] + cw_ref[0, r, :]
        m_w = jnp.max(s_w, axis=-1, keepdims=True)
        pw_s[r, :] = jnp.exp(s_w - m_w).astype(BF16)
    acc_w = _dot(pw_s[...], vwa[pl.ds(q0, nw), :])
    o_w = acc_w[:, 0:hd] * (1.0 / acc_w[:, hd:hd + 1])

    gates = gate_ref[0, 0]
    outs = []
    for hh in range(NSA_HPG):
        r0 = hh * qs
        outs.append(gates[:, hh:hh + 1] * o_c[r0:r0 + qs]
                    + gates[:, NSA_HPG + hh:NSA_HPG + hh + 1] * o_s[r0:r0 + qs]
                    + gates[:, 2 * NSA_HPG + hh:2 * NSA_HPG + hh + 1] * o_w[r0:r0 + qs])
    o_ref[0] = jnp.concatenate(outs, axis=1).astype(BF16)


def _bf16_pieces(x):
    x = np.asarray(x, np.float32)
    out = []
    for _ in range(3):
        p = x.astype(BF16).astype(np.float32)
        out.append(p)
        x = x - p
    return out


def _nsa_constants(seq):
    qb, hd = NSA_STEP, NSA_HD
    n_cmp_rows = seq // CMP_STRIDE
    n_slc = seq // SLC_LEN
    cmp_start = np.arange(n_cmp_rows) * CMP_STRIDE
    cmp_end = cmp_start + CMP_LEN - 1
    slc_start = np.arange(n_slc) * SLC_LEN
    overlap = ((cmp_start[:, None] <= slc_start[None, :] + SLC_LEN - 1)
               & (cmp_end[:, None] >= slc_start[None, :])).astype(np.float32)
    start = 2.0 ** (-8.0 / NSA_HEADS)
    slopes = np.asarray(start ** np.arange(1, NSA_HEADS + 1), np.float32).reshape(NSA_GROUPS, NSA_HPG)
    slope_rows = np.repeat(slopes, qb, axis=1)

    pos = np.arange(seq)
    kconst = np.zeros((seq, 256), np.float32)
    kconst[pos, hd + pos // SLC_LEN] = 1.0
    kconst[:, 2 * hd + 0:2 * hd + 3] = (pos // 64 * 64)[:, None]
    kconst[:, 2 * hd + 3:2 * hd + 6] = (pos % 64)[:, None]
    qconst = np.zeros((NSA_GROUPS, NSA_HPG * qb, 128), np.float32)
    for j, piece in enumerate(_bf16_pieces(slope_rows)):
        qconst[:, :, j] = piece
        qconst[:, :, 3 + j] = piece
    dist = (np.arange(NSA_HPG * qb) % qb)[:, None] + WIN_PAD - np.arange(WIN_PAD + qb)[None, :]
    cw = np.where((dist >= 0) & (dist < WIN), -slope_rows[:, :, None] * dist[None].astype(np.float32),
                  -MASK_BIG).astype(np.float32)
    return (jnp.asarray(overlap.T), jnp.asarray(kconst, BF16), jnp.asarray(qconst),
            jnp.asarray(slope_rows[:, :, None]), jnp.asarray(cw))


def _nsa_attn(q, gates, kc, vc, kv):
    bsz, seq, _ = q.shape
    qb = NSA_STEP
    rows = NSA_HPG * qb
    gw = NSA_HPG * NSA_HD
    n_cmp_rows = seq // CMP_STRIDE
    n_slc = seq // SLC_LEN
    nw = WIN_PAD + qb
    ovl_t, kconst, qconst, slope_rows, cw = _nsa_constants(seq)
    cmp_spec = lambda: pl.BlockSpec((1, 1, n_cmp_rows, NSA_HD), lambda b, g, i: (b, g, 0, 0))
    kv_spec = lambda t: pl.BlockSpec((None, 1, 1, seq, NSA_HD), lambda b, g, i, t=t: (t, b, g, 0, 0))
    return pl.pallas_call(
        _nsa_attn_kernel,
        grid=(bsz, NSA_GROUPS, seq // qb),
        in_specs=[pl.BlockSpec((1, qb, gw), lambda b, g, i: (b, i, g)),
                  pl.BlockSpec((1, 1, qb, 3 * NSA_HPG), lambda b, g, i: (b, g, i, 0)),
                  pl.BlockSpec((1, rows, 128), lambda b, g, i: (g, 0, 0)),
                  pl.BlockSpec((1, rows, 1), lambda b, g, i: (g, 0, 0)),
                  pl.BlockSpec((n_slc, n_cmp_rows), lambda b, g, i: (0, 0)),
                  pl.BlockSpec((seq, 256), lambda b, g, i: (0, 0)),
                  pl.BlockSpec((1, rows, nw), lambda b, g, i: (g, 0, 0)),
                  cmp_spec(), cmp_spec(), kv_spec(0), kv_spec(1), kv_spec(2), kv_spec(3)],
        out_specs=pl.BlockSpec((1, qb, gw), lambda b, g, i: (b, i, g)),
        out_shape=jax.ShapeDtypeStruct((bsz, seq, NSA_HEADS * NSA_HD), BF16),
        scratch_shapes=[pltpu.VMEM((seq, 256), BF16), pltpu.VMEM((seq, 128), BF16),
                        pltpu.VMEM((WIN_PAD + seq, 128), BF16), pltpu.VMEM((WIN_PAD + seq, 128), BF16),
                        pltpu.VMEM((n_slc, qb), F32),
                        pltpu.VMEM((rows, n_cmp_rows), F32), pltpu.VMEM((rows, n_cmp_rows), BF16),
                        pltpu.VMEM((qb, n_cmp_rows), F32), pltpu.VMEM((rows, 256), BF16),
                        pltpu.VMEM((rows, 1), F32), pltpu.VMEM((rows, 128), F32),
                        pltpu.VMEM((rows, SLC_KEY_TILE), F32), pltpu.VMEM((rows, SLC_KEY_TILE), F32),
                        pltpu.VMEM((rows, SLC_KEY_TILE), BF16),
                        pltpu.VMEM((rows, nw), F32), pltpu.VMEM((rows, nw), BF16)],
        compiler_params=_params("parallel", "parallel", "arbitrary"),
        name="nsa_attn",
    )(q, gates, qconst, slope_rows, ovl_t, kconst, cw, kc, vc, kv, kv, kv, kv)


def _nsa_mixer(x, norm_g, shift, scale, shared, w_in, b_gate):
    kcv, kv = shared
    bsz, seq, _ = x.shape
    q, gates = _nsa_proj(x, norm_g, shift, scale, w_in, b_gate)
    gates = gates.reshape(bsz, seq, 3, NSA_GROUPS, NSA_HPG).transpose(0, 3, 1, 2, 4)
    gates = gates.reshape(bsz, NSA_GROUPS, seq, 3 * NSA_HPG)
    return _nsa_attn(q, gates, kcv[0], kcv[1], kv)


def _final_norm_kernel(x_ref, g_ref, o_ref):
    x = x_ref[...]
    o_ref[...] = x * lax.rsqrt(jnp.mean(x * x, axis=-1, keepdims=True) + RMS_EPS) * g_ref[...]


def _final_norm(x, g):
    bsz, seq, d = x.shape
    x2d = x.reshape(bsz * seq, d)
    tm = min(ROW_TILE, seq)
    out = pl.pallas_call(
        _final_norm_kernel,
        grid=(x2d.shape[0] // tm,),
        in_specs=[pl.BlockSpec((tm, d), lambda i: (i, 0)), pl.BlockSpec((1, d), lambda i: (0, 0))],
        out_specs=pl.BlockSpec((tm, d), lambda i: (i, 0)),
        out_shape=jax.ShapeDtypeStruct(x2d.shape, F32),
        compiler_params=_params("parallel"),
        name="final_norm",
    )(x2d, g.reshape(1, d))
    return out.reshape(bsz, seq, d)


def kernel(x, c, ada_w, ada_b, norm1_g, norm2_g, gla_w_in, gla_w_gate2, gla_b_gate, gla_norm_g, gla_w_out, kv_norm_g, kv_ada_w, kv_ada_b, kv_w, cmp_pos, cmp_w1, cmp_b1, cmp_w2, cmp_b2, nsa_w_in, nsa_b_gate, nsa_w_out, router_w, router_b, moe_w_gate_up, moe_b_gate_up, moe_w_down, moe_b_down, final_g):
    bsz, seq, d = x.shape
    mod = _ada_vectors(c, ada_w, ada_b)
    kv_mod = _ada_vectors(c, kv_ada_w[None], kv_ada_b[None])[0]
    vec = lambda m, j: m[:, None, j * d:(j + 1) * d]
    shared = None
    for layer in range(DEPTH):
        m = mod[layer]
        sh1, sc1, g1, sh2, sc2, g2 = (vec(m, j) for j in range(6))
        if layer < N_A_LAYERS:
            i = layer
            q, k, v, r, la = _gla_proj(x, norm1_g[layer], sh1, sc1, gla_w_in[i], gla_w_gate2[i], gla_b_gate[i])
            o = _gla_core(q, k, v, r, la, gla_norm_g[i])
            x = _res_matmul(o, gla_w_out[i], x, g1)
        else:
            i = layer - N_A_LAYERS
            o = _nsa_mixer(x, norm1_g[layer], sh1, sc1, shared, nsa_w_in[i], nsa_b_gate[i])
            x = _res_matmul(o, nsa_w_out[i], x, g1)
        x = _moe_layer(x, norm2_g[layer], sh2, sc2, g2, router_w[layer], router_b[layer],
                       layer, moe_w_gate_up, moe_b_gate_up[layer], moe_w_down, moe_b_down[layer])
        if layer == N_A_LAYERS - 1:
            kv_cmp, kv = _kv_proj(x, kv_norm_g, vec(kv_mod, 0), vec(kv_mod, 1), kv_w)
            shared = (_compress(kv_cmp, cmp_pos, cmp_w1, cmp_b1, cmp_w2, cmp_b2), kv)
    return _final_norm(x, final_g)
```

```python
import functools

import numpy as np
import jax
import jax.numpy as jnp
from jax import lax
from jax.experimental import pallas as pl
from jax.experimental.pallas import tpu as pltpu

F32 = jnp.float32
BF16 = jnp.bfloat16
HIGHEST = lax.Precision.HIGHEST

D_MODEL = 1024
DEPTH = 4
N_A_LAYERS = DEPTH // 2
RMS_EPS = 1e-5

GLA_HEADS = 4
GLA_KEY_DIM = D_MODEL // 2
GLA_VAL_DIM = D_MODEL
GLA_HK = GLA_KEY_DIM // GLA_HEADS
GLA_HV = GLA_VAL_DIM // GLA_HEADS
GLA_RANK = 16
GLA_TAU = 16.0
GLA_CHUNK = 64

NSA_HEADS = 16
NSA_GROUPS = 4
NSA_HPG = NSA_HEADS // NSA_GROUPS
NSA_HD = D_MODEL // NSA_HEADS
CMP_LEN = 32
CMP_STRIDE = 16
CMP_HIDDEN = 2 * NSA_HD
SLC_LEN = 64
SLC_TOPK = 16
WIN = 512
NSA_Q_BLOCK = 64

N_EXPERTS = 32
TOP_K = 4
EXPERT_FF = D_MODEL
SWIGLU_LIMIT = 7.0
SWIGLU_ALPHA = 1.702

FORCE = 1e4
NEG = -1e30

VMEM_LIMIT_BYTES = 56 * 1024 * 1024

ROW_TILE = 512
EXPERT_TILE = 256
LANES = 128
ROW_CHUNK = 8
ROUTE_SPARE_ROWS = 96
ROUTE_TILE = 512
GLA_STEP = 512
SLC_KEY_TILE = 512
NSA_STEP = 256
NSA_ROW_CHUNK = 64
RANK_UNROLL = 4
WIN_PAD = WIN
MASK_BIG = 1e30


def _params(*sem):
    return pltpu.CompilerParams(dimension_semantics=sem, vmem_limit_bytes=VMEM_LIMIT_BYTES)


def _norm_mod(x, g, shift, scale):
    y = x * lax.rsqrt(jnp.mean(x * x, axis=-1, keepdims=True) + RMS_EPS)
    return (y * g) * (1.0 + scale) + shift


def _dot(a, b):
    return jnp.dot(a, b, preferred_element_type=F32)


def _dot_nt(a, b, precision=None):
    return lax.dot_general(a, b, (((1,), (1,)), ((), ())), precision=precision,
                           preferred_element_type=F32)


def _ada_kernel(c_ref, w_ref, b_ref, o_ref):
    c = c_ref[...]
    cs = c * jax.nn.sigmoid(c)
    o_ref[0] = jnp.dot(cs, w_ref[0], precision=HIGHEST, preferred_element_type=F32) + b_ref[0]


def _ada_vectors(c, w, b):
    n_l, d, m = w.shape
    bsz = c.shape[0]
    tn = 1024
    return pl.pallas_call(
        _ada_kernel,
        grid=(n_l, m // tn),
        in_specs=[
            pl.BlockSpec((bsz, d), lambda l, j: (0, 0)),
            pl.BlockSpec((1, d, tn), lambda l, j: (l, 0, j)),
            pl.BlockSpec((1, 1, tn), lambda l, j: (l, 0, j)),
        ],
        out_specs=pl.BlockSpec((1, bsz, tn), lambda l, j: (l, 0, j)),
        out_shape=jax.ShapeDtypeStruct((n_l, bsz, m), F32),
        compiler_params=_params("parallel", "parallel"),
        name="ada_vectors",
    )(c, w, b.reshape(n_l, 1, m))


def _gla_proj_kernel(x_ref, g_ref, sh_ref, sc_ref, wq_ref, wk_ref, wv_ref, wr_ref, wlr_ref,
                     wg2_ref, bg_ref, q_ref, k_ref, v_ref, r_ref, la_ref):
    h = _norm_mod(x_ref[0], g_ref[...], sh_ref[0], sc_ref[0]).astype(BF16)
    q_ref[0] = _dot(h, wq_ref[...]).astype(BF16)
    k_ref[0] = _dot(h, wk_ref[...]).astype(BF16)
    v_ref[0] = _dot(h, wv_ref[...]).astype(BF16)
    r_ref[0] = _dot(h, wr_ref[...]).astype(BF16)
    g_lr = _dot(h, wlr_ref[...])
    z = jnp.dot(g_lr, wg2_ref[...], precision=HIGHEST, preferred_element_type=F32) + bg_ref[...]
    log_sig = jnp.minimum(z, 0.0) - jnp.log(1.0 + jnp.exp(-jnp.abs(z)))
    la_ref[0] = log_sig / GLA_TAU


def _gla_proj(x, g, shift, scale, w_in, w_gate2, b_gate):
    bsz, seq, d = x.shape
    tm = min(ROW_TILE, seq)
    kd, vd = GLA_KEY_DIM, GLA_VAL_DIM
    wq = w_in[:, :kd].astype(BF16)
    wk = w_in[:, kd:2 * kd].astype(BF16)
    wv = w_in[:, 2 * kd:2 * kd + vd].astype(BF16)
    wr = w_in[:, 2 * kd + vd:2 * kd + 2 * vd].astype(BF16)
    wlr = w_in[:, 2 * kd + 2 * vd:].astype(BF16)
    full = lambda shape: pl.BlockSpec(shape, lambda b, i: (0,) * len(shape))
    row = lambda n: pl.BlockSpec((1, tm, n), lambda b, i: (b, i, 0))
    vec = pl.BlockSpec((1, 1, d), lambda b, i: (b, 0, 0))
    return pl.pallas_call(
        _gla_proj_kernel,
        grid=(bsz, seq // tm),
        in_specs=[row(d), full((1, d)), vec, vec, full((d, kd)), full((d, kd)), full((d, vd)),
                  full((d, vd)), full((d, GLA_RANK)), full((GLA_RANK, kd)), full((1, kd))],
        out_specs=[row(kd), row(kd), row(vd), row(vd), row(kd)],
        out_shape=[jax.ShapeDtypeStruct((bsz, seq, kd), BF16),
                   jax.ShapeDtypeStruct((bsz, seq, kd), BF16),
                   jax.ShapeDtypeStruct((bsz, seq, vd), BF16),
                   jax.ShapeDtypeStruct((bsz, seq, vd), BF16),
                   jax.ShapeDtypeStruct((bsz, seq, kd), F32)],
        compiler_params=_params("parallel", "parallel"),
        name="gla_proj",
    )(x, g.reshape(1, d), shift, scale, wq, wk, wv, wr, wlr, w_gate2, b_gate.reshape(1, kd))


def _gla_core_kernel(q_ref, k_ref, v_ref, r_ref, la_ref, ng_ref, o_ref, state_ref, *, n_chunks):
    @pl.when(pl.program_id(1) == 0)
    def _():
        state_ref[...] = jnp.zeros_like(state_ref)

    c_len = GLA_CHUNK
    row = lax.broadcasted_iota(jnp.int32, (c_len, c_len), 0)
    col = lax.broadcasted_iota(jnp.int32, (c_len, c_len), 1)
    causal = col <= row
    row_k = lax.broadcasted_iota(jnp.int32, (c_len, GLA_HK), 0)
    shifts = [1 << j for j in range(c_len.bit_length() - 1)]

    def chunk(c, carry):
        c0 = pl.multiple_of(c * c_len, c_len)
        for hh in range(GLA_HEADS):
            kc = slice(hh * GLA_HK, (hh + 1) * GLA_HK)
            vc = slice(hh * GLA_HV, (hh + 1) * GLA_HV)
            b = la_ref[0, pl.ds(c0, c_len), kc]
            for shift in shifts:
                b = b + jnp.where(row_k >= shift, pltpu.roll(b, shift, 0), 0.0)
            q = q_ref[0, pl.ds(c0, c_len), kc].astype(F32) * (GLA_HK ** -0.5)
            k = k_ref[0, pl.ds(c0, c_len), kc].astype(F32)
            v = v_ref[0, pl.ds(c0, c_len), vc]
            q_dec = (q * jnp.exp(b)).astype(BF16)
            k_intra = (k * jnp.exp(-b)).astype(BF16)
            b_t = b.T
            bl_t = b_t[:, c_len - 1:c_len]
            k_inter_t = (k.T * jnp.exp(bl_t - b_t)).astype(BF16)
            att = jnp.where(causal, _dot_nt(q_dec, k_intra), 0.0).astype(BF16)
            state = state_ref[hh]
            o = _dot(att, v) + _dot(q_dec, state.astype(BF16))
            state_ref[hh] = state * jnp.exp(bl_t) + _dot(k_inter_t, v)
            o = o * lax.rsqrt(jnp.mean(o * o, axis=-1, keepdims=True) + RMS_EPS)
            r = r_ref[0, pl.ds(c0, c_len), vc].astype(F32)
            o_ref[0, pl.ds(c0, c_len), vc] = ((o * ng_ref[:, vc]) * (r * jax.nn.sigmoid(r))).astype(BF16)
        return carry

    lax.fori_loop(0, n_chunks, chunk, 0)


def _gla_core(q, k, v, r, la, norm_g):
    bsz, seq, _ = q.shape
    ts = min(GLA_STEP, seq)
    kern = functools.partial(_gla_core_kernel, n_chunks=ts // GLA_CHUNK)
    kd = lambda: pl.BlockSpec((1, ts, GLA_KEY_DIM), lambda b, s: (b, s, 0))
    vd = lambda: pl.BlockSpec((1, ts, GLA_VAL_DIM), lambda b, s: (b, s, 0))
    return pl.pallas_call(
        kern,
        grid=(bsz, seq // ts),
        in_specs=[kd(), kd(), vd(), vd(), kd(), pl.BlockSpec((1, GLA_VAL_DIM), lambda b, s: (0, 0))],
        out_specs=vd(),
        out_shape=jax.ShapeDtypeStruct((bsz, seq, GLA_VAL_DIM), BF16),
        scratch_shapes=[pltpu.VMEM((GLA_HEADS, GLA_HK, GLA_HV), F32)],
        compiler_params=_params("parallel", "arbitrary"),
        name="gla_core",
    )(q, k, v, r, la, norm_g.reshape(1, GLA_VAL_DIM))


def _res_matmul_kernel(a_ref, w_ref, x_ref, gate_ref, o_ref):
    o_ref[0] = x_ref[0] + gate_ref[0] * _dot(a_ref[0], w_ref[...])


def _res_matmul(a, w, x, gate):
    bsz, seq, d = x.shape
    kdim = a.shape[-1]
    tm = min(ROW_TILE, seq)
    return pl.pallas_call(
        _res_matmul_kernel,
        grid=(bsz, seq // tm),
        in_specs=[pl.BlockSpec((1, tm, kdim), lambda b, i: (b, i, 0)),
                  pl.BlockSpec((kdim, d), lambda b, i: (0, 0)),
                  pl.BlockSpec((1, tm, d), lambda b, i: (b, i, 0)),
                  pl.BlockSpec((1, 1, d), lambda b, i: (b, 0, 0))],
        out_specs=pl.BlockSpec((1, tm, d), lambda b, i: (b, i, 0)),
        out_shape=jax.ShapeDtypeStruct((bsz, seq, d), F32),
        compiler_params=_params("parallel", "parallel"),
        name="res_matmul",
    )(a, w.astype(BF16), x, gate)


def _route_kernel(x_ref, g_ref, sh_ref, sc_ref, rwt_ref, rb_ref, xs_ref, w_ref, slot_ref, cnt_ref):
    tm = x_ref.shape[0]
    n_slot = xs_ref.shape[0]
    h = _norm_mod(x_ref[...], g_ref[...], sh_ref[0], sc_ref[0])
    logits = _dot_nt(rwt_ref[...], h, precision=HIGHEST) + rb_ref[...]
    e_iota = lax.broadcasted_iota(jnp.int32, logits.shape, 0)
    vals, hots = [], []
    for _ in range(TOP_K):
        m = jnp.max(logits, axis=0, keepdims=True)
        idx = jnp.min(jnp.where(logits == m, e_iota, N_EXPERTS), axis=0, keepdims=True)
        hot = e_iota == idx
        vals.append(m)
        hots.append(hot)
        logits = jnp.where(hot, -jnp.inf, logits)
    exps = [jnp.exp(v - vals[0]) for v in vals]
    denom = exps[0] + exps[1] + exps[2] + exps[3]
    for kk in range(TOP_K):
        w_ref[kk:kk + 1, :] = exps[kk] / denom

    sel = (hots[0] | hots[1] | hots[2] | hots[3]).astype(F32)
    s_iota = lax.broadcasted_iota(jnp.int32, (tm, tm), 0)
    t_iota = lax.broadcasted_iota(jnp.int32, (tm, tm), 1)
    before = (s_iota < t_iota).astype(BF16)
    prefix = _dot(sel.astype(BF16), before)
    cnt = jnp.sum(sel, axis=1, keepdims=True)
    chunks = jnp.floor((cnt + (ROW_CHUNK - 1.0)) * (1.0 / ROW_CHUNK))
    lower = (lax.broadcasted_iota(jnp.int32, (N_EXPERTS, N_EXPERTS), 0) >
             lax.broadcasted_iota(jnp.int32, (N_EXPERTS, N_EXPERTS), 1)).astype(BF16)
    seg0 = _dot(lower, jnp.broadcast_to(chunks, (N_EXPERTS, LANES)).astype(BF16))[:, 0:1] * ROW_CHUNK
    slots = [jnp.sum(jnp.where(hot, prefix + seg0, 0.0), axis=0, keepdims=True).astype(jnp.int32)
             for hot in hots]
    r_iota = lax.broadcasted_iota(jnp.int32, (n_slot, tm), 0)
    q = jnp.zeros((n_slot, tm), F32)
    for kk in range(TOP_K):
        q = q + jnp.where(r_iota == slots[kk], 1.0, 0.0)
    for kk in range(TOP_K):
        slot_ref[kk:kk + 1, :] = slots[kk]
    xs_ref[...] = _dot((q > 0.0).astype(BF16), h.astype(BF16))

    cnt_ref[0] = jnp.broadcast_to(cnt, (N_EXPERTS, LANES)).astype(jnp.int32)


def _route(x2d, g, shift, scale, router_w, router_b, seq):
    n_tok, d = x2d.shape
    tm = min(ROUTE_TILE, seq)
    per_b = seq // tm
    n_win = n_tok // tm
    n_slot = tm * TOP_K + N_EXPERTS * ROW_CHUNK + ROUTE_SPARE_ROWS
    vec = pl.BlockSpec((1, 1, d), lambda i: (i // per_b, 0, 0))
    return pl.pallas_call(
        _route_kernel,
        grid=(n_win,),
        in_specs=[pl.BlockSpec((tm, d), lambda i: (i, 0)),
                  pl.BlockSpec((1, d), lambda i: (0, 0)), vec, vec,
                  pl.BlockSpec((N_EXPERTS, d), lambda i: (0, 0)),
                  pl.BlockSpec((N_EXPERTS, 1), lambda i: (0, 0))],
        out_specs=[pl.BlockSpec((n_slot, d), lambda i: (i, 0)),
                   pl.BlockSpec((TOP_K, tm), lambda i: (0, i)),
                   pl.BlockSpec((TOP_K, tm), lambda i: (0, i)),
                   pl.BlockSpec((1, N_EXPERTS, LANES), lambda i: (i, 0, 0))],
        out_shape=[jax.ShapeDtypeStruct((n_win * n_slot, d), F32),
                   jax.ShapeDtypeStruct((TOP_K, n_tok), F32),
                   jax.ShapeDtypeStruct((TOP_K, n_tok), jnp.int32),
                   jax.ShapeDtypeStruct((n_win, N_EXPERTS, LANES), jnp.int32)],
        compiler_params=_params("parallel"),
        name="moe_route",
    )(x2d, g.reshape(1, d), shift, scale, router_w.T, router_b.reshape(N_EXPERTS, 1))


def _expert_kernel(te_ref, nu_ref, src_hbm, dst_hbm, xs_in, wgu_ref, wd_ref, bg_ref, bu_ref, bd_ref, xy_hbm,
                   wg_s, wu_s, wd_s, tr_s, xbuf, ybuf, tok_s, out_s, tok_sem, out_sem, g_sem, s_sem):
    del xs_in
    i = pl.program_id(0)
    n_tiles = pl.num_programs(0)
    tm = xbuf.shape[1]
    used = i < nu_ref[0]
    fresh = (i == 0) | (te_ref[i] != te_ref[jnp.maximum(i - 1, 0)])
    slot = i % 2
    other = 1 - slot

    def tok_copy(t, sl):
        return pltpu.make_async_copy(src_hbm.at[t], tok_s.at[sl], tok_sem.at[sl])

    def out_copy(t, sl):
        return pltpu.make_async_copy(dst_hbm.at[t], out_s.at[sl], out_sem.at[sl])

    def gather_start(sl):
        for c in range(tm // ROW_CHUNK):
            src = pl.multiple_of(tok_s[sl, c] * ROW_CHUNK, ROW_CHUNK)
            pltpu.make_async_copy(xy_hbm.at[pl.ds(src, ROW_CHUNK), :],
                                  xbuf.at[sl, pl.ds(c * ROW_CHUNK, ROW_CHUNK), :], g_sem.at[sl]).start()

    def gather_wait(sl):
        pltpu.make_async_copy(xy_hbm.at[pl.ds(0, tm), :], xbuf.at[sl], g_sem.at[sl]).wait()

    def scatter_start(sl):
        for c in range(tm // ROW_CHUNK):
            dst = pl.multiple_of(out_s[sl, c] * ROW_CHUNK, ROW_CHUNK)
            pltpu.make_async_copy(ybuf.at[sl, pl.ds(c * ROW_CHUNK, ROW_CHUNK), :],
                                  xy_hbm.at[pl.ds(dst, ROW_CHUNK), :], s_sem.at[sl]).start()

    def scatter_wait(sl):
        pltpu.make_async_copy(ybuf.at[sl], xy_hbm.at[pl.ds(0, tm), :], s_sem.at[sl]).wait()

    @pl.when(i == 0)
    def _():
        tok_copy(0, 0).start()
        out_copy(0, 0).start()
        tok_copy(0, 0).wait()
        gather_start(0)
        tok_copy(jnp.minimum(1, n_tiles - 1), 1).start()

    @pl.when(used & fresh)
    def _():
        n_slab, chunk, lanes = tr_s.shape
        half = chunk // 2
        for c in range(wgu_ref.shape[2] // chunk):
            t = wgu_ref[0, :, c * chunk:(c + 1) * chunk].T
            for j in range(n_slab):
                tr_s[j] = t[:, j * lanes:(j + 1) * lanes]
            for j in range(n_slab):
                wg_s[c * half:(c + 1) * half, j * lanes:(j + 1) * lanes] = (
                    tr_s[j, pl.ds(0, half, stride=2), :].astype(BF16))
                wu_s[c * half:(c + 1) * half, j * lanes:(j + 1) * lanes] = (
                    tr_s[j, pl.ds(1, half, stride=2), :].astype(BF16))
        wd_s[...] = wd_ref[0].astype(BF16)

    @pl.when(used)
    def _():
        nxt = jnp.minimum(i + 1, n_tiles - 1)
        tok_copy(nxt, other).wait()
        gather_start(other)
        tok_copy(jnp.minimum(i + 2, n_tiles - 1), slot).start()
        gather_wait(slot)
        xb = xbuf[slot].astype(BF16)
        gate = jnp.minimum(_dot_nt(xb, wg_s[...]) + bg_ref[0], SWIGLU_LIMIT)
        up = jnp.clip(_dot_nt(xb, wu_s[...]) + bu_ref[0], -SWIGLU_LIMIT, SWIGLU_LIMIT)
        act = (up + 1.0) * gate * jax.nn.sigmoid(SWIGLU_ALPHA * gate)
        ybuf[slot] = _dot(act.astype(BF16), wd_s[...]) + bd_ref[0]
        out_copy(i, slot).wait()
        scatter_start(slot)
        out_copy(nxt, other).start()

    @pl.when(used & (i > 0))
    def _():
        scatter_wait(other)

    @pl.when(i == nu_ref[0] - 1)
    def _():
        gather_wait(other)
        scatter_wait(slot)
        tok_copy(0, slot).wait()
        out_copy(0, other).wait()


def _experts(xs, src_tiles, dst_tiles, tile_expert, n_used, layer, w_gu, w_down, bg, bu, bd):
    d = xs.shape[1]
    n_tiles = src_tiles.shape[0]
    tm = EXPERT_TILE
    ff = w_down.shape[2]
    wsel = lambda i, te, nu: (te[i], 0, 0)
    lsel = lambda i, te, nu: (layer, te[i], 0, 0)
    hbm = pl.BlockSpec(memory_space=pl.ANY)
    return pl.pallas_call(
        _expert_kernel,
        grid_spec=pltpu.PrefetchScalarGridSpec(
            num_scalar_prefetch=2,
            grid=(n_tiles,),
            in_specs=[hbm, hbm, hbm,
                      pl.BlockSpec((None, 1, d, 2 * ff), lsel),
                      pl.BlockSpec((None, 1, ff, d), lsel),
                      pl.BlockSpec((1, 1, ff), wsel), pl.BlockSpec((1, 1, ff), wsel),
                      pl.BlockSpec((1, 1, d), wsel)],
            out_specs=hbm,
            scratch_shapes=[pltpu.VMEM((ff, d), BF16), pltpu.VMEM((ff, d), BF16),
                            pltpu.VMEM((ff, d), BF16), pltpu.VMEM((d // 128, 256, 128), F32),
                            pltpu.VMEM((2, tm, d), F32), pltpu.VMEM((2, tm, d), F32),
                            pltpu.SMEM((2, tm // ROW_CHUNK), jnp.int32),
                            pltpu.SMEM((2, tm // ROW_CHUNK), jnp.int32),
                            pltpu.SemaphoreType.DMA((2,)), pltpu.SemaphoreType.DMA((2,)),
                            pltpu.SemaphoreType.DMA((2,)), pltpu.SemaphoreType.DMA((2,))]),
        out_shape=jax.ShapeDtypeStruct(xs.shape, xs.dtype),
        input_output_aliases={4: 0},
        compiler_params=_params("arbitrary"),
        name="moe_experts",
    )(tile_expert, n_used, src_tiles, dst_tiles, xs, w_gu, w_down, bg, bu, bd)


def _combine_kernel(y_ref, slot_ref, w_ref, x_ref, gate_ref, o_ref):
    tm = x_ref.shape[0]
    n_slot = y_ref.shape[0]
    r_iota = lax.broadcasted_iota(jnp.int32, (tm, n_slot), 1)
    w = w_ref[...]
    slot = slot_ref[...]
    c = jnp.zeros((tm, n_slot), F32)
    for kk in range(TOP_K):
        c = c + jnp.where(r_iota == slot[:, kk:kk + 1], w[:, kk:kk + 1], 0.0)
    y = _dot(c.astype(BF16), y_ref[...].astype(BF16))
    o_ref[...] = x_ref[...] + gate_ref[0] * y


def _combine(y, slots, w_tok, x2d, gate, seq):
    n_tok, d = x2d.shape
    tm = min(ROUTE_TILE, seq)
    per_b = seq // tm
    n_win = n_tok // tm
    n_slot = y.shape[0] // n_win
    return pl.pallas_call(
        _combine_kernel,
        grid=(n_win,),
        in_specs=[pl.BlockSpec((n_slot, d), lambda i: (i, 0)),
                  pl.BlockSpec((tm, TOP_K), lambda i: (i, 0)),
                  pl.BlockSpec((tm, TOP_K), lambda i: (i, 0)),
                  pl.BlockSpec((tm, d), lambda i: (i, 0)),
                  pl.BlockSpec((1, 1, d), lambda i: (i // per_b, 0, 0))],
        out_specs=pl.BlockSpec((tm, d), lambda i: (i, 0)),
        out_shape=jax.ShapeDtypeStruct((n_tok, d), F32),
        compiler_params=_params("parallel"),
        name="moe_combine",
    )(y, slots, w_tok, x2d, gate)


def _moe_layer(x, norm_g, shift, scale, gate, router_w, router_b, layer, w_gu, b_gu, w_down, b_down):
    bsz, seq, d = x.shape
    n_tok = bsz * seq
    x2d = x.reshape(n_tok, d)
    xs, w_t, slot_t, cnt = _route(x2d, norm_g, shift, scale, router_w, router_b, seq)

    cnt = cnt[:, :, 0]
    n_win = cnt.shape[0]
    win_chunks = xs.shape[0] // n_win // ROW_CHUNK
    tile_chunks = EXPERT_TILE // ROW_CHUNK
    c8 = (cnt + ROW_CHUNK - 1) // ROW_CHUNK
    seg_off = jnp.cumsum(c8, axis=1) - c8
    cum_w = jnp.cumsum(c8, axis=0)
    total = cum_w[-1]
    padded = (total + tile_chunks - 1) // tile_chunks * tile_chunks
    ends = jnp.cumsum(padded)
    starts = ends - padded
    n_chunk = n_tok * TOP_K // ROW_CHUNK + n_win * N_EXPERTS + N_EXPERTS * tile_chunks
    n_tiles = n_chunk // tile_chunks
    tile0 = jnp.arange(n_tiles, dtype=jnp.int32) * tile_chunks
    tile_expert = jnp.minimum(jnp.sum((ends[None, :] <= tile0[:, None]).astype(jnp.int32), axis=1),
                              N_EXPERTS - 1)
    e_hot = (tile_expert[:, None] == jnp.arange(N_EXPERTS, dtype=jnp.int32)[None, :]).astype(jnp.int32)
    pick = lambda tab: jnp.sum(e_hot[:, :, None] * tab.T[None, :, :], axis=1)
    cum_t, c8_t, off_t = pick(cum_w)[:, None, :], pick(c8)[:, None, :], pick(seg_off)[:, None, :]
    lane = jnp.arange(tile_chunks, dtype=jnp.int32)[None, :]
    rank = (tile0 - jnp.sum(e_hot * starts[None, :], axis=1))[:, None] + lane
    valid = rank < jnp.sum(e_hot * total[None, :], axis=1)[:, None]
    w_of = jnp.minimum(jnp.sum((cum_t <= rank[:, :, None]).astype(jnp.int32), axis=2), n_win - 1)
    w_hot = (w_of[:, :, None] == jnp.arange(n_win, dtype=jnp.int32)[None, None, :]).astype(jnp.int32)
    seg_first = jnp.sum(w_hot * (cum_t - c8_t), axis=2)
    src = w_of * win_chunks + jnp.sum(w_hot * off_t, axis=2) + rank - seg_first
    spare0 = win_chunks - ROUTE_SPARE_ROWS // ROW_CHUNK
    u = (jnp.arange(n_tiles, dtype=jnp.int32) % 2)[:, None] * tile_chunks + lane
    dst = jnp.where(valid, src, (u % n_win) * win_chunks + spare0 + u // n_win)
    src = jnp.where(valid, src, win_chunks - 1)
    n_used = (ends[-1:] // tile_chunks).astype(jnp.int32)

    y = _experts(xs, src.reshape(n_tiles, tile_chunks), dst.reshape(n_tiles, tile_chunks), tile_expert,
                 n_used, layer, w_gu, w_down, b_gu[:, None, 0::2], b_gu[:, None, 1::2], b_down[:, None, :])
    out = _combine(y, slot_t.T, w_t.T, x2d, gate, seq)
    return out.reshape(bsz, seq, d)


def _kv_proj_kernel(x_ref, g_ref, sh_ref, sc_ref, w_ref, cmp_ref, kv_ref):
    h = _norm_mod(x_ref[0], g_ref[...], sh_ref[0], sc_ref[0]).astype(BF16)
    kv = _dot(h, w_ref[...])
    for t in range(6):
        for g in range(NSA_GROUPS):
            c0 = (t * NSA_GROUPS + g) * NSA_HD
            piece = kv[:, c0:c0 + NSA_HD]
            if t < 2:
                cmp_ref[t, 0, g] = piece
            else:
                kv_ref[t - 2, 0, g] = piece.astype(BF16)


def _kv_proj(x, g, shift, scale, kv_w):
    bsz, seq, d = x.shape
    tm = min(ROW_TILE, seq)
    n_out = kv_w.shape[1]
    vec = pl.BlockSpec((1, 1, d), lambda b, i: (b, 0, 0))
    return pl.pallas_call(
        _kv_proj_kernel,
        grid=(bsz, seq // tm),
        in_specs=[pl.BlockSpec((1, tm, d), lambda b, i: (b, i, 0)),
                  pl.BlockSpec((1, d), lambda b, i: (0, 0)), vec, vec,
                  pl.BlockSpec((d, n_out), lambda b, i: (0, 0))],
        out_specs=[pl.BlockSpec((2, 1, NSA_GROUPS, tm, NSA_HD), lambda b, i: (0, b, 0, i, 0)),
                   pl.BlockSpec((4, 1, NSA_GROUPS, tm, NSA_HD), lambda b, i: (0, b, 0, i, 0))],
        out_shape=[jax.ShapeDtypeStruct((2, bsz, NSA_GROUPS, seq, NSA_HD), F32),
                   jax.ShapeDtypeStruct((4, bsz, NSA_GROUPS, seq, NSA_HD), BF16)],
        compiler_params=_params("parallel", "parallel"),
        name="nsa_kv_proj",
    )(x, g.reshape(1, d), shift, scale, kv_w.astype(BF16))


def _compress_kernel(x_ref, pos_ref, w1_ref, b1_ref, w2_ref, b2_ref, o_ref, *, n_rows):
    for g in range(NSA_GROUPS):
        first = jnp.zeros((n_rows, CMP_HIDDEN), F32)
        second = jnp.zeros((n_rows, CMP_HIDDEN), F32)
        for l in range(CMP_STRIDE):
            rows = x_ref[0, 0, g, pl.ds(l, n_rows, stride=CMP_STRIDE), :]
            first = first + _dot((rows + pos_ref[0, l:l + 1, :]).astype(BF16), w1_ref[0, l])
            second = second + _dot((rows + pos_ref[0, CMP_STRIDE + l:CMP_STRIDE + l + 1, :]).astype(BF16),
                                   w1_ref[0, CMP_STRIDE + l])
        pre = first + pltpu.roll(second, n_rows - 1, 0) + b1_ref[0]
        hid = 0.5 * pre * (1.0 + jnp.tanh(0.7978845608028654 * (pre + 0.044715 * pre * pre * pre)))
        o_ref[0, 0, g] = (_dot(hid.astype(BF16), w2_ref[0]) + b2_ref[0]).astype(BF16)


def _compress(kv_cmp, cmp_pos, cmp_w1, cmp_b1, cmp_w2, cmp_b2):
    _, bsz, _, seq, _ = kv_cmp.shape
    n_rows = seq // CMP_STRIDE
    kern = functools.partial(_compress_kernel, n_rows=n_rows)
    w1 = cmp_w1.reshape(2, CMP_LEN, NSA_HD, CMP_HIDDEN).astype(BF16)
    return pl.pallas_call(
        kern,
        grid=(2, bsz),
        in_specs=[pl.BlockSpec((1, 1, NSA_GROUPS, seq, NSA_HD), lambda t, b: (t, b, 0, 0, 0)),
                  pl.BlockSpec((1, CMP_LEN, NSA_HD), lambda t, b: (t, 0, 0)),
                  pl.BlockSpec((1, CMP_LEN, NSA_HD, CMP_HIDDEN), lambda t, b: (t, 0, 0, 0)),
                  pl.BlockSpec((1, 1, CMP_HIDDEN), lambda t, b: (t, 0, 0)),
                  pl.BlockSpec((1, CMP_HIDDEN, NSA_HD), lambda t, b: (t, 0, 0)),
                  pl.BlockSpec((1, 1, NSA_HD), lambda t, b: (t, 0, 0))],
        out_specs=pl.BlockSpec((1, 1, NSA_GROUPS, n_rows, NSA_HD), lambda t, b: (t, b, 0, 0, 0)),
        out_shape=jax.ShapeDtypeStruct((2, bsz, NSA_GROUPS, n_rows, NSA_HD), BF16),
        compiler_params=_params("parallel", "parallel"),
        name="nsa_compress",
    )(kv_cmp, cmp_pos, w1, cmp_b1[:, None, :], cmp_w2.astype(BF16), cmp_b2[:, None, :])


def _nsa_proj_kernel(x_ref, g_ref, sh_ref, sc_ref, wq_ref, wg_ref, bg_ref, q_ref, gate_ref):
    h = _norm_mod(x_ref[0], g_ref[...], sh_ref[0], sc_ref[0]).astype(BF16)
    q_ref[0] = (_dot(h, wq_ref[...]) * (NSA_HD ** -0.5)).astype(BF16)
    gate_ref[0] = jax.nn.sigmoid(_dot(h, wg_ref[...]) + bg_ref[...])


def _nsa_proj(x, g, shift, scale, w_in, b_gate):
    bsz, seq, d = x.shape
    tm = min(ROW_TILE, seq)
    nq = NSA_HEADS * NSA_HD
    ng = 3 * NSA_HEADS
    vec = pl.BlockSpec((1, 1, d), lambda b, i: (b, 0, 0))
    return pl.pallas_call(
        _nsa_proj_kernel,
        grid=(bsz, seq // tm),
        in_specs=[pl.BlockSpec((1, tm, d), lambda b, i: (b, i, 0)),
                  pl.BlockSpec((1, d), lambda b, i: (0, 0)), vec, vec,
                  pl.BlockSpec((d, nq), lambda b, i: (0, 0)),
                  pl.BlockSpec((d, ng), lambda b, i: (0, 0)),
                  pl.BlockSpec((1, ng), lambda b, i: (0, 0))],
        out_specs=[pl.BlockSpec((1, tm, nq), lambda b, i: (b, i, 0)),
                   pl.BlockSpec((1, tm, ng), lambda b, i: (b, i, 0))],
        out_shape=[jax.ShapeDtypeStruct((bsz, seq, nq), BF16),
                   jax.ShapeDtypeStruct((bsz, seq, ng), F32)],
        compiler_params=_params("parallel", "parallel"),
        name="nsa_proj",
    )(x, g.reshape(1, d), shift, scale, w_in[:, :nq].astype(BF16), w_in[:, nq:].astype(BF16),
      b_gate.reshape(1, ng))


def _nsa_attn_kernel(q_ref, gate_ref, qc_ref, slope_ref, ovl_ref, kconst_ref, cw_ref, kc_ref, vc_ref,
                     ks_ref, vs_ref, kw_ref, vw_ref, o_ref, ksa, vsa, kwa, vwa, score_s,
                     sc_s, pc_s, psum_s, qa_s, m_s, acc_s, sa_s, sb_s, ps_s, sw_s, pw_s, ow_s):
    qi = pl.program_id(2)
    qs = NSA_STEP
    hd = NSA_HD
    rows = NSA_HPG * qs
    seq = ks_ref.shape[2]
    q0 = pl.multiple_of(qi * qs, qs)

    @pl.when(qi == 0)
    def _():
        ksa[...] = kconst_ref[...]
        ksa[:, 0:hd] = ks_ref[0, 0]
        one_col = (lax.broadcasted_iota(jnp.int32, (seq, 128), 1) == hd).astype(BF16)
        vsa[...] = one_col
        vsa[:, 0:hd] = vs_ref[0, 0]
        flag_col = (lax.broadcasted_iota(jnp.int32, (WIN_PAD, 128), 1) == hd).astype(BF16)
        kwa[0:WIN_PAD, :] = flag_col
        kwa[WIN_PAD:, :] = jnp.zeros((seq, 128), BF16)
        kwa[WIN_PAD:, 0:hd] = kw_ref[0, 0]
        vwa[0:WIN_PAD, :] = jnp.zeros((WIN_PAD, 128), BF16)
        vwa[WIN_PAD:, :] = one_col
        vwa[WIN_PAD:, 0:hd] = vw_ref[0, 0]

    qt = q_ref[0].astype(F32)
    q32 = jnp.concatenate([qt[:, hh * hd:(hh + 1) * hd] for hh in range(NSA_HPG)], axis=0)
    q = q32.astype(BF16)
    slope = slope_ref[0]
    t_row = q0 + lax.broadcasted_iota(jnp.int32, (rows, 1), 0) % qs

    rc = NSA_ROW_CHUNK

    nw = WIN_PAD + qs
    q_w = jnp.concatenate([q32, jnp.full((rows, hd), -MASK_BIG, F32)], axis=1).astype(BF16)
    sw_s[...] = _dot_nt(q_w, kwa[pl.ds(q0, nw), :])
    for c in range(rows // rc):
        r = slice(c * rc, (c + 1) * rc)
        s_w = sw_s[r, :] + cw_ref[0, r, :]
        m_w = jnp.max(s_w, axis=-1, keepdims=True)
        pw_s[r, :] = jnp.exp(s_w - m_w).astype(BF16)
    acc_w = _dot(pw_s[...], vwa[pl.ds(q0, nw), :])
    ow_s[...] = acc_w[:, 0:hd] * (1.0 / acc_w[:, hd:hd + 1])

    n_cmp = kc_ref.shape[2]
    cmp_end = lax.broadcasted_iota(jnp.int32, (1, n_cmp), 1) * CMP_STRIDE + (CMP_LEN - 1)
    cmp_bias = cmp_end.astype(F32)
    sc_s[...] = _dot_nt(q, kc_ref[0, 0])
    for c in range(rows // rc):
        r = slice(c * rc, (c + 1) * rc)
        ok_c = cmp_end <= t_row[r]
        s_c = jnp.where(ok_c, sc_s[r, :] + slope[r] * cmp_bias, NEG)
        m_c = jnp.max(s_c, axis=-1, keepdims=True)
        e_c = jnp.where(ok_c, jnp.exp(s_c - m_c), 0.0)
        l_c = jnp.sum(e_c, axis=-1, keepdims=True)
        p_c = e_c * (1.0 / jnp.where(l_c > 0.0, l_c, 1.0))
        pc_s[r, :] = p_c.astype(BF16)
        qr = slice((c * rc) % qs, (c * rc) % qs + rc)
        if c * rc < qs:
            psum_s[qr, :] = p_c
        else:
            psum_s[qr, :] = psum_s[qr, :] + p_c
    o_c = _dot(pc_s[...], vc_ref[0, 0])

    imp_t = _dot_nt(ovl_ref[...], psum_s[...], precision=HIGHEST)
    n_slc = imp_t.shape[0]
    blk = lax.broadcasted_iota(jnp.int32, (n_slc, qs), 0)
    cur = (q0 + lax.broadcasted_iota(jnp.int32, (1, qs), 1)) // SLC_LEN
    forced = (blk == 0) | (blk == cur) | (blk == cur - 1)
    score = jnp.where(blk <= cur, imp_t + jnp.where(forced, FORCE, 0.0), -FORCE)
    score_s[...] = score

    def rank_step(ip, rank):
        for u in range(RANK_UNROLL):
            i = RANK_UNROLL * ip + u
            other = score_s[pl.ds(i, 1), :]
            ahead = (other > score) | ((other == score) & (blk > i))
            rank = rank + ahead.astype(F32)
        return rank

    last_blk = (q0 + qs - 1) // SLC_LEN
    n_rank = jnp.where(last_blk >= SLC_TOPK, last_blk // RANK_UNROLL + 1, 0)
    rank = lax.fori_loop(0, n_rank, rank_step, jnp.zeros((n_slc, qs), F32))
    sel_t = (rank < float(min(SLC_TOPK, n_slc))).astype(BF16)
    eye = (lax.broadcasted_iota(jnp.int32, (qs, qs), 0) ==
           lax.broadcasted_iota(jnp.int32, (qs, qs), 1)).astype(BF16)
    sel = _dot_nt(eye, sel_t)
    drop = (sel - 1.0) * MASK_BIG
    if n_slc < hd:
        drop = jnp.concatenate([drop, jnp.zeros((qs, hd - n_slc), F32)], axis=1)
    q_aug = jnp.concatenate([q32, jnp.concatenate([drop] * NSA_HPG, axis=0), qc_ref[0]],
                            axis=1).astype(BF16)

    tk = SLC_KEY_TILE

    n_full = q0 // tk
    qa_s[...] = q_aug
    m_s[...] = jnp.full((rows, 1), -MASK_BIG, F32)
    acc_s[...] = jnp.zeros((rows, 128), F32)

    def scores(kt, s_out):
        k0 = pl.multiple_of(kt * tk, tk)
        s_out[...] = _dot_nt(qa_s[...], ksa[pl.ds(k0, tk), :])

    def absorb(s_in, kt, causal):
        k0 = pl.multiple_of(kt * tk, tk)
        kpos = k0 + lax.broadcasted_iota(jnp.int32, (1, tk), 1)
        for c in range(rows // rc):
            r = slice(c * rc, (c + 1) * rc)
            s = s_in[r, :]
            if causal:
                s = jnp.where(kpos <= t_row[r], s, -MASK_BIG)
            m_old = m_s[r, :]
            m_new = jnp.maximum(m_old, jnp.max(s, axis=-1, keepdims=True))
            m_s[r, :] = m_new
            ps_s[r, :] = jnp.exp(s - m_new).astype(BF16)
            acc_s[r, :] = jnp.exp(m_old - m_new) * acc_s[r, :]
        acc_s[...] = acc_s[...] + _dot(ps_s[...], vsa[pl.ds(k0, tk), :])

    scores(0, sa_s)

    def slc_pair(j, carry):
        scores(2 * j + 1, sb_s)
        absorb(sa_s, 2 * j, False)

        @pl.when(2 * j + 1 < n_full)
        def _():
            scores(2 * j + 2, sa_s)
            absorb(sb_s, 2 * j + 1, False)
        return carry

    lax.fori_loop(0, (n_full + 1) // 2, slc_pair, 0)

    @pl.when(n_full % 2 == 0)
    def _():
        absorb(sa_s, n_full, True)

    @pl.when(n_full % 2 == 1)
    def _():
        absorb(sb_s, n_full, True)

    o_s = acc_s[:, 0:hd] * (1.0 / acc_s[:, hd:hd + 1])

    gates = gate_ref[0, 0]
    outs = []
    for hh in range(NSA_HPG):
        r0 = hh * qs
        outs.append(gates[:, hh:hh + 1] * o_c[r0:r0 + qs]
                    + gates[:, NSA_HPG + hh:NSA_HPG + hh + 1] * o_s[r0:r0 + qs]
                    + gates[:, 2 * NSA_HPG + hh:2 * NSA_HPG + hh + 1] * ow_s[r0:r0 + qs, :])
    o_ref[0] = jnp.concatenate(outs, axis=1).astype(BF16)


def _bf16_pieces(x):
    x = np.asarray(x, np.float32)
    out = []
    for _ in range(3):
        p = x.astype(BF16).astype(np.float32)
        out.append(p)
        x = x - p
    return out


def _nsa_constants(seq):
    qb, hd = NSA_STEP, NSA_HD
    n_cmp_rows = seq // CMP_STRIDE
    n_slc = seq // SLC_LEN
    cmp_start = np.arange(n_cmp_rows) * CMP_STRIDE
    cmp_end = cmp_start + CMP_LEN - 1
    slc_start = np.arange(n_slc) * SLC_LEN
    overlap = ((cmp_start[:, None] <= slc_start[None, :] + SLC_LEN - 1)
               & (cmp_end[:, None] >= slc_start[None, :])).astype(np.float32)
    start = 2.0 ** (-8.0 / NSA_HEADS)
    slopes = np.asarray(start ** np.arange(1, NSA_HEADS + 1), np.float32).reshape(NSA_GROUPS, NSA_HPG)
    slope_rows = np.repeat(slopes, qb, axis=1)

    pos = np.arange(seq)
    kconst = np.zeros((seq, 256), np.float32)
    kconst[pos, hd + pos // SLC_LEN] = 1.0
    kconst[:, 2 * hd + 0:2 * hd + 3] = (pos // 64 * 64)[:, None]
    kconst[:, 2 * hd + 3:2 * hd + 6] = (pos % 64)[:, None]
    qconst = np.zeros((NSA_GROUPS, NSA_HPG * qb, 128), np.float32)
    for j, piece in enumerate(_bf16_pieces(slope_rows)):
        qconst[:, :, j] = piece
        qconst[:, :, 3 + j] = piece
    dist = (np.arange(NSA_HPG * qb) % qb)[:, None] + WIN_PAD - np.arange(WIN_PAD + qb)[None, :]
    cw = np.where((dist >= 0) & (dist < WIN), -slope_rows[:, :, None] * dist[None].astype(np.float32),
                  -MASK_BIG).astype(np.float32)
    return (jnp.asarray(overlap.T), jnp.asarray(kconst, BF16), jnp.asarray(qconst),
            jnp.asarray(slope_rows[:, :, None]), jnp.asarray(cw))


def _nsa_attn(q, gates, kc, vc, kv):
    bsz, seq, _ = q.shape
    qb = NSA_STEP
    rows = NSA_HPG * qb
    gw = NSA_HPG * NSA_HD
    n_cmp_rows = seq // CMP_STRIDE
    n_slc = seq // SLC_LEN
    nw = WIN_PAD + qb
    ovl_t, kconst, qconst, slope_rows, cw = _nsa_constants(seq)
    cmp_spec = lambda: pl.BlockSpec((1, 1, n_cmp_rows, NSA_HD), lambda b, g, i: (b, g, 0, 0))
    kv_spec = lambda t: pl.BlockSpec((None, 1, 1, seq, NSA_HD), lambda b, g, i, t=t: (t, b, g, 0, 0))
    return pl.pallas_call(
        _nsa_attn_kernel,
        grid=(bsz, NSA_GROUPS, seq // qb),
        in_specs=[pl.BlockSpec((1, qb, gw), lambda b, g, i: (b, i, g)),
                  pl.BlockSpec((1, 1, qb, 3 * NSA_HPG), lambda b, g, i: (b, g, i, 0)),
                  pl.BlockSpec((1, rows, 128), lambda b, g, i: (g, 0, 0)),
                  pl.BlockSpec((1, rows, 1), lambda b, g, i: (g, 0, 0)),
                  pl.BlockSpec((n_slc, n_cmp_rows), lambda b, g, i: (0, 0)),
                  pl.BlockSpec((seq, 256), lambda b, g, i: (0, 0)),
                  pl.BlockSpec((1, rows, nw), lambda b, g, i: (g, 0, 0)),
                  cmp_spec(), cmp_spec(), kv_spec(0), kv_spec(1), kv_spec(2), kv_spec(3)],
        out_specs=pl.BlockSpec((1, qb, gw), lambda b, g, i: (b, i, g)),
        out_shape=jax.ShapeDtypeStruct((bsz, seq, NSA_HEADS * NSA_HD), BF16),
        scratch_shapes=[pltpu.VMEM((seq, 256), BF16), pltpu.VMEM((seq, 128), BF16),
                        pltpu.VMEM((WIN_PAD + seq, 128), BF16), pltpu.VMEM((WIN_PAD + seq, 128), BF16),
                        pltpu.VMEM((n_slc, qb), F32),
                        pltpu.VMEM((rows, n_cmp_rows), F32), pltpu.VMEM((rows, n_cmp_rows), BF16),
                        pltpu.VMEM((qb, n_cmp_rows), F32), pltpu.VMEM((rows, 256), BF16),
                        pltpu.VMEM((rows, 1), F32), pltpu.VMEM((rows, 128), F32),
                        pltpu.VMEM((rows, SLC_KEY_TILE), F32), pltpu.VMEM((rows, SLC_KEY_TILE), F32),
                        pltpu.VMEM((rows, SLC_KEY_TILE), BF16),
                        pltpu.VMEM((rows, nw), F32), pltpu.VMEM((rows, nw), BF16),
                        pltpu.VMEM((rows, NSA_HD), F32)],
        compiler_params=_params("parallel", "parallel", "arbitrary"),
        name="nsa_attn",
    )(q, gates, qconst, slope_rows, ovl_t, kconst, cw, kc, vc, kv, kv, kv, kv)


def _nsa_mixer(x, norm_g, shift, scale, shared, w_in, b_gate):
    kcv, kv = shared
    bsz, seq, _ = x.shape
    q, gates = _nsa_proj(x, norm_g, shift, scale, w_in, b_gate)
    gates = gates.reshape(bsz, seq, 3, NSA_GROUPS, NSA_HPG).transpose(0, 3, 1, 2, 4)
    gates = gates.reshape(bsz, NSA_GROUPS, seq, 3 * NSA_HPG)
    return _nsa_attn(q, gates, kcv[0], kcv[1], kv)


def _final_norm_kernel(x_ref, g_ref, o_ref):
    x = x_ref[...]
    o_ref[...] = x * lax.rsqrt(jnp.mean(x * x, axis=-1, keepdims=True) + RMS_EPS) * g_ref[...]


def _final_norm(x, g):
    bsz, seq, d = x.shape
    x2d = x.reshape(bsz * seq, d)
    tm = min(ROW_TILE, seq)
    out = pl.pallas_call(
        _final_norm_kernel,
        grid=(x2d.shape[0] // tm,),
        in_specs=[pl.BlockSpec((tm, d), lambda i: (i, 0)), pl.BlockSpec((1, d), lambda i: (0, 0))],
        out_specs=pl.BlockSpec((tm, d), lambda i: (i, 0)),
        out_shape=jax.ShapeDtypeStruct(x2d.shape, F32),
        compiler_params=_params("parallel"),
        name="final_norm",
    )(x2d, g.reshape(1, d))
    return out.reshape(bsz, seq, d)


def kernel(x, c, ada_w, ada_b, norm1_g, norm2_g, gla_w_in, gla_w_gate2, gla_b_gate, gla_norm_g, gla_w_out, kv_norm_g, kv_ada_w, kv_ada_b, kv_w, cmp_pos, cmp_w1, cmp_b1, cmp_w2, cmp_b2, nsa_w_in, nsa_b_gate, nsa_w_out, router_w, router_b, moe_w_gate_up, moe_b_gate_up, moe_w_down, moe_b_down, final_g):
    bsz, seq, d = x.shape
    mod = _ada_vectors(c, ada_w, ada_b)
    kv_mod = _ada_vectors(c, kv_ada_w[None], kv_ada_b[None])[0]
    vec = lambda m, j: m[:, None, j * d:(j + 1) * d]
    shared = None
    for layer in range(DEPTH):
        m = mod[layer]
        sh1, sc1, g1, sh2, sc2, g2 = (vec(m, j) for j in range(6))
        if layer < N_A_LAYERS:
            i = layer
            q, k, v, r, la = _gla_proj(x, norm1_g[layer], sh1, sc1, gla_w_in[i], gla_w_gate2[i], gla_b_gate[i])
            o = _gla_core(q, k, v, r, la, gla_norm_g[i])
            x = _res_matmul(o, gla_w_out[i], x, g1)
        else:
            i = layer - N_A_LAYERS
            o = _nsa_mixer(x, norm1_g[layer], sh1, sc1, shared, nsa_w_in[i], nsa_b_gate[i])
            x = _res_matmul(o, nsa_w_out[i], x, g1)
        x = _moe_layer(x, norm2_g[layer], sh2, sc2, g2, router_w[layer], router_b[layer],
                       layer, moe_w_gate_up, moe_b_gate_up[layer], moe_w_down, moe_b_down[layer])
        if layer == N_A_LAYERS - 1:
            kv_cmp, kv = _kv_proj(x, kv_norm_g, vec(kv_mod, 0), vec(kv_mod, 1), kv_w)
            shared = (_compress(kv_cmp, cmp_pos, cmp_w1, cmp_b1, cmp_w2, cmp_b2), kv)
    return _final_norm(x, final_g)
```

```python
import functools

import numpy as np
import jax
import jax.numpy as jnp
from jax import lax
from jax.experimental import pallas as pl
from jax.experimental.pallas import tpu as pltpu

F32 = jnp.float32
BF16 = jnp.bfloat16
HIGHEST = lax.Precision.HIGHEST

D_MODEL = 1024
DEPTH = 4
N_A_LAYERS = DEPTH // 2
RMS_EPS = 1e-5

GLA_HEADS = 4
GLA_KEY_DIM = D_MODEL // 2
GLA_VAL_DIM = D_MODEL
GLA_HK = GLA_KEY_DIM // GLA_HEADS
GLA_HV = GLA_VAL_DIM // GLA_HEADS
GLA_RANK = 16
GLA_TAU = 16.0
GLA_CHUNK = 64

NSA_HEADS = 16
NSA_GROUPS = 4
NSA_HPG = NSA_HEADS // NSA_GROUPS
NSA_HD = D_MODEL // NSA_HEADS
CMP_LEN = 32
CMP_STRIDE = 16
CMP_HIDDEN = 2 * NSA_HD
SLC_LEN = 64
SLC_TOPK = 16
WIN = 512
NSA_Q_BLOCK = 64

N_EXPERTS = 32
TOP_K = 4
EXPERT_FF = D_MODEL
SWIGLU_LIMIT = 7.0
SWIGLU_ALPHA = 1.702

FORCE = 1e4
NEG = -1e30

VMEM_LIMIT_BYTES = 56 * 1024 * 1024

ROW_TILE = 512
EXPERT_TILE = 256
LANES = 128
ROW_CHUNK = 8
ROUTE_SPARE_ROWS = 96
ROUTE_TILE = 512
GLA_STEP = 512
SLC_KEY_TILE = 512
NSA_STEP = 256
NSA_ROW_CHUNK = 64
RANK_UNROLL = 4
WIN_PAD = WIN
MASK_BIG = 1e30


def _params(*sem):
    return pltpu.CompilerParams(dimension_semantics=sem, vmem_limit_bytes=VMEM_LIMIT_BYTES)


def _norm_mod(x, g, shift, scale):
    y = x * lax.rsqrt(jnp.mean(x * x, axis=-1, keepdims=True) + RMS_EPS)
    return (y * g) * (1.0 + scale) + shift


def _dot(a, b):
    return jnp.dot(a, b, preferred_element_type=F32)


def _dot_nt(a, b, precision=None):
    return lax.dot_general(a, b, (((1,), (1,)), ((), ())), precision=precision,
                           preferred_element_type=F32)


def _ada_kernel(c_ref, w_ref, b_ref, o_ref):
    c = c_ref[...]
    cs = c * jax.nn.sigmoid(c)
    o_ref[0] = jnp.dot(cs, w_ref[0], precision=HIGHEST, preferred_element_type=F32) + b_ref[0]


def _ada_vectors(c, w, b):
    n_l, d, m = w.shape
    bsz = c.shape[0]
    tn = 1024
    return pl.pallas_call(
        _ada_kernel,
        grid=(n_l, m // tn),
        in_specs=[
            pl.BlockSpec((bsz, d), lambda l, j: (0, 0)),
            pl.BlockSpec((1, d, tn), lambda l, j: (l, 0, j)),
            pl.BlockSpec((1, 1, tn), lambda l, j: (l, 0, j)),
        ],
        out_specs=pl.BlockSpec((1, bsz, tn), lambda l, j: (l, 0, j)),
        out_shape=jax.ShapeDtypeStruct((n_l, bsz, m), F32),
        compiler_params=_params("parallel", "parallel"),
        name="ada_vectors",
    )(c, w, b.reshape(n_l, 1, m))


def _gla_proj_kernel(x_ref, g_ref, sh_ref, sc_ref, wq_ref, wk_ref, wv_ref, wr_ref, wlr_ref,
                     wg2_ref, bg_ref, q_ref, k_ref, v_ref, r_ref, la_ref):
    h = _norm_mod(x_ref[0], g_ref[...], sh_ref[0], sc_ref[0]).astype(BF16)
    q_ref[0] = _dot(h, wq_ref[...]).astype(BF16)
    k_ref[0] = _dot(h, wk_ref[...]).astype(BF16)
    v_ref[0] = _dot(h, wv_ref[...]).astype(BF16)
    r_ref[0] = _dot(h, wr_ref[...]).astype(BF16)
    g_lr = _dot(h, wlr_ref[...])
    z = jnp.dot(g_lr, wg2_ref[...], precision=HIGHEST, preferred_element_type=F32) + bg_ref[...]
    log_sig = jnp.minimum(z, 0.0) - jnp.log(1.0 + jnp.exp(-jnp.abs(z)))
    la_ref[0] = log_sig / GLA_TAU


def _gla_proj(x, g, shift, scale, w_in, w_gate2, b_gate):
    bsz, seq, d = x.shape
    tm = min(ROW_TILE, seq)
    kd, vd = GLA_KEY_DIM, GLA_VAL_DIM
    wq = w_in[:, :kd].astype(BF16)
    wk = w_in[:, kd:2 * kd].astype(BF16)
    wv = w_in[:, 2 * kd:2 * kd + vd].astype(BF16)
    wr = w_in[:, 2 * kd + vd:2 * kd + 2 * vd].astype(BF16)
    wlr = w_in[:, 2 * kd + 2 * vd:].astype(BF16)
    full = lambda shape: pl.BlockSpec(shape, lambda b, i: (0,) * len(shape))
    row = lambda n: pl.BlockSpec((1, tm, n), lambda b, i: (b, i, 0))
    vec = pl.BlockSpec((1, 1, d), lambda b, i: (b, 0, 0))
    return pl.pallas_call(
        _gla_proj_kernel,
        grid=(bsz, seq // tm),
        in_specs=[row(d), full((1, d)), vec, vec, full((d, kd)), full((d, kd)), full((d, vd)),
                  full((d, vd)), full((d, GLA_RANK)), full((GLA_RANK, kd)), full((1, kd))],
        out_specs=[row(kd), row(kd), row(vd), row(vd), row(kd)],
        out_shape=[jax.ShapeDtypeStruct((bsz, seq, kd), BF16),
                   jax.ShapeDtypeStruct((bsz, seq, kd), BF16),
                   jax.ShapeDtypeStruct((bsz, seq, vd), BF16),
                   jax.ShapeDtypeStruct((bsz, seq, vd), BF16),
                   jax.ShapeDtypeStruct((bsz, seq, kd), F32)],
        compiler_params=_params("parallel", "parallel"),
        name="gla_proj",
    )(x, g.reshape(1, d), shift, scale, wq, wk, wv, wr, wlr, w_gate2, b_gate.reshape(1, kd))


def _gla_core_kernel(q_ref, k_ref, v_ref, r_ref, la_ref, ng_ref, o_ref, state_ref, *, n_chunks):
    @pl.when(pl.program_id(1) == 0)
    def _():
        state_ref[...] = jnp.zeros_like(state_ref)

    c_len = GLA_CHUNK
    row = lax.broadcasted_iota(jnp.int32, (c_len, c_len), 0)
    col = lax.broadcasted_iota(jnp.int32, (c_len, c_len), 1)
    causal = col <= row
    row_k = lax.broadcasted_iota(jnp.int32, (c_len, GLA_HK), 0)
    shifts = [1 << j for j in range(c_len.bit_length() - 1)]

    def chunk(c, carry):
        c0 = pl.multiple_of(c * c_len, c_len)
        for hh in range(GLA_HEADS):
            kc = slice(hh * GLA_HK, (hh + 1) * GLA_HK)
            vc = slice(hh * GLA_HV, (hh + 1) * GLA_HV)
            b = la_ref[0, pl.ds(c0, c_len), kc]
            for shift in shifts:
                b = b + jnp.where(row_k >= shift, pltpu.roll(b, shift, 0), 0.0)
            q = q_ref[0, pl.ds(c0, c_len), kc].astype(F32) * (GLA_HK ** -0.5)
            k = k_ref[0, pl.ds(c0, c_len), kc].astype(F32)
            v = v_ref[0, pl.ds(c0, c_len), vc]
            q_dec = (q * jnp.exp(b)).astype(BF16)
            k_intra = (k * jnp.exp(-b)).astype(BF16)
            b_t = b.T
            bl_t = b_t[:, c_len - 1:c_len]
            k_inter_t = (k.T * jnp.exp(bl_t - b_t)).astype(BF16)
            att = jnp.where(causal, _dot_nt(q_dec, k_intra), 0.0).astype(BF16)
            state = state_ref[hh]
            o = _dot(att, v) + _dot(q_dec, state.astype(BF16))
            state_ref[hh] = state * jnp.exp(bl_t) + _dot(k_inter_t, v)
            o = o * lax.rsqrt(jnp.mean(o * o, axis=-1, keepdims=True) + RMS_EPS)
            r = r_ref[0, pl.ds(c0, c_len), vc].astype(F32)
            o_ref[0, pl.ds(c0, c_len), vc] = ((o * ng_ref[:, vc]) * (r * jax.nn.sigmoid(r))).astype(BF16)
        return carry

    lax.fori_loop(0, n_chunks, chunk, 0)


def _gla_core(q, k, v, r, la, norm_g):
    bsz, seq, _ = q.shape
    ts = min(GLA_STEP, seq)
    kern = functools.partial(_gla_core_kernel, n_chunks=ts // GLA_CHUNK)
    kd = lambda: pl.BlockSpec((1, ts, GLA_KEY_DIM), lambda b, s: (b, s, 0))
    vd = lambda: pl.BlockSpec((1, ts, GLA_VAL_DIM), lambda b, s: (b, s, 0))
    return pl.pallas_call(
        kern,
        grid=(bsz, seq // ts),
        in_specs=[kd(), kd(), vd(), vd(), kd(), pl.BlockSpec((1, GLA_VAL_DIM), lambda b, s: (0, 0))],
        out_specs=vd(),
        out_shape=jax.ShapeDtypeStruct((bsz, seq, GLA_VAL_DIM), BF16),
        scratch_shapes=[pltpu.VMEM((GLA_HEADS, GLA_HK, GLA_HV), F32)],
        compiler_params=_params("parallel", "arbitrary"),
        name="gla_core",
    )(q, k, v, r, la, norm_g.reshape(1, GLA_VAL_DIM))


def _res_matmul_kernel(a_ref, w_ref, x_ref, gate_ref, o_ref):
    o_ref[0] = x_ref[0] + gate_ref[0] * _dot(a_ref[0], w_ref[...])


def _res_matmul(a, w, x, gate):
    bsz, seq, d = x.shape
    kdim = a.shape[-1]
    tm = min(ROW_TILE, seq)
    return pl.pallas_call(
        _res_matmul_kernel,
        grid=(bsz, seq // tm),
        in_specs=[pl.BlockSpec((1, tm, kdim), lambda b, i: (b, i, 0)),
                  pl.BlockSpec((kdim, d), lambda b, i: (0, 0)),
                  pl.BlockSpec((1, tm, d), lambda b, i: (b, i, 0)),
                  pl.BlockSpec((1, 1, d), lambda b, i: (b, 0, 0))],
        out_specs=pl.BlockSpec((1, tm, d), lambda b, i: (b, i, 0)),
        out_shape=jax.ShapeDtypeStruct((bsz, seq, d), F32),
        compiler_params=_params("parallel", "parallel"),
        name="res_matmul",
    )(a, w.astype(BF16), x, gate)


def _route_kernel(x_ref, g_ref, sh_ref, sc_ref, rwt_ref, rb_ref, xs_ref, w_ref, slot_ref, cnt_ref):
    tm = x_ref.shape[0]
    n_slot = xs_ref.shape[0]
    h = _norm_mod(x_ref[...], g_ref[...], sh_ref[0], sc_ref[0])
    logits = _dot_nt(rwt_ref[...], h, precision=HIGHEST) + rb_ref[...]
    e_iota = lax.broadcasted_iota(jnp.int32, logits.shape, 0)
    vals, hots = [], []
    for _ in range(TOP_K):
        m = jnp.max(logits, axis=0, keepdims=True)
        idx = jnp.min(jnp.where(logits == m, e_iota, N_EXPERTS), axis=0, keepdims=True)
        hot = e_iota == idx
        vals.append(m)
        hots.append(hot)
        logits = jnp.where(hot, -jnp.inf, logits)
    exps = [jnp.exp(v - vals[0]) for v in vals]
    denom = exps[0] + exps[1] + exps[2] + exps[3]
    for kk in range(TOP_K):
        w_ref[kk:kk + 1, :] = exps[kk] / denom

    sel = (hots[0] | hots[1] | hots[2] | hots[3]).astype(F32)
    s_iota = lax.broadcasted_iota(jnp.int32, (tm, tm), 0)
    t_iota = lax.broadcasted_iota(jnp.int32, (tm, tm), 1)
    before = (s_iota < t_iota).astype(BF16)
    prefix = _dot(sel.astype(BF16), before)
    cnt = jnp.sum(sel, axis=1, keepdims=True)
    chunks = jnp.floor((cnt + (ROW_CHUNK - 1.0)) * (1.0 / ROW_CHUNK))
    lower = (lax.broadcasted_iota(jnp.int32, (N_EXPERTS, N_EXPERTS), 0) >
             lax.broadcasted_iota(jnp.int32, (N_EXPERTS, N_EXPERTS), 1)).astype(BF16)
    seg0 = _dot(lower, jnp.broadcast_to(chunks, (N_EXPERTS, LANES)).astype(BF16))[:, 0:1] * ROW_CHUNK
    slots = [jnp.sum(jnp.where(hot, prefix + seg0, 0.0), axis=0, keepdims=True).astype(jnp.int32)
             for hot in hots]
    r_iota = lax.broadcasted_iota(jnp.int32, (n_slot, tm), 0)
    q = jnp.zeros((n_slot, tm), F32)
    for kk in range(TOP_K):
        q = q + jnp.where(r_iota == slots[kk], 1.0, 0.0)
    for kk in range(TOP_K):
        slot_ref[kk:kk + 1, :] = slots[kk]
    xs_ref[...] = _dot((q > 0.0).astype(BF16), h.astype(BF16))

    cnt_ref[0] = jnp.broadcast_to(cnt, (N_EXPERTS, LANES)).astype(jnp.int32)


def _route(x2d, g, shift, scale, router_w, router_b, seq):
    n_tok, d = x2d.shape
    tm = min(ROUTE_TILE, seq)
    per_b = seq // tm
    n_win = n_tok // tm
    n_slot = tm * TOP_K + N_EXPERTS * ROW_CHUNK + ROUTE_SPARE_ROWS
    vec = pl.BlockSpec((1, 1, d), lambda i: (i // per_b, 0, 0))
    return pl.pallas_call(
        _route_kernel,
        grid=(n_win,),
        in_specs=[pl.BlockSpec((tm, d), lambda i: (i, 0)),
                  pl.BlockSpec((1, d), lambda i: (0, 0)), vec, vec,
                  pl.BlockSpec((N_EXPERTS, d), lambda i: (0, 0)),
                  pl.BlockSpec((N_EXPERTS, 1), lambda i: (0, 0))],
        out_specs=[pl.BlockSpec((n_slot, d), lambda i: (i, 0)),
                   pl.BlockSpec((TOP_K, tm), lambda i: (0, i)),
                   pl.BlockSpec((TOP_K, tm), lambda i: (0, i)),
                   pl.BlockSpec((1, N_EXPERTS, LANES), lambda i: (i, 0, 0))],
        out_shape=[jax.ShapeDtypeStruct((n_win * n_slot, d), F32),
                   jax.ShapeDtypeStruct((TOP_K, n_tok), F32),
                   jax.ShapeDtypeStruct((TOP_K, n_tok), jnp.int32),
                   jax.ShapeDtypeStruct((n_win, N_EXPERTS, LANES), jnp.int32)],
        compiler_params=_params("parallel"),
        name="moe_route",
    )(x2d, g.reshape(1, d), shift, scale, router_w.T, router_b.reshape(N_EXPERTS, 1))


def _expert_kernel(te_ref, nu_ref, src_hbm, dst_hbm, xs_in, wgu_ref, wd_ref, bg_ref, bu_ref, bd_ref, xy_hbm,
                   wg_s, wu_s, wd_s, tr_s, xbuf, ybuf, tok_s, out_s, tok_sem, out_sem, g_sem, s_sem):
    del xs_in
    i = pl.program_id(0)
    n_tiles = pl.num_programs(0)
    tm = xbuf.shape[1]
    used = i < nu_ref[0]
    fresh = (i == 0) | (te_ref[i] != te_ref[jnp.maximum(i - 1, 0)])
    slot = i % 2
    other = 1 - slot

    def tok_copy(t, sl):
        return pltpu.make_async_copy(src_hbm.at[t], tok_s.at[sl], tok_sem.at[sl])

    def out_copy(t, sl):
        return pltpu.make_async_copy(dst_hbm.at[t], out_s.at[sl], out_sem.at[sl])

    def gather_start(sl):
        for c in range(tm // ROW_CHUNK):
            src = pl.multiple_of(tok_s[sl, c] * ROW_CHUNK, ROW_CHUNK)
            pltpu.make_async_copy(xy_hbm.at[pl.ds(src, ROW_CHUNK), :],
                                  xbuf.at[sl, pl.ds(c * ROW_CHUNK, ROW_CHUNK), :], g_sem.at[sl]).start()

    def gather_wait(sl):
        pltpu.make_async_copy(xy_hbm.at[pl.ds(0, tm), :], xbuf.at[sl], g_sem.at[sl]).wait()

    def scatter_start(sl):
        for c in range(tm // ROW_CHUNK):
            dst = pl.multiple_of(out_s[sl, c] * ROW_CHUNK, ROW_CHUNK)
            pltpu.make_async_copy(ybuf.at[sl, pl.ds(c * ROW_CHUNK, ROW_CHUNK), :],
                                  xy_hbm.at[pl.ds(dst, ROW_CHUNK), :], s_sem.at[sl]).start()

    def scatter_wait(sl):
        pltpu.make_async_copy(ybuf.at[sl], xy_hbm.at[pl.ds(0, tm), :], s_sem.at[sl]).wait()

    @pl.when(i == 0)
    def _():
        tok_copy(0, 0).start()
        out_copy(0, 0).start()
        tok_copy(0, 0).wait()
        gather_start(0)
        tok_copy(jnp.minimum(1, n_tiles - 1), 1).start()

    @pl.when(used & fresh)
    def _():
        n_slab, chunk, lanes = tr_s.shape
        half = chunk // 2
        for c in range(wgu_ref.shape[2] // chunk):
            t = wgu_ref[0, :, c * chunk:(c + 1) * chunk].T
            for j in range(n_slab):
                tr_s[j] = t[:, j * lanes:(j + 1) * lanes]
            for j in range(n_slab):
                wg_s[j * lanes:(j + 1) * lanes, c * half:(c + 1) * half] = (
                    tr_s[j, pl.ds(0, half, stride=2), :].T.astype(BF16))
                wu_s[j * lanes:(j + 1) * lanes, c * half:(c + 1) * half] = (
                    tr_s[j, pl.ds(1, half, stride=2), :].T.astype(BF16))
        wd_s[...] = wd_ref[0].astype(BF16)

    @pl.when(used)
    def _():
        nxt = jnp.minimum(i + 1, n_tiles - 1)
        tok_copy(nxt, other).wait()
        gather_start(other)
        tok_copy(jnp.minimum(i + 2, n_tiles - 1), slot).start()
        gather_wait(slot)
        xb = xbuf[slot].astype(BF16)
        gate = jnp.minimum(_dot(xb, wg_s[...]) + bg_ref[0], SWIGLU_LIMIT)
        up = jnp.clip(_dot(xb, wu_s[...]) + bu_ref[0], -SWIGLU_LIMIT, SWIGLU_LIMIT)
        act = (up + 1.0) * gate * jax.nn.sigmoid(SWIGLU_ALPHA * gate)
        ybuf[slot] = _dot(act.astype(BF16), wd_s[...]) + bd_ref[0]
        out_copy(i, slot).wait()
        scatter_start(slot)
        out_copy(nxt, other).start()

    @pl.when(used & (i > 0))
    def _():
        scatter_wait(other)

    @pl.when(i == nu_ref[0] - 1)
    def _():
        gather_wait(other)
        scatter_wait(slot)
        tok_copy(0, slot).wait()
        out_copy(0, other).wait()


def _experts(xs, src_tiles, dst_tiles, tile_expert, n_used, layer, w_gu, w_down, bg, bu, bd):
    d = xs.shape[1]
    n_tiles = src_tiles.shape[0]
    tm = EXPERT_TILE
    ff = w_down.shape[2]
    wsel = lambda i, te, nu: (te[i], 0, 0)
    lsel = lambda i, te, nu: (layer, te[i], 0, 0)
    hbm = pl.BlockSpec(memory_space=pl.ANY)
    return pl.pallas_call(
        _expert_kernel,
        grid_spec=pltpu.PrefetchScalarGridSpec(
            num_scalar_prefetch=2,
            grid=(n_tiles,),
            in_specs=[hbm, hbm, hbm,
                      pl.BlockSpec((None, 1, d, 2 * ff), lsel),
                      pl.BlockSpec((None, 1, ff, d), lsel),
                      pl.BlockSpec((1, 1, ff), wsel), pl.BlockSpec((1, 1, ff), wsel),
                      pl.BlockSpec((1, 1, d), wsel)],
            out_specs=hbm,
            scratch_shapes=[pltpu.VMEM((d, ff), BF16), pltpu.VMEM((d, ff), BF16),
                            pltpu.VMEM((ff, d), BF16), pltpu.VMEM((d // LANES, 2 * LANES, LANES), F32),
                            pltpu.VMEM((2, tm, d), F32), pltpu.VMEM((2, tm, d), F32),
                            pltpu.SMEM((2, tm // ROW_CHUNK), jnp.int32),
                            pltpu.SMEM((2, tm // ROW_CHUNK), jnp.int32),
                            pltpu.SemaphoreType.DMA((2,)), pltpu.SemaphoreType.DMA((2,)),
                            pltpu.SemaphoreType.DMA((2,)), pltpu.SemaphoreType.DMA((2,))]),
        out_shape=jax.ShapeDtypeStruct(xs.shape, xs.dtype),
        input_output_aliases={4: 0},
        compiler_params=_params("arbitrary"),
        name="moe_experts",
    )(tile_expert, n_used, src_tiles, dst_tiles, xs, w_gu, w_down, bg, bu, bd)


def _combine_kernel(y_ref, slot_ref, w_ref, x_ref, gate_ref, o_ref):
    tm = x_ref.shape[0]
    n_slot = y_ref.shape[0]
    r_iota = lax.broadcasted_iota(jnp.int32, (tm, n_slot), 1)
    w = w_ref[...]
    slot = slot_ref[...]
    c = jnp.zeros((tm, n_slot), F32)
    for kk in range(TOP_K):
        c = c + jnp.where(r_iota == slot[:, kk:kk + 1], w[:, kk:kk + 1], 0.0)
    y = _dot(c.astype(BF16), y_ref[...].astype(BF16))
    o_ref[...] = x_ref[...] + gate_ref[0] * y


def _combine(y, slots, w_tok, x2d, gate, seq):
    n_tok, d = x2d.shape
    tm = min(ROUTE_TILE, seq)
    per_b = seq // tm
    n_win = n_tok // tm
    n_slot = y.shape[0] // n_win
    return pl.pallas_call(
        _combine_kernel,
        grid=(n_win,),
        in_specs=[pl.BlockSpec((n_slot, d), lambda i: (i, 0)),
                  pl.BlockSpec((tm, TOP_K), lambda i: (i, 0)),
                  pl.BlockSpec((tm, TOP_K), lambda i: (i, 0)),
                  pl.BlockSpec((tm, d), lambda i: (i, 0)),
                  pl.BlockSpec((1, 1, d), lambda i: (i // per_b, 0, 0))],
        out_specs=pl.BlockSpec((tm, d), lambda i: (i, 0)),
        out_shape=jax.ShapeDtypeStruct((n_tok, d), F32),
        compiler_params=_params("parallel"),
        name="moe_combine",
    )(y, slots, w_tok, x2d, gate)


def _moe_layer(x, norm_g, shift, scale, gate, router_w, router_b, layer, w_gu, b_gu, w_down, b_down):
    bsz, seq, d = x.shape
    n_tok = bsz * seq
    x2d = x.reshape(n_tok, d)
    xs, w_t, slot_t, cnt = _route(x2d, norm_g, shift, scale, router_w, router_b, seq)

    cnt = cnt[:, :, 0]
    n_win = cnt.shape[0]
    win_chunks = xs.shape[0] // n_win // ROW_CHUNK
    tile_chunks = EXPERT_TILE // ROW_CHUNK
    c8 = (cnt + ROW_CHUNK - 1) // ROW_CHUNK
    seg_off = jnp.cumsum(c8, axis=1) - c8
    cum_w = jnp.cumsum(c8, axis=0)
    total = cum_w[-1]
    padded = (total + tile_chunks - 1) // tile_chunks * tile_chunks
    ends = jnp.cumsum(padded)
    starts = ends - padded
    n_chunk = n_tok * TOP_K // ROW_CHUNK + n_win * N_EXPERTS + N_EXPERTS * tile_chunks
    n_tiles = n_chunk // tile_chunks
    tile0 = jnp.arange(n_tiles, dtype=jnp.int32) * tile_chunks
    tile_expert = jnp.minimum(jnp.sum((ends[None, :] <= tile0[:, None]).astype(jnp.int32), axis=1),
                              N_EXPERTS - 1)
    e_hot = (tile_expert[:, None] == jnp.arange(N_EXPERTS, dtype=jnp.int32)[None, :]).astype(jnp.int32)
    pick = lambda tab: jnp.sum(e_hot[:, :, None] * tab.T[None, :, :], axis=1)
    cum_t, c8_t, off_t = pick(cum_w)[:, None, :], pick(c8)[:, None, :], pick(seg_off)[:, None, :]
    lane = jnp.arange(tile_chunks, dtype=jnp.int32)[None, :]
    rank = (tile0 - jnp.sum(e_hot * starts[None, :], axis=1))[:, None] + lane
    valid = rank < jnp.sum(e_hot * total[None, :], axis=1)[:, None]
    w_of = jnp.minimum(jnp.sum((cum_t <= rank[:, :, None]).astype(jnp.int32), axis=2), n_win - 1)
    w_hot = (w_of[:, :, None] == jnp.arange(n_win, dtype=jnp.int32)[None, None, :]).astype(jnp.int32)
    seg_first = jnp.sum(w_hot * (cum_t - c8_t), axis=2)
    src = w_of * win_chunks + jnp.sum(w_hot * off_t, axis=2) + rank - seg_first
    spare0 = win_chunks - ROUTE_SPARE_ROWS // ROW_CHUNK
    u = (jnp.arange(n_tiles, dtype=jnp.int32) % 2)[:, None] * tile_chunks + lane
    dst = jnp.where(valid, src, (u % n_win) * win_chunks + spare0 + u // n_win)
    src = jnp.where(valid, src, win_chunks - 1)
    n_used = (ends[-1:] // tile_chunks).astype(jnp.int32)

    y = _experts(xs, src.reshape(n_tiles, tile_chunks), dst.reshape(n_tiles, tile_chunks), tile_expert,
                 n_used, layer, w_gu, w_down, b_gu[:, None, 0::2], b_gu[:, None, 1::2], b_down[:, None, :])
    out = _combine(y, slot_t.T, w_t.T, x2d, gate, seq)
    return out.reshape(bsz, seq, d)


def _kv_proj_kernel(x_ref, g_ref, sh_ref, sc_ref, w_ref, cmp_ref, kv_ref):
    h = _norm_mod(x_ref[0], g_ref[...], sh_ref[0], sc_ref[0]).astype(BF16)
    kv = _dot(h, w_ref[...])
    for t in range(6):
        for g in range(NSA_GROUPS):
            c0 = (t * NSA_GROUPS + g) * NSA_HD
            piece = kv[:, c0:c0 + NSA_HD]
            if t < 2:
                cmp_ref[t, 0, g] = piece
            else:
                kv_ref[t - 2, 0, g] = piece.astype(BF16)


def _kv_proj(x, g, shift, scale, kv_w):
    bsz, seq, d = x.shape
    tm = min(ROW_TILE, seq)
    n_out = kv_w.shape[1]
    vec = pl.BlockSpec((1, 1, d), lambda b, i: (b, 0, 0))
    return pl.pallas_call(
        _kv_proj_kernel,
        grid=(bsz, seq // tm),
        in_specs=[pl.BlockSpec((1, tm, d), lambda b, i: (b, i, 0)),
                  pl.BlockSpec((1, d), lambda b, i: (0, 0)), vec, vec,
                  pl.BlockSpec((d, n_out), lambda b, i: (0, 0))],
        out_specs=[pl.BlockSpec((2, 1, NSA_GROUPS, tm, NSA_HD), lambda b, i: (0, b, 0, i, 0)),
                   pl.BlockSpec((4, 1, NSA_GROUPS, tm, NSA_HD), lambda b, i: (0, b, 0, i, 0))],
        out_shape=[jax.ShapeDtypeStruct((2, bsz, NSA_GROUPS, seq, NSA_HD), F32),
                   jax.ShapeDtypeStruct((4, bsz, NSA_GROUPS, seq, NSA_HD), BF16)],
        compiler_params=_params("parallel", "parallel"),
        name="nsa_kv_proj",
    )(x, g.reshape(1, d), shift, scale, kv_w.astype(BF16))


def _compress_kernel(x_ref, pos_ref, w1_ref, b1_ref, w2_ref, b2_ref, o_ref, *, n_rows):
    for g in range(NSA_GROUPS):
        first = jnp.zeros((n_rows, CMP_HIDDEN), F32)
        second = jnp.zeros((n_rows, CMP_HIDDEN), F32)
        for l in range(CMP_STRIDE):
            rows = x_ref[0, 0, g, pl.ds(l, n_rows, stride=CMP_STRIDE), :]
            first = first + _dot((rows + pos_ref[0, l:l + 1, :]).astype(BF16), w1_ref[0, l])
            second = second + _dot((rows + pos_ref[0, CMP_STRIDE + l:CMP_STRIDE + l + 1, :]).astype(BF16),
                                   w1_ref[0, CMP_STRIDE + l])
        pre = first + pltpu.roll(second, n_rows - 1, 0) + b1_ref[0]
        hid = 0.5 * pre * (1.0 + jnp.tanh(0.7978845608028654 * (pre + 0.044715 * pre * pre * pre)))
        o_ref[0, 0, g] = (_dot(hid.astype(BF16), w2_ref[0]) + b2_ref[0]).astype(BF16)


def _compress(kv_cmp, cmp_pos, cmp_w1, cmp_b1, cmp_w2, cmp_b2):
    _, bsz, _, seq, _ = kv_cmp.shape
    n_rows = seq // CMP_STRIDE
    kern = functools.partial(_compress_kernel, n_rows=n_rows)
    w1 = cmp_w1.reshape(2, CMP_LEN, NSA_HD, CMP_HIDDEN).astype(BF16)
    return pl.pallas_call(
        kern,
        grid=(2, bsz),
        in_specs=[pl.BlockSpec((1, 1, NSA_GROUPS, seq, NSA_HD), lambda t, b: (t, b, 0, 0, 0)),
                  pl.BlockSpec((1, CMP_LEN, NSA_HD), lambda t, b: (t, 0, 0)),
                  pl.BlockSpec((1, CMP_LEN, NSA_HD, CMP_HIDDEN), lambda t, b: (t, 0, 0, 0)),
                  pl.BlockSpec((1, 1, CMP_HIDDEN), lambda t, b: (t, 0, 0)),
                  pl.BlockSpec((1, CMP_HIDDEN, NSA_HD), lambda t, b: (t, 0, 0)),
                  pl.BlockSpec((1, 1, NSA_HD), lambda t, b: (t, 0, 0))],
        out_specs=pl.BlockSpec((1, 1, NSA_GROUPS, n_rows, NSA_HD), lambda t, b: (t, b, 0, 0, 0)),
        out_shape=jax.ShapeDtypeStruct((2, bsz, NSA_GROUPS, n_rows, NSA_HD), BF16),
        compiler_params=_params("parallel", "parallel"),
        name="nsa_compress",
    )(kv_cmp, cmp_pos, w1, cmp_b1[:, None, :], cmp_w2.astype(BF16), cmp_b2[:, None, :])


def _nsa_proj_kernel(x_ref, g_ref, sh_ref, sc_ref, wq_ref, wg_ref, bg_ref, q_ref, gate_ref):
    h = _norm_mod(x_ref[0], g_ref[...], sh_ref[0], sc_ref[0]).astype(BF16)
    q_ref[0] = (_dot(h, wq_ref[...]) * (NSA_HD ** -0.5)).astype(BF16)
    gate_ref[0] = jax.nn.sigmoid(_dot(h, wg_ref[...]) + bg_ref[...])


def _nsa_proj(x, g, shift, scale, w_in, b_gate):
    bsz, seq, d = x.shape
    tm = min(ROW_TILE, seq)
    nq = NSA_HEADS * NSA_HD
    ng = 3 * NSA_HEADS
    vec = pl.BlockSpec((1, 1, d), lambda b, i: (b, 0, 0))
    return pl.pallas_call(
        _nsa_proj_kernel,
        grid=(bsz, seq // tm),
        in_specs=[pl.BlockSpec((1, tm, d), lambda b, i: (b, i, 0)),
                  pl.BlockSpec((1, d), lambda b, i: (0, 0)), vec, vec,
                  pl.BlockSpec((d, nq), lambda b, i: (0, 0)),
                  pl.BlockSpec((d, ng), lambda b, i: (0, 0)),
                  pl.BlockSpec((1, ng), lambda b, i: (0, 0))],
        out_specs=[pl.BlockSpec((1, tm, nq), lambda b, i: (b, i, 0)),
                   pl.BlockSpec((1, tm, ng), lambda b, i: (b, i, 0))],
        out_shape=[jax.ShapeDtypeStruct((bsz, seq, nq), BF16),
                   jax.ShapeDtypeStruct((bsz, seq, ng), F32)],
        compiler_params=_params("parallel", "parallel"),
        name="nsa_proj",
    )(x, g.reshape(1, d), shift, scale, w_in[:, :nq].astype(BF16), w_in[:, nq:].astype(BF16),
      b_gate.reshape(1, ng))


def _nsa_attn_kernel(q_ref, gate_ref, qc_ref, slope_ref, ovl_ref, kconst_ref, cw_ref, kc_ref, vc_ref,
                     ks_ref, vs_ref, kw_ref, vw_ref, o_ref, ksa, vsa, kwa, vwa, score_s,
                     sc_s, pc_s, psum_s, qa_s, m_s, acc_s, sa_s, sb_s, ps_s, sw_s, pw_s, ow_s):
    qi = pl.program_id(2)
    qs = NSA_STEP
    hd = NSA_HD
    rows = NSA_HPG * qs
    seq = ks_ref.shape[2]
    q0 = pl.multiple_of(qi * qs, qs)

    @pl.when(qi == 0)
    def _():
        ksa[...] = kconst_ref[...]
        ksa[:, 0:hd] = ks_ref[0, 0]
        one_col = (lax.broadcasted_iota(jnp.int32, (seq, 128), 1) == hd).astype(BF16)
        vsa[...] = one_col
        vsa[:, 0:hd] = vs_ref[0, 0]
        flag_col = (lax.broadcasted_iota(jnp.int32, (WIN_PAD, 128), 1) == hd).astype(BF16)
        kwa[0:WIN_PAD, :] = flag_col
        kwa[WIN_PAD:, :] = jnp.zeros((seq, 128), BF16)
        kwa[WIN_PAD:, 0:hd] = kw_ref[0, 0]
        vwa[0:WIN_PAD, :] = jnp.zeros((WIN_PAD, 128), BF16)
        vwa[WIN_PAD:, :] = one_col
        vwa[WIN_PAD:, 0:hd] = vw_ref[0, 0]
        qa_s[:, 2 * hd:] = qc_ref[0]

    qt = q_ref[0].astype(F32)
    q32 = jnp.concatenate([qt[:, hh * hd:(hh + 1) * hd] for hh in range(NSA_HPG)], axis=0)
    q = q32.astype(BF16)
    slope = slope_ref[0]
    t_row = q0 + lax.broadcasted_iota(jnp.int32, (rows, 1), 0) % qs

    rc = NSA_ROW_CHUNK

    nw = WIN_PAD + qs
    q_w = jnp.concatenate([q32, jnp.full((rows, hd), -MASK_BIG, F32)], axis=1).astype(BF16)
    sw_s[...] = _dot_nt(q_w, kwa[pl.ds(q0, nw), :])
    rw = rc // 2
    for c in range(rows // rw):
        r = slice(c * rw, (c + 1) * rw)
        s_w = sw_s[r, :] + cw_ref[0, r, :]
        m_w = jnp.max(s_w, axis=-1, keepdims=True)
        pw_s[r, :] = jnp.exp(s_w - m_w).astype(BF16)
    acc_w = _dot(pw_s[...], vwa[pl.ds(q0, nw), :])
    ow_s[...] = acc_w[:, 0:hd] * (1.0 / acc_w[:, hd:hd + 1])

    n_cmp = kc_ref.shape[2]
    cmp_end = lax.broadcasted_iota(jnp.int32, (1, n_cmp), 1) * CMP_STRIDE + (CMP_LEN - 1)
    cmp_bias = cmp_end.astype(F32)
    sc_s[...] = _dot_nt(q, kc_ref[0, 0])
    for c in range(rows // rc):
        r = slice(c * rc, (c + 1) * rc)
        ok_c = cmp_end <= t_row[r]
        s_c = jnp.where(ok_c, sc_s[r, :] + slope[r] * cmp_bias, NEG)
        m_c = jnp.max(s_c, axis=-1, keepdims=True)
        e_c = jnp.where(ok_c, jnp.exp(s_c - m_c), 0.0)
        l_c = jnp.sum(e_c, axis=-1, keepdims=True)
        p_c = e_c * (1.0 / jnp.where(l_c > 0.0, l_c, 1.0))
        pc_s[r, :] = p_c.astype(BF16)
        qr = slice((c * rc) % qs, (c * rc) % qs + rc)
        if c * rc < qs:
            psum_s[qr, :] = p_c
        else:
            psum_s[qr, :] = psum_s[qr, :] + p_c
    o_c = _dot(pc_s[...], vc_ref[0, 0])

    imp_t = _dot_nt(ovl_ref[...], psum_s[...], precision=HIGHEST)
    n_slc = imp_t.shape[0]
    blk = lax.broadcasted_iota(jnp.int32, (n_slc, qs), 0)
    cur = (q0 + lax.broadcasted_iota(jnp.int32, (1, qs), 1)) // SLC_LEN
    forced = (blk == 0) | (blk == cur) | (blk == cur - 1)
    score = jnp.where(blk <= cur, imp_t + jnp.where(forced, FORCE, 0.0), -FORCE)
    score_s[...] = score

    def rank_step(ip, rank):
        for u in range(RANK_UNROLL):
            i = RANK_UNROLL * ip + u
            other = score_s[pl.ds(i, 1), :]
            ahead = (other > score) | ((other == score) & (blk > i))
            rank = rank + ahead.astype(F32)
        return rank

    last_blk = (q0 + qs - 1) // SLC_LEN
    n_rank = jnp.where(last_blk >= SLC_TOPK, last_blk // RANK_UNROLL + 1, 0)
    rank = lax.fori_loop(0, n_rank, rank_step, jnp.zeros((n_slc, qs), F32))
    sel_t = (rank < float(min(SLC_TOPK, n_slc))).astype(BF16)
    eye = (lax.broadcasted_iota(jnp.int32, (qs, qs), 0) ==
           lax.broadcasted_iota(jnp.int32, (qs, qs), 1)).astype(BF16)
    sel = _dot_nt(eye, sel_t)
    drop = (sel - 1.0) * MASK_BIG
    if n_slc < hd:
        drop = jnp.concatenate([drop, jnp.zeros((qs, hd - n_slc), F32)], axis=1)
    qa_s[:, 0:2 * hd] = jnp.concatenate([q32, jnp.concatenate([drop] * NSA_HPG, axis=0)], axis=1).astype(BF16)

    tk = SLC_KEY_TILE

    n_full = q0 // tk
    m_s[...] = jnp.full((rows, 1), -MASK_BIG, F32)
    acc_s[...] = jnp.zeros((rows, 128), F32)

    def scores(kt, s_out):
        k0 = pl.multiple_of(kt * tk, tk)
        s_out[...] = _dot_nt(qa_s[...], ksa[pl.ds(k0, tk), :])

    def absorb(s_in, kt, causal):
        k0 = pl.multiple_of(kt * tk, tk)
        kpos = k0 + lax.broadcasted_iota(jnp.int32, (1, tk), 1)
        for c in range(rows // rc):
            r = slice(c * rc, (c + 1) * rc)
            s = s_in[r, :]
            if causal:
                s = jnp.where(kpos <= t_row[r], s, -MASK_BIG)
            m_old = m_s[r, :]
            m_new = jnp.maximum(m_old, jnp.max(s, axis=-1, keepdims=True))
            m_s[r, :] = m_new
            ps_s[r, :] = jnp.exp(s - m_new).astype(BF16)
            acc_s[r, :] = jnp.exp(m_old - m_new) * acc_s[r, :]
        acc_s[...] = acc_s[...] + _dot(ps_s[...], vsa[pl.ds(k0, tk), :])

    scores(0, sa_s)

    def slc_pair(j, carry):
        scores(2 * j + 1, sb_s)
        absorb(sa_s, 2 * j, False)

        @pl.when(2 * j + 1 < n_full)
        def _():
            scores(2 * j + 2, sa_s)
            absorb(sb_s, 2 * j + 1, False)
        return carry

    lax.fori_loop(0, (n_full + 1) // 2, slc_pair, 0)

    @pl.when(n_full % 2 == 0)
    def _():
        absorb(sa_s, n_full, True)

    @pl.when(n_full % 2 == 1)
    def _():
        absorb(sb_s, n_full, True)

    o_s = acc_s[:, 0:hd] * (1.0 / acc_s[:, hd:hd + 1])

    gates = gate_ref[0, 0]
    outs = []
    for hh in range(NSA_HPG):
        r0 = hh * qs
        outs.append(gates[:, hh:hh + 1] * o_c[r0:r0 + qs]
                    + gates[:, NSA_HPG + hh:NSA_HPG + hh + 1] * o_s[r0:r0 + qs]
                    + gates[:, 2 * NSA_HPG + hh:2 * NSA_HPG + hh + 1] * ow_s[r0:r0 + qs, :])
    o_ref[0] = jnp.concatenate(outs, axis=1).astype(BF16)


def _bf16_pieces(x):
    x = np.asarray(x, np.float32)
    out = []
    for _ in range(3):
        p = x.astype(BF16).astype(np.float32)
        out.append(p)
        x = x - p
    return out


def _nsa_constants(seq):
    qb, hd = NSA_STEP, NSA_HD
    n_cmp_rows = seq // CMP_STRIDE
    n_slc = seq // SLC_LEN
    cmp_start = np.arange(n_cmp_rows) * CMP_STRIDE
    cmp_end = cmp_start + CMP_LEN - 1
    slc_start = np.arange(n_slc) * SLC_LEN
    overlap = ((cmp_start[:, None] <= slc_start[None, :] + SLC_LEN - 1)
               & (cmp_end[:, None] >= slc_start[None, :])).astype(np.float32)
    start = 2.0 ** (-8.0 / NSA_HEADS)
    slopes = np.asarray(start ** np.arange(1, NSA_HEADS + 1), np.float32).reshape(NSA_GROUPS, NSA_HPG)
    slope_rows = np.repeat(slopes, qb, axis=1)

    pos = np.arange(seq)
    kconst = np.zeros((seq, 256), np.float32)
    kconst[pos, hd + pos // SLC_LEN] = 1.0
    kconst[:, 2 * hd + 0:2 * hd + 3] = (pos // 64 * 64)[:, None]
    kconst[:, 2 * hd + 3:2 * hd + 6] = (pos % 64)[:, None]
    qconst = np.zeros((NSA_GROUPS, NSA_HPG * qb, 128), np.float32)
    for j, piece in enumerate(_bf16_pieces(slope_rows)):
        qconst[:, :, j] = piece
        qconst[:, :, 3 + j] = piece
    dist = (np.arange(NSA_HPG * qb) % qb)[:, None] + WIN_PAD - np.arange(WIN_PAD + qb)[None, :]
    cw = np.where((dist >= 0) & (dist < WIN), -slope_rows[:, :, None] * dist[None].astype(np.float32),
                  -MASK_BIG).astype(np.float32)
    return (jnp.asarray(overlap.T), jnp.asarray(kconst, BF16), jnp.asarray(qconst, BF16),
            jnp.asarray(slope_rows[:, :, None]), jnp.asarray(cw))


def _nsa_attn(q, gates, kc, vc, kv):
    bsz, seq, _ = q.shape
    qb = NSA_STEP
    rows = NSA_HPG * qb
    gw = NSA_HPG * NSA_HD
    n_cmp_rows = seq // CMP_STRIDE
    n_slc = seq // SLC_LEN
    nw = WIN_PAD + qb
    ovl_t, kconst, qconst, slope_rows, cw = _nsa_constants(seq)
    cmp_spec = lambda: pl.BlockSpec((1, 1, n_cmp_rows, NSA_HD), lambda b, g, i: (b, g, 0, 0))
    kv_spec = lambda t: pl.BlockSpec((None, 1, 1, seq, NSA_HD), lambda b, g, i, t=t: (t, b, g, 0, 0))
    return pl.pallas_call(
        _nsa_attn_kernel,
        grid=(bsz, NSA_GROUPS, seq // qb),
        in_specs=[pl.BlockSpec((1, qb, gw), lambda b, g, i: (b, i, g)),
                  pl.BlockSpec((1, 1, qb, 3 * NSA_HPG), lambda b, g, i: (b, g, i, 0)),
                  pl.BlockSpec((1, rows, 128), lambda b, g, i: (g, 0, 0)),
                  pl.BlockSpec((1, rows, 1), lambda b, g, i: (g, 0, 0)),
                  pl.BlockSpec((n_slc, n_cmp_rows), lambda b, g, i: (0, 0)),
                  pl.BlockSpec((seq, 256), lambda b, g, i: (0, 0)),
                  pl.BlockSpec((1, rows, nw), lambda b, g, i: (g, 0, 0)),
                  cmp_spec(), cmp_spec(), kv_spec(0), kv_spec(1), kv_spec(2), kv_spec(3)],
        out_specs=pl.BlockSpec((1, qb, gw), lambda b, g, i: (b, i, g)),
        out_shape=jax.ShapeDtypeStruct((bsz, seq, NSA_HEADS * NSA_HD), BF16),
        scratch_shapes=[pltpu.VMEM((seq, 256), BF16), pltpu.VMEM((seq, 128), BF16),
                        pltpu.VMEM((WIN_PAD + seq, 128), BF16), pltpu.VMEM((WIN_PAD + seq, 128), BF16),
                        pltpu.VMEM((n_slc, qb), F32),
                        pltpu.VMEM((rows, n_cmp_rows), F32), pltpu.VMEM((rows, n_cmp_rows), BF16),
                        pltpu.VMEM((qb, n_cmp_rows), F32), pltpu.VMEM((rows, 256), BF16),
                        pltpu.VMEM((rows, 1), F32), pltpu.VMEM((rows, 128), F32),
                        pltpu.VMEM((rows, SLC_KEY_TILE), F32), pltpu.VMEM((rows, SLC_KEY_TILE), F32),
                        pltpu.VMEM((rows, SLC_KEY_TILE), BF16),
                        pltpu.VMEM((rows, nw), F32), pltpu.VMEM((rows, nw), BF16),
                        pltpu.VMEM((rows, NSA_HD), F32)],
        compiler_params=_params("parallel", "parallel", "arbitrary"),
        name="nsa_attn",
    )(q, gates, qconst, slope_rows, ovl_t, kconst, cw, kc, vc, kv, kv, kv, kv)


def _nsa_mixer(x, norm_g, shift, scale, shared, w_in, b_gate):
    kcv, kv = shared
    bsz, seq, _ = x.shape
    q, gates = _nsa_proj(x, norm_g, shift, scale, w_in, b_gate)
    gates = gates.reshape(bsz, seq, 3, NSA_GROUPS, NSA_HPG).transpose(0, 3, 1, 2, 4)
    gates = gates.reshape(bsz, NSA_GROUPS, seq, 3 * NSA_HPG)
    return _nsa_attn(q, gates, kcv[0], kcv[1], kv)


def _final_norm_kernel(x_ref, g_ref, o_ref):
    x = x_ref[...]
    o_ref[...] = x * lax.rsqrt(jnp.mean(x * x, axis=-1, keepdims=True) + RMS_EPS) * g_ref[...]


def _final_norm(x, g):
    bsz, seq, d = x.shape
    x2d = x.reshape(bsz * seq, d)
    tm = min(ROW_TILE, seq)
    out = pl.pallas_call(
        _final_norm_kernel,
        grid=(x2d.shape[0] // tm,),
        in_specs=[pl.BlockSpec((tm, d), lambda i: (i, 0)), pl.BlockSpec((1, d), lambda i: (0, 0))],
        out_specs=pl.BlockSpec((tm, d), lambda i: (i, 0)),
        out_shape=jax.ShapeDtypeStruct(x2d.shape, F32),
        compiler_params=_params("parallel"),
        name="final_norm",
    )(x2d, g.reshape(1, d))
    return out.reshape(bsz, seq, d)


def kernel(x, c, ada_w, ada_b, norm1_g, norm2_g, gla_w_in, gla_w_gate2, gla_b_gate, gla_norm_g, gla_w_out, kv_norm_g, kv_ada_w, kv_ada_b, kv_w, cmp_pos, cmp_w1, cmp_b1, cmp_w2, cmp_b2, nsa_w_in, nsa_b_gate, nsa_w_out, router_w, router_b, moe_w_gate_up, moe_b_gate_up, moe_w_down, moe_b_down, final_g):
    bsz, seq, d = x.shape
    mod = _ada_vectors(c, ada_w, ada_b)
    kv_mod = _ada_vectors(c, kv_ada_w[None], kv_ada_b[None])[0]
    vec = lambda m, j: m[:, None, j * d:(j + 1) * d]
    shared = None
    for layer in range(DEPTH):
        m = mod[layer]
        sh1, sc1, g1, sh2, sc2, g2 = (vec(m, j) for j in range(6))
        if layer < N_A_LAYERS:
            i = layer
            q, k, v, r, la = _gla_proj(x, norm1_g[layer], sh1, sc1, gla_w_in[i], gla_w_gate2[i], gla_b_gate[i])
            o = _gla_core(q, k, v, r, la, gla_norm_g[i])
            x = _res_matmul(o, gla_w_out[i], x, g1)
        else:
            i = layer - N_A_LAYERS
            o = _nsa_mixer(x, norm1_g[layer], sh1, sc1, shared, nsa_w_in[i], nsa_b_gate[i])
            x = _res_matmul(o, nsa_w_out[i], x, g1)
        x = _moe_layer(x, norm2_g[layer], sh2, sc2, g2, router_w[layer], router_b[layer],
                       layer, moe_w_gate_up, moe_b_gate_up[layer], moe_w_down, moe_b_down[layer])
        if layer == N_A_LAYERS - 1:
            kv_cmp, kv = _kv_proj(x, kv_norm_g, vec(kv_mod, 0), vec(kv_mod, 1), kv_w)
            shared = (_compress(kv_cmp, cmp_pos, cmp_w1, cmp_b1, cmp_w2, cmp_b2), kv)
    return _final_norm(x, final_g)
```

```python
import functools

import numpy as np
import jax
import jax.numpy as jnp
from jax import lax
from jax.experimental import pallas as pl
from jax.experimental.pallas import tpu as pltpu

F32 = jnp.float32
BF16 = jnp.bfloat16
HIGHEST = lax.Precision.HIGHEST

D_MODEL = 1024
DEPTH = 4
N_A_LAYERS = DEPTH // 2
RMS_EPS = 1e-5

GLA_HEADS = 4
GLA_KEY_DIM = D_MODEL // 2
GLA_VAL_DIM = D_MODEL
GLA_HK = GLA_KEY_DIM // GLA_HEADS
GLA_HV = GLA_VAL_DIM // GLA_HEADS
GLA_RANK = 16
GLA_TAU = 16.0
GLA_CHUNK = 64

NSA_HEADS = 16
NSA_GROUPS = 4
NSA_HPG = NSA_HEADS // NSA_GROUPS
NSA_HD = D_MODEL // NSA_HEADS
CMP_LEN = 32
CMP_STRIDE = 16
CMP_HIDDEN = 2 * NSA_HD
SLC_LEN = 64
SLC_TOPK = 16
WIN = 512

N_EXPERTS = 32
TOP_K = 4
EXPERT_FF = D_MODEL
SWIGLU_LIMIT = 7.0
SWIGLU_ALPHA = 1.702

FORCE = 1e4
NEG = -1e30

VMEM_LIMIT_BYTES = 56 * 1024 * 1024

ROW_TILE = 512
EXPERT_TILE = 256
LANES = 128
ROW_CHUNK = 8
ROUTE_SPARE_ROWS = 96
ROUTE_TILE = 512
GLA_STEP = 512
SLC_KEY_TILE = 512
NSA_STEP = 256
NSA_ROW_CHUNK = 64
RANK_UNROLL = 4
WIN_PAD = WIN
MASK_BIG = 1e30


def _params(*sem):
    return pltpu.CompilerParams(dimension_semantics=sem, vmem_limit_bytes=VMEM_LIMIT_BYTES)


def _norm_mod(x, g, shift, scale):
    y = x * lax.rsqrt(jnp.mean(x * x, axis=-1, keepdims=True) + RMS_EPS)
    return (y * g) * (1.0 + scale) + shift


def _dot(a, b):
    return jnp.dot(a, b, preferred_element_type=F32)


def _dot_nt(a, b, precision=None):
    return lax.dot_general(a, b, (((1,), (1,)), ((), ())), precision=precision,
                           preferred_element_type=F32)


def _ada_kernel(c_ref, w_ref, b_ref, o_ref):
    c = c_ref[...]
    cs = c * jax.nn.sigmoid(c)
    o_ref[0] = jnp.dot(cs, w_ref[0], precision=HIGHEST, preferred_element_type=F32) + b_ref[0]


def _ada_vectors(c, w, b):
    n_l, d, m = w.shape
    bsz = c.shape[0]
    tn = 1024
    return pl.pallas_call(
        _ada_kernel,
        grid=(n_l, m // tn),
        in_specs=[
            pl.BlockSpec((bsz, d), lambda l, j: (0, 0)),
            pl.BlockSpec((1, d, tn), lambda l, j: (l, 0, j)),
            pl.BlockSpec((1, 1, tn), lambda l, j: (l, 0, j)),
        ],
        out_specs=pl.BlockSpec((1, bsz, tn), lambda l, j: (l, 0, j)),
        out_shape=jax.ShapeDtypeStruct((n_l, bsz, m), F32),
        compiler_params=_params("parallel", "parallel"),
        name="ada_vectors",
    )(c, w, b.reshape(n_l, 1, m))


def _gla_proj_kernel(x_ref, g_ref, sh_ref, sc_ref, wq_ref, wk_ref, wv_ref, wr_ref, wlr_ref,
                     wg2_ref, bg_ref, q_ref, k_ref, v_ref, r_ref, la_ref):
    h = _norm_mod(x_ref[0], g_ref[...], sh_ref[0], sc_ref[0]).astype(BF16)
    q_ref[0] = _dot(h, wq_ref[...]).astype(BF16)
    k_ref[0] = _dot(h, wk_ref[...]).astype(BF16)
    v_ref[0] = _dot(h, wv_ref[...]).astype(BF16)
    r_ref[0] = _dot(h, wr_ref[...]).astype(BF16)
    g_lr = _dot(h, wlr_ref[...])
    z = jnp.dot(g_lr, wg2_ref[...], precision=HIGHEST, preferred_element_type=F32) + bg_ref[...]
    log_sig = jnp.minimum(z, 0.0) - jnp.log(1.0 + jnp.exp(-jnp.abs(z)))
    la_ref[0] = log_sig / GLA_TAU


def _gla_proj(x, g, shift, scale, w_in, w_gate2, b_gate):
    bsz, seq, d = x.shape
    tm = min(ROW_TILE, seq)
    kd, vd = GLA_KEY_DIM, GLA_VAL_DIM
    wq = w_in[:, :kd].astype(BF16)
    wk = w_in[:, kd:2 * kd].astype(BF16)
    wv = w_in[:, 2 * kd:2 * kd + vd].astype(BF16)
    wr = w_in[:, 2 * kd + vd:2 * kd + 2 * vd].astype(BF16)
    wlr = w_in[:, 2 * kd + 2 * vd:].astype(BF16)
    full = lambda shape: pl.BlockSpec(shape, lambda b, i: (0,) * len(shape))
    row = lambda n: pl.BlockSpec((1, tm, n), lambda b, i: (b, i, 0))
    vec = pl.BlockSpec((1, 1, d), lambda b, i: (b, 0, 0))
    return pl.pallas_call(
        _gla_proj_kernel,
        grid=(bsz, seq // tm),
        in_specs=[row(d), full((1, d)), vec, vec, full((d, kd)), full((d, kd)), full((d, vd)),
                  full((d, vd)), full((d, GLA_RANK)), full((GLA_RANK, kd)), full((1, kd))],
        out_specs=[row(kd), row(kd), row(vd), row(vd), row(kd)],
        out_shape=[jax.ShapeDtypeStruct((bsz, seq, kd), BF16),
                   jax.ShapeDtypeStruct((bsz, seq, kd), BF16),
                   jax.ShapeDtypeStruct((bsz, seq, vd), BF16),
                   jax.ShapeDtypeStruct((bsz, seq, vd), BF16),
                   jax.ShapeDtypeStruct((bsz, seq, kd), F32)],
        compiler_params=_params("parallel", "parallel"),
        name="gla_proj",
    )(x, g.reshape(1, d), shift, scale, wq, wk, wv, wr, wlr, w_gate2, b_gate.reshape(1, kd))


def _gla_core_kernel(q_ref, k_ref, v_ref, r_ref, la_ref, ng_ref, o_ref, state_ref, *, n_chunks):
    @pl.when(pl.program_id(1) == 0)
    def _():
        state_ref[...] = jnp.zeros_like(state_ref)

    c_len = GLA_CHUNK
    row = lax.broadcasted_iota(jnp.int32, (c_len, c_len), 0)
    col = lax.broadcasted_iota(jnp.int32, (c_len, c_len), 1)
    causal = col <= row
    row_k = lax.broadcasted_iota(jnp.int32, (c_len, GLA_HK), 0)
    shifts = [1 << j for j in range(c_len.bit_length() - 1)]

    def chunk(c, carry):
        c0 = pl.multiple_of(c * c_len, c_len)
        for hh in range(GLA_HEADS):
            kc = slice(hh * GLA_HK, (hh + 1) * GLA_HK)
            vc = slice(hh * GLA_HV, (hh + 1) * GLA_HV)
            b = la_ref[0, pl.ds(c0, c_len), kc]
            for shift in shifts:
                b = b + jnp.where(row_k >= shift, pltpu.roll(b, shift, 0), 0.0)
            q = q_ref[0, pl.ds(c0, c_len), kc].astype(F32) * (GLA_HK ** -0.5)
            k = k_ref[0, pl.ds(c0, c_len), kc].astype(F32)
            v = v_ref[0, pl.ds(c0, c_len), vc]
            q_dec = (q * jnp.exp(b)).astype(BF16)
            k_intra = (k * jnp.exp(-b)).astype(BF16)
            b_t = b.T
            bl_t = b_t[:, c_len - 1:c_len]
            k_inter_t = (k.T * jnp.exp(bl_t - b_t)).astype(BF16)
            att = jnp.where(causal, _dot_nt(q_dec, k_intra), 0.0).astype(BF16)
            state = state_ref[hh]
            o = _dot(att, v) + _dot(q_dec, state.astype(BF16))
            state_ref[hh] = state * jnp.exp(bl_t) + _dot(k_inter_t, v)
            o = o * lax.rsqrt(jnp.mean(o * o, axis=-1, keepdims=True) + RMS_EPS)
            r = r_ref[0, pl.ds(c0, c_len), vc].astype(F32)
            o_ref[0, pl.ds(c0, c_len), vc] = ((o * ng_ref[:, vc]) * (r * jax.nn.sigmoid(r))).astype(BF16)
        return carry

    lax.fori_loop(0, n_chunks, chunk, 0, unroll=4)


def _gla_core(q, k, v, r, la, norm_g):
    bsz, seq, _ = q.shape
    ts = min(GLA_STEP, seq)
    kern = functools.partial(_gla_core_kernel, n_chunks=ts // GLA_CHUNK)
    kd = lambda: pl.BlockSpec((1, ts, GLA_KEY_DIM), lambda b, s: (b, s, 0))
    vd = lambda: pl.BlockSpec((1, ts, GLA_VAL_DIM), lambda b, s: (b, s, 0))
    return pl.pallas_call(
        kern,
        grid=(bsz, seq // ts),
        in_specs=[kd(), kd(), vd(), vd(), kd(), pl.BlockSpec((1, GLA_VAL_DIM), lambda b, s: (0, 0))],
        out_specs=vd(),
        out_shape=jax.ShapeDtypeStruct((bsz, seq, GLA_VAL_DIM), BF16),
        scratch_shapes=[pltpu.VMEM((GLA_HEADS, GLA_HK, GLA_HV), F32)],
        compiler_params=_params("parallel", "arbitrary"),
        name="gla_core",
    )(q, k, v, r, la, norm_g.reshape(1, GLA_VAL_DIM))


def _res_matmul_kernel(a_ref, w_ref, x_ref, gate_ref, o_ref):
    o_ref[0] = x_ref[0] + gate_ref[0] * _dot(a_ref[0], w_ref[...])


def _res_matmul(a, w, x, gate):
    bsz, seq, d = x.shape
    kdim = a.shape[-1]
    tm = min(ROW_TILE, seq)
    return pl.pallas_call(
        _res_matmul_kernel,
        grid=(bsz, seq // tm),
        in_specs=[pl.BlockSpec((1, tm, kdim), lambda b, i: (b, i, 0)),
                  pl.BlockSpec((kdim, d), lambda b, i: (0, 0)),
                  pl.BlockSpec((1, tm, d), lambda b, i: (b, i, 0)),
                  pl.BlockSpec((1, 1, d), lambda b, i: (b, 0, 0))],
        out_specs=pl.BlockSpec((1, tm, d), lambda b, i: (b, i, 0)),
        out_shape=jax.ShapeDtypeStruct((bsz, seq, d), F32),
        compiler_params=_params("parallel", "parallel"),
        name="res_matmul",
    )(a, w.astype(BF16), x, gate)


def _route_kernel(x_ref, g_ref, sh_ref, sc_ref, rwt_ref, rb_ref, xs_ref, w_ref, slot_ref, cnt_ref):
    tm = x_ref.shape[0]
    n_slot = xs_ref.shape[0]
    h = _norm_mod(x_ref[...], g_ref[...], sh_ref[0], sc_ref[0])
    logits = _dot_nt(rwt_ref[...], h, precision=HIGHEST) + rb_ref[...]
    e_iota = lax.broadcasted_iota(jnp.int32, logits.shape, 0)
    vals, hots = [], []
    for _ in range(TOP_K):
        m = jnp.max(logits, axis=0, keepdims=True)
        idx = jnp.min(jnp.where(logits == m, e_iota, N_EXPERTS), axis=0, keepdims=True)
        hot = e_iota == idx
        vals.append(m)
        hots.append(hot)
        logits = jnp.where(hot, -jnp.inf, logits)
    exps = [jnp.exp(v - vals[0]) for v in vals]
    denom = exps[0] + exps[1] + exps[2] + exps[3]
    for kk in range(TOP_K):
        w_ref[kk:kk + 1, :] = exps[kk] / denom

    sel = (hots[0] | hots[1] | hots[2] | hots[3]).astype(F32)
    s_iota = lax.broadcasted_iota(jnp.int32, (tm, tm), 0)
    t_iota = lax.broadcasted_iota(jnp.int32, (tm, tm), 1)
    before = (s_iota < t_iota).astype(BF16)
    prefix = _dot(sel.astype(BF16), before)
    cnt = jnp.sum(sel, axis=1, keepdims=True)
    chunks = jnp.floor((cnt + (ROW_CHUNK - 1.0)) * (1.0 / ROW_CHUNK))
    lower = (lax.broadcasted_iota(jnp.int32, (N_EXPERTS, N_EXPERTS), 0) >
             lax.broadcasted_iota(jnp.int32, (N_EXPERTS, N_EXPERTS), 1)).astype(BF16)
    seg0 = _dot(lower, jnp.broadcast_to(chunks, (N_EXPERTS, LANES)).astype(BF16))[:, 0:1] * ROW_CHUNK
    slots = [jnp.sum(jnp.where(hot, prefix + seg0, 0.0), axis=0, keepdims=True).astype(jnp.int32)
             for hot in hots]
    r_iota = lax.broadcasted_iota(jnp.int32, (n_slot, tm), 0)
    q = jnp.zeros((n_slot, tm), F32)
    for kk in range(TOP_K):
        q = q + jnp.where(r_iota == slots[kk], 1.0, 0.0)
    for kk in range(TOP_K):
        slot_ref[kk:kk + 1, :] = slots[kk]
    xs_ref[...] = _dot((q > 0.0).astype(BF16), h.astype(BF16))

    cnt_ref[0] = jnp.broadcast_to(cnt, (N_EXPERTS, LANES)).astype(jnp.int32)


def _route(x2d, g, shift, scale, router_w, router_b, seq):
    n_tok, d = x2d.shape
    tm = min(ROUTE_TILE, seq)
    per_b = seq // tm
    n_win = n_tok // tm
    n_slot = tm * TOP_K + N_EXPERTS * ROW_CHUNK + ROUTE_SPARE_ROWS
    vec = pl.BlockSpec((1, 1, d), lambda i: (i // per_b, 0, 0))
    return pl.pallas_call(
        _route_kernel,
        grid=(n_win,),
        in_specs=[pl.BlockSpec((tm, d), lambda i: (i, 0)),
                  pl.BlockSpec((1, d), lambda i: (0, 0)), vec, vec,
                  pl.BlockSpec((N_EXPERTS, d), lambda i: (0, 0)),
                  pl.BlockSpec((N_EXPERTS, 1), lambda i: (0, 0))],
        out_specs=[pl.BlockSpec((n_slot, d), lambda i: (i, 0)),
                   pl.BlockSpec((TOP_K, tm), lambda i: (0, i)),
                   pl.BlockSpec((TOP_K, tm), lambda i: (0, i)),
                   pl.BlockSpec((1, N_EXPERTS, LANES), lambda i: (i, 0, 0))],
        out_shape=[jax.ShapeDtypeStruct((n_win * n_slot, d), F32),
                   jax.ShapeDtypeStruct((TOP_K, n_tok), F32),
                   jax.ShapeDtypeStruct((TOP_K, n_tok), jnp.int32),
                   jax.ShapeDtypeStruct((n_win, N_EXPERTS, LANES), jnp.int32)],
        compiler_params=_params("parallel"),
        name="moe_route",
    )(x2d, g.reshape(1, d), shift, scale, router_w.T, router_b.reshape(N_EXPERTS, 1))


def _expert_kernel(te_ref, nu_ref, src_hbm, dst_hbm, xs_in, wgu_ref, wd_ref, bg_ref, bu_ref, bd_ref, xy_hbm,
                   wg_s, wu_s, wd_s, tr_s, xbuf, ybuf, src_s, dst_s, src_sem, dst_sem, g_sem, s_sem):
    del xs_in
    i = pl.program_id(0)
    n_tiles = pl.num_programs(0)
    tm = xbuf.shape[1]
    used = i < nu_ref[0]
    fresh = (i == 0) | (te_ref[i] != te_ref[jnp.maximum(i - 1, 0)])
    slot = i % 2
    other = 1 - slot

    def src_copy(t, sl):
        return pltpu.make_async_copy(src_hbm.at[t], src_s.at[sl], src_sem.at[sl])

    def dst_copy(t, sl):
        return pltpu.make_async_copy(dst_hbm.at[t], dst_s.at[sl], dst_sem.at[sl])

    def gather_start(sl):
        for c in range(tm // ROW_CHUNK):
            src = pl.multiple_of(src_s[sl, c] * ROW_CHUNK, ROW_CHUNK)
            pltpu.make_async_copy(xy_hbm.at[pl.ds(src, ROW_CHUNK), :],
                                  xbuf.at[sl, pl.ds(c * ROW_CHUNK, ROW_CHUNK), :], g_sem.at[sl]).start()

    def gather_wait(sl):
        pltpu.make_async_copy(xy_hbm.at[pl.ds(0, tm), :], xbuf.at[sl], g_sem.at[sl]).wait()

    def scatter_start(sl):
        for c in range(tm // ROW_CHUNK):
            dst = pl.multiple_of(dst_s[sl, c] * ROW_CHUNK, ROW_CHUNK)
            pltpu.make_async_copy(ybuf.at[sl, pl.ds(c * ROW_CHUNK, ROW_CHUNK), :],
                                  xy_hbm.at[pl.ds(dst, ROW_CHUNK), :], s_sem.at[sl]).start()

    def scatter_wait(sl):
        pltpu.make_async_copy(ybuf.at[sl], xy_hbm.at[pl.ds(0, tm), :], s_sem.at[sl]).wait()

    @pl.when(i == 0)
    def _():
        src_copy(0, 0).start()
        dst_copy(0, 0).start()
        src_copy(0, 0).wait()
        gather_start(0)
        src_copy(jnp.minimum(1, n_tiles - 1), 1).start()

    @pl.when(used & fresh)
    def _():
        n_slab, chunk, lanes = tr_s.shape
        half = chunk // 2
        for c in range(wgu_ref.shape[2] // chunk):
            t = wgu_ref[0, :, c * chunk:(c + 1) * chunk].T
            for j in range(n_slab):
                tr_s[j] = t[:, j * lanes:(j + 1) * lanes]
            for j in range(n_slab):
                wg_s[j * lanes:(j + 1) * lanes, c * half:(c + 1) * half] = (
                    tr_s[j, pl.ds(0, half, stride=2), :].T.astype(BF16))
                wu_s[j * lanes:(j + 1) * lanes, c * half:(c + 1) * half] = (
                    tr_s[j, pl.ds(1, half, stride=2), :].T.astype(BF16))
        wd_s[...] = wd_ref[0].astype(BF16)

    @pl.when(used)
    def _():
        nxt = jnp.minimum(i + 1, n_tiles - 1)
        src_copy(nxt, other).wait()
        gather_start(other)
        src_copy(jnp.minimum(i + 2, n_tiles - 1), slot).start()
        gather_wait(slot)
        xb = xbuf[slot].astype(BF16)
        gate = jnp.minimum(_dot(xb, wg_s[...]) + bg_ref[0], SWIGLU_LIMIT)
        up = jnp.clip(_dot(xb, wu_s[...]) + bu_ref[0], -SWIGLU_LIMIT, SWIGLU_LIMIT)
        act = (up + 1.0) * gate * jax.nn.sigmoid(SWIGLU_ALPHA * gate)
        ybuf[slot] = _dot(act.astype(BF16), wd_s[...]) + bd_ref[0]
        dst_copy(i, slot).wait()
        scatter_start(slot)
        dst_copy(nxt, other).start()

    @pl.when(used & (i > 0))
    def _():
        scatter_wait(other)

    @pl.when(i == nu_ref[0] - 1)
    def _():
        gather_wait(other)
        scatter_wait(slot)
        src_copy(0, slot).wait()
        dst_copy(0, other).wait()


def _experts(xs, src_tiles, dst_tiles, tile_expert, n_used, layer, w_gu, w_down, bg, bu, bd):
    d = xs.shape[1]
    n_tiles = src_tiles.shape[0]
    tm = EXPERT_TILE
    ff = w_down.shape[2]
    wsel = lambda i, te, nu: (te[i], 0, 0)
    lsel = lambda i, te, nu: (layer, te[i], 0, 0)
    hbm = pl.BlockSpec(memory_space=pl.ANY)
    return pl.pallas_call(
        _expert_kernel,
        grid_spec=pltpu.PrefetchScalarGridSpec(
            num_scalar_prefetch=2,
            grid=(n_tiles,),
            in_specs=[hbm, hbm, hbm,
                      pl.BlockSpec((None, 1, d, 2 * ff), lsel),
                      pl.BlockSpec((None, 1, ff, d), lsel),
                      pl.BlockSpec((1, 1, ff), wsel), pl.BlockSpec((1, 1, ff), wsel),
                      pl.BlockSpec((1, 1, d), wsel)],
            out_specs=hbm,
            scratch_shapes=[pltpu.VMEM((d, ff), BF16), pltpu.VMEM((d, ff), BF16),
                            pltpu.VMEM((ff, d), BF16), pltpu.VMEM((d // LANES, 2 * LANES, LANES), F32),
                            pltpu.VMEM((2, tm, d), F32), pltpu.VMEM((2, tm, d), F32),
                            pltpu.SMEM((2, tm // ROW_CHUNK), jnp.int32),
                            pltpu.SMEM((2, tm // ROW_CHUNK), jnp.int32),
                            pltpu.SemaphoreType.DMA((2,)), pltpu.SemaphoreType.DMA((2,)),
                            pltpu.SemaphoreType.DMA((2,)), pltpu.SemaphoreType.DMA((2,))]),
        out_shape=jax.ShapeDtypeStruct(xs.shape, xs.dtype),
        input_output_aliases={4: 0},
        compiler_params=_params("arbitrary"),
        name="moe_experts",
    )(tile_expert, n_used, src_tiles, dst_tiles, xs, w_gu, w_down, bg, bu, bd)


def _combine_kernel(y_ref, slot_ref, w_ref, x_ref, gate_ref, o_ref):
    tm = x_ref.shape[0]
    n_slot = y_ref.shape[0]
    r_iota = lax.broadcasted_iota(jnp.int32, (tm, n_slot), 1)
    w = w_ref[...]
    slot = slot_ref[...]
    c = jnp.zeros((tm, n_slot), F32)
    for kk in range(TOP_K):
        c = c + jnp.where(r_iota == slot[:, kk:kk + 1], w[:, kk:kk + 1], 0.0)
    y = _dot(c.astype(BF16), y_ref[...].astype(BF16))
    o_ref[...] = x_ref[...] + gate_ref[0] * y


def _combine(y, slots, w_tok, x2d, gate, seq):
    n_tok, d = x2d.shape
    tm = min(ROUTE_TILE, seq)
    per_b = seq // tm
    n_win = n_tok // tm
    n_slot = y.shape[0] // n_win
    return pl.pallas_call(
        _combine_kernel,
        grid=(n_win,),
        in_specs=[pl.BlockSpec((n_slot, d), lambda i: (i, 0)),
                  pl.BlockSpec((tm, TOP_K), lambda i: (i, 0)),
                  pl.BlockSpec((tm, TOP_K), lambda i: (i, 0)),
                  pl.BlockSpec((tm, d), lambda i: (i, 0)),
                  pl.BlockSpec((1, 1, d), lambda i: (i // per_b, 0, 0))],
        out_specs=pl.BlockSpec((tm, d), lambda i: (i, 0)),
        out_shape=jax.ShapeDtypeStruct((n_tok, d), F32),
        compiler_params=_params("parallel"),
        name="moe_combine",
    )(y, slots, w_tok, x2d, gate)


def _moe_layer(x, norm_g, shift, scale, gate, router_w, router_b, layer, w_gu, b_gu, w_down, b_down):
    bsz, seq, d = x.shape
    n_tok = bsz * seq
    x2d = x.reshape(n_tok, d)
    xs, w_t, slot_t, cnt = _route(x2d, norm_g, shift, scale, router_w, router_b, seq)

    cnt = cnt[:, :, 0]
    n_win = cnt.shape[0]
    win_chunks = xs.shape[0] // n_win // ROW_CHUNK
    tile_chunks = EXPERT_TILE // ROW_CHUNK
    c8 = (cnt + ROW_CHUNK - 1) // ROW_CHUNK
    seg_off = jnp.cumsum(c8, axis=1) - c8
    cum_w = jnp.cumsum(c8, axis=0)
    total = cum_w[-1]
    padded = (total + tile_chunks - 1) // tile_chunks * tile_chunks
    ends = jnp.cumsum(padded)
    starts = ends - padded
    n_chunk = n_tok * TOP_K // ROW_CHUNK + n_win * N_EXPERTS + N_EXPERTS * tile_chunks
    n_tiles = n_chunk // tile_chunks
    tile0 = jnp.arange(n_tiles, dtype=jnp.int32) * tile_chunks
    tile_expert = jnp.minimum(jnp.sum((ends[None, :] <= tile0[:, None]).astype(jnp.int32), axis=1),
                              N_EXPERTS - 1)
    e_hot = (tile_expert[:, None] == jnp.arange(N_EXPERTS, dtype=jnp.int32)[None, :]).astype(jnp.int32)
    pick = lambda tab: jnp.sum(e_hot[:, :, None] * tab.T[None, :, :], axis=1)
    cum_t, c8_t, off_t = pick(cum_w)[:, None, :], pick(c8)[:, None, :], pick(seg_off)[:, None, :]
    lane = jnp.arange(tile_chunks, dtype=jnp.int32)[None, :]
    rank = (tile0 - jnp.sum(e_hot * starts[None, :], axis=1))[:, None] + lane
    valid = rank < jnp.sum(e_hot * total[None, :], axis=1)[:, None]
    w_of = jnp.minimum(jnp.sum((cum_t <= rank[:, :, None]).astype(jnp.int32), axis=2), n_win - 1)
    w_hot = (w_of[:, :, None] == jnp.arange(n_win, dtype=jnp.int32)[None, None, :]).astype(jnp.int32)
    seg_first = jnp.sum(w_hot * (cum_t - c8_t), axis=2)
    src = w_of * win_chunks + jnp.sum(w_hot * off_t, axis=2) + rank - seg_first
    spare0 = win_chunks - ROUTE_SPARE_ROWS // ROW_CHUNK
    u = (jnp.arange(n_tiles, dtype=jnp.int32) % 2)[:, None] * tile_chunks + lane
    dst = jnp.where(valid, src, (u % n_win) * win_chunks + spare0 + u // n_win)
    src = jnp.where(valid, src, win_chunks - 1)
    n_used = (ends[-1:] // tile_chunks).astype(jnp.int32)

    y = _experts(xs, src.reshape(n_tiles, tile_chunks), dst.reshape(n_tiles, tile_chunks), tile_expert,
                 n_used, layer, w_gu, w_down, b_gu[:, None, 0::2], b_gu[:, None, 1::2], b_down[:, None, :])
    out = _combine(y, slot_t.T, w_t.T, x2d, gate, seq)
    return out.reshape(bsz, seq, d)


def _kv_proj_kernel(x_ref, g_ref, sh_ref, sc_ref, w_ref, cmp_ref, kv_ref):
    h = _norm_mod(x_ref[0], g_ref[...], sh_ref[0], sc_ref[0]).astype(BF16)
    kv = _dot(h, w_ref[...])
    for t in range(6):
        for g in range(NSA_GROUPS):
            c0 = (t * NSA_GROUPS + g) * NSA_HD
            piece = kv[:, c0:c0 + NSA_HD]
            if t < 2:
                cmp_ref[t, 0, g] = piece
            else:
                kv_ref[t - 2, 0, g] = piece.astype(BF16)


def _kv_proj(x, g, shift, scale, kv_w):
    bsz, seq, d = x.shape
    tm = min(ROW_TILE, seq)
    n_out = kv_w.shape[1]
    vec = pl.BlockSpec((1, 1, d), lambda b, i: (b, 0, 0))
    return pl.pallas_call(
        _kv_proj_kernel,
        grid=(bsz, seq // tm),
        in_specs=[pl.BlockSpec((1, tm, d), lambda b, i: (b, i, 0)),
                  pl.BlockSpec((1, d), lambda b, i: (0, 0)), vec, vec,
                  pl.BlockSpec((d, n_out), lambda b, i: (0, 0))],
        out_specs=[pl.BlockSpec((2, 1, NSA_GROUPS, tm, NSA_HD), lambda b, i: (0, b, 0, i, 0)),
                   pl.BlockSpec((4, 1, NSA_GROUPS, tm, NSA_HD), lambda b, i: (0, b, 0, i, 0))],
        out_shape=[jax.ShapeDtypeStruct((2, bsz, NSA_GROUPS, seq, NSA_HD), F32),
                   jax.ShapeDtypeStruct((4, bsz, NSA_GROUPS, seq, NSA_HD), BF16)],
        compiler_params=_params("parallel", "parallel"),
        name="nsa_kv_proj",
    )(x, g.reshape(1, d), shift, scale, kv_w.astype(BF16))


def _compress_kernel(x_ref, pos_ref, w1_ref, b1_ref, w2_ref, b2_ref, o_ref, *, n_rows):
    for g in range(NSA_GROUPS):
        first = jnp.zeros((n_rows, CMP_HIDDEN), F32)
        second = jnp.zeros((n_rows, CMP_HIDDEN), F32)
        for l in range(CMP_STRIDE):
            rows = x_ref[0, 0, g, pl.ds(l, n_rows, stride=CMP_STRIDE), :]
            first = first + _dot((rows + pos_ref[0, l:l + 1, :]).astype(BF16), w1_ref[0, l])
            second = second + _dot((rows + pos_ref[0, CMP_STRIDE + l:CMP_STRIDE + l + 1, :]).astype(BF16),
                                   w1_ref[0, CMP_STRIDE + l])
        pre = first + pltpu.roll(second, n_rows - 1, 0) + b1_ref[0]
        hid = 0.5 * pre * (1.0 + jnp.tanh(0.7978845608028654 * (pre + 0.044715 * pre * pre * pre)))
        o_ref[0, 0, g] = (_dot(hid.astype(BF16), w2_ref[0]) + b2_ref[0]).astype(BF16)


def _compress(kv_cmp, cmp_pos, cmp_w1, cmp_b1, cmp_w2, cmp_b2):
    _, bsz, _, seq, _ = kv_cmp.shape
    n_rows = seq // CMP_STRIDE
    kern = functools.partial(_compress_kernel, n_rows=n_rows)
    w1 = cmp_w1.reshape(2, CMP_LEN, NSA_HD, CMP_HIDDEN).astype(BF16)
    return pl.pallas_call(
        kern,
        grid=(2, bsz),
        in_specs=[pl.BlockSpec((1, 1, NSA_GROUPS, seq, NSA_HD), lambda t, b: (t, b, 0, 0, 0)),
                  pl.BlockSpec((1, CMP_LEN, NSA_HD), lambda t, b: (t, 0, 0)),
                  pl.BlockSpec((1, CMP_LEN, NSA_HD, CMP_HIDDEN), lambda t, b: (t, 0, 0, 0)),
                  pl.BlockSpec((1, 1, CMP_HIDDEN), lambda t, b: (t, 0, 0)),
                  pl.BlockSpec((1, CMP_HIDDEN, NSA_HD), lambda t, b: (t, 0, 0)),
                  pl.BlockSpec((1, 1, NSA_HD), lambda t, b: (t, 0, 0))],
        out_specs=pl.BlockSpec((1, 1, NSA_GROUPS, n_rows, NSA_HD), lambda t, b: (t, b, 0, 0, 0)),
        out_shape=jax.ShapeDtypeStruct((2, bsz, NSA_GROUPS, n_rows, NSA_HD), BF16),
        compiler_params=_params("parallel", "parallel"),
        name="nsa_compress",
    )(kv_cmp, cmp_pos, w1, cmp_b1[:, None, :], cmp_w2.astype(BF16), cmp_b2[:, None, :])


def _nsa_proj_kernel(x_ref, g_ref, sh_ref, sc_ref, wq_ref, wg_ref, bg_ref, q_ref, gate_ref):
    h = _norm_mod(x_ref[0], g_ref[...], sh_ref[0], sc_ref[0]).astype(BF16)
    q_ref[0] = (_dot(h, wq_ref[...]) * (NSA_HD ** -0.5)).astype(BF16)
    gate_ref[0] = jax.nn.sigmoid(_dot(h, wg_ref[...]) + bg_ref[...])


def _nsa_proj(x, g, shift, scale, w_in, b_gate):
    bsz, seq, d = x.shape
    tm = min(ROW_TILE, seq)
    nq = NSA_HEADS * NSA_HD
    ng = 3 * NSA_HEADS
    vec = pl.BlockSpec((1, 1, d), lambda b, i: (b, 0, 0))
    return pl.pallas_call(
        _nsa_proj_kernel,
        grid=(bsz, seq // tm),
        in_specs=[pl.BlockSpec((1, tm, d), lambda b, i: (b, i, 0)),
                  pl.BlockSpec((1, d), lambda b, i: (0, 0)), vec, vec,
                  pl.BlockSpec((d, nq), lambda b, i: (0, 0)),
                  pl.BlockSpec((d, ng), lambda b, i: (0, 0)),
                  pl.BlockSpec((1, ng), lambda b, i: (0, 0))],
        out_specs=[pl.BlockSpec((1, tm, nq), lambda b, i: (b, i, 0)),
                   pl.BlockSpec((1, tm, ng), lambda b, i: (b, i, 0))],
        out_shape=[jax.ShapeDtypeStruct((bsz, seq, nq), BF16),
                   jax.ShapeDtypeStruct((bsz, seq, ng), F32)],
        compiler_params=_params("parallel", "parallel"),
        name="nsa_proj",
    )(x, g.reshape(1, d), shift, scale, w_in[:, :nq].astype(BF16), w_in[:, nq:].astype(BF16),
      b_gate.reshape(1, ng))


def _nsa_attn_kernel(q_ref, gate_ref, qc_ref, slope_ref, ovl_ref, kconst_ref, cw_ref, kc_ref, vc_ref,
                     ks_ref, vs_ref, kw_ref, vw_ref, o_ref, ksa, vsa, kwa, vwa, score_s,
                     sc_s, pc_s, psum_s, qa_s, m_s, acc_s, sa_s, sb_s, ps_s, sw_s, pw_s, ow_s):
    qi = pl.program_id(2)
    qs = NSA_STEP
    hd = NSA_HD
    rows = NSA_HPG * qs
    seq = ks_ref.shape[2]
    q0 = pl.multiple_of(qi * qs, qs)

    @pl.when(qi == 0)
    def _():
        ksa[...] = kconst_ref[...]
        ksa[:, 0:hd] = ks_ref[0, 0]
        one_col = (lax.broadcasted_iota(jnp.int32, (seq, 128), 1) == hd).astype(BF16)
        vsa[...] = one_col
        vsa[:, 0:hd] = vs_ref[0, 0]
        flag_col = (lax.broadcasted_iota(jnp.int32, (WIN_PAD, 128), 1) == hd).astype(BF16)
        kwa[0:WIN_PAD, :] = flag_col
        kwa[WIN_PAD:, :] = jnp.zeros((seq, 128), BF16)
        kwa[WIN_PAD:, 0:hd] = kw_ref[0, 0]
        vwa[0:WIN_PAD, :] = jnp.zeros((WIN_PAD, 128), BF16)
        vwa[WIN_PAD:, :] = one_col
        vwa[WIN_PAD:, 0:hd] = vw_ref[0, 0]
        qa_s[:, 2 * hd:] = qc_ref[0]

    qt = q_ref[0].astype(F32)
    q32 = jnp.concatenate([qt[:, hh * hd:(hh + 1) * hd] for hh in range(NSA_HPG)], axis=0)
    q = q32.astype(BF16)
    slope = slope_ref[0]
    t_row = q0 + lax.broadcasted_iota(jnp.int32, (rows, 1), 0) % qs

    rc = NSA_ROW_CHUNK

    nw = WIN_PAD + qs
    q_w = jnp.concatenate([q32, jnp.full((rows, hd), -MASK_BIG, F32)], axis=1).astype(BF16)
    sw_s[...] = _dot_nt(q_w, kwa[pl.ds(q0, nw), :])
    rw = rc // 2
    for c in range(rows // rw):
        r = slice(c * rw, (c + 1) * rw)
        s_w = sw_s[r, :] + cw_ref[0, r, :]
        m_w = jnp.max(s_w, axis=-1, keepdims=True)
        pw_s[r, :] = jnp.exp(s_w - m_w).astype(BF16)
    acc_w = _dot(pw_s[...], vwa[pl.ds(q0, nw), :])
    ow_s[...] = acc_w[:, 0:hd] * (1.0 / acc_w[:, hd:hd + 1])

    n_cmp = kc_ref.shape[2]
    cmp_end = lax.broadcasted_iota(jnp.int32, (1, n_cmp), 1) * CMP_STRIDE + (CMP_LEN - 1)
    cmp_bias = cmp_end.astype(F32)
    sc_s[...] = _dot_nt(q, kc_ref[0, 0])
    for c in range(rows // rc):
        r = slice(c * rc, (c + 1) * rc)
        ok_c = cmp_end <= t_row[r]
        s_c = jnp.where(ok_c, sc_s[r, :] + slope[r] * cmp_bias, NEG)
        m_c = jnp.max(s_c, axis=-1, keepdims=True)
        e_c = jnp.where(ok_c, jnp.exp(s_c - m_c), 0.0)
        l_c = jnp.sum(e_c, axis=-1, keepdims=True)
        p_c = e_c * (1.0 / jnp.where(l_c > 0.0, l_c, 1.0))
        pc_s[r, :] = p_c.astype(BF16)
        qr = slice((c * rc) % qs, (c * rc) % qs + rc)
        if c * rc < qs:
            psum_s[qr, :] = p_c
        else:
            psum_s[qr, :] = psum_s[qr, :] + p_c
    o_c = _dot(pc_s[...], vc_ref[0, 0])

    imp_t = _dot_nt(ovl_ref[...], psum_s[...], precision=HIGHEST)
    n_slc = imp_t.shape[0]
    blk = lax.broadcasted_iota(jnp.int32, (n_slc, qs), 0)
    cur = (q0 + lax.broadcasted_iota(jnp.int32, (1, qs), 1)) // SLC_LEN
    forced = (blk == 0) | (blk == cur) | (blk == cur - 1)
    score = jnp.where(blk <= cur, imp_t + jnp.where(forced, FORCE, 0.0), -FORCE)
    score_s[...] = score

    def rank_step(ip, rank):
        for u in range(RANK_UNROLL):
            i = RANK_UNROLL * ip + u
            other = score_s[pl.ds(i, 1), :]
            ahead = (other > score) | ((other == score) & (blk > i))
            rank = rank + ahead.astype(F32)
        return rank

    last_blk = (q0 + qs - 1) // SLC_LEN
    n_rank = jnp.where(last_blk >= SLC_TOPK, last_blk // RANK_UNROLL + 1, 0)
    rank = lax.fori_loop(0, n_rank, rank_step, jnp.zeros((n_slc, qs), F32))
    sel_t = (rank < float(min(SLC_TOPK, n_slc))).astype(BF16)
    eye = (lax.broadcasted_iota(jnp.int32, (qs, qs), 0) ==
           lax.broadcasted_iota(jnp.int32, (qs, qs), 1)).astype(BF16)
    sel = _dot_nt(eye, sel_t)
    drop = (sel - 1.0) * MASK_BIG
    if n_slc < hd:
        drop = jnp.concatenate([drop, jnp.zeros((qs, hd - n_slc), F32)], axis=1)
    qa_s[:, 0:2 * hd] = jnp.concatenate([q32, jnp.concatenate([drop] * NSA_HPG, axis=0)], axis=1).astype(BF16)

    tk = SLC_KEY_TILE

    n_full = q0 // tk
    m_s[...] = jnp.full((rows, 1), -MASK_BIG, F32)
    acc_s[...] = jnp.zeros((rows, 128), F32)

    def scores(kt, s_out):
        k0 = pl.multiple_of(kt * tk, tk)
        s_out[...] = _dot_nt(qa_s[...], ksa[pl.ds(k0, tk), :])

    def absorb(s_in, kt, causal):
        k0 = pl.multiple_of(kt * tk, tk)
        kpos = k0 + lax.broadcasted_iota(jnp.int32, (1, tk), 1)
        for c in range(rows // rc):
            r = slice(c * rc, (c + 1) * rc)
            s = s_in[r, :]
            if causal:
                s = jnp.where(kpos <= t_row[r], s, -MASK_BIG)
            m_old = m_s[r, :]
            m_new = jnp.maximum(m_old, jnp.max(s, axis=-1, keepdims=True))
            m_s[r, :] = m_new
            ps_s[r, :] = jnp.exp(s - m_new).astype(BF16)
            acc_s[r, :] = jnp.exp(m_old - m_new) * acc_s[r, :]
        acc_s[...] = acc_s[...] + _dot(ps_s[...], vsa[pl.ds(k0, tk), :])

    scores(0, sa_s)

    def slc_pair(j, carry):
        scores(2 * j + 1, sb_s)
        absorb(sa_s, 2 * j, False)

        @pl.when(2 * j + 1 < n_full)
        def _():
            scores(2 * j + 2, sa_s)
            absorb(sb_s, 2 * j + 1, False)
        return carry

    lax.fori_loop(0, (n_full + 1) // 2, slc_pair, 0)

    @pl.when(n_full % 2 == 0)
    def _():
        absorb(sa_s, n_full, True)

    @pl.when(n_full % 2 == 1)
    def _():
        absorb(sb_s, n_full, True)

    o_s = acc_s[:, 0:hd] * (1.0 / acc_s[:, hd:hd + 1])

    gates = gate_ref[0, 0]
    outs = []
    for hh in range(NSA_HPG):
        r0 = hh * qs
        outs.append(gates[:, hh:hh + 1] * o_c[r0:r0 + qs]
                    + gates[:, NSA_HPG + hh:NSA_HPG + hh + 1] * o_s[r0:r0 + qs]
                    + gates[:, 2 * NSA_HPG + hh:2 * NSA_HPG + hh + 1] * ow_s[r0:r0 + qs, :])
    o_ref[0] = jnp.concatenate(outs, axis=1).astype(BF16)


def _bf16_pieces(x):
    x = np.asarray(x, np.float32)
    out = []
    for _ in range(3):
        p = x.astype(BF16).astype(np.float32)
        out.append(p)
        x = x - p
    return out


def _nsa_constants(seq):
    qb, hd = NSA_STEP, NSA_HD
    n_cmp_rows = seq // CMP_STRIDE
    n_slc = seq // SLC_LEN
    cmp_start = np.arange(n_cmp_rows) * CMP_STRIDE
    cmp_end = cmp_start + CMP_LEN - 1
    slc_start = np.arange(n_slc) * SLC_LEN
    overlap = ((cmp_start[:, None] <= slc_start[None, :] + SLC_LEN - 1)
               & (cmp_end[:, None] >= slc_start[None, :])).astype(np.float32)
    start = 2.0 ** (-8.0 / NSA_HEADS)
    slopes = np.asarray(start ** np.arange(1, NSA_HEADS + 1), np.float32).reshape(NSA_GROUPS, NSA_HPG)
    slope_rows = np.repeat(slopes, qb, axis=1)

    pos = np.arange(seq)
    kconst = np.zeros((seq, 256), np.float32)
    kconst[pos, hd + pos // SLC_LEN] = 1.0
    kconst[:, 2 * hd + 0:2 * hd + 3] = (pos // 64 * 64)[:, None]
    kconst[:, 2 * hd + 3:2 * hd + 6] = (pos % 64)[:, None]
    qconst = np.zeros((NSA_GROUPS, NSA_HPG * qb, 128), np.float32)
    for j, piece in enumerate(_bf16_pieces(slope_rows)):
        qconst[:, :, j] = piece
        qconst[:, :, 3 + j] = piece
    dist = (np.arange(NSA_HPG * qb) % qb)[:, None] + WIN_PAD - np.arange(WIN_PAD + qb)[None, :]
    cw = np.where((dist >= 0) & (dist < WIN), -slope_rows[:, :, None] * dist[None].astype(np.float32),
                  -MASK_BIG).astype(np.float32)
    return (jnp.asarray(overlap.T), jnp.asarray(kconst, BF16), jnp.asarray(qconst, BF16),
            jnp.asarray(slope_rows[:, :, None]), jnp.asarray(cw))


def _nsa_attn(q, gates, kc, vc, kv):
    bsz, seq, _ = q.shape
    qb = NSA_STEP
    rows = NSA_HPG * qb
    gw = NSA_HPG * NSA_HD
    n_cmp_rows = seq // CMP_STRIDE
    n_slc = seq // SLC_LEN
    nw = WIN_PAD + qb
    ovl_t, kconst, qconst, slope_rows, cw = _nsa_constants(seq)
    cmp_spec = lambda: pl.BlockSpec((1, 1, n_cmp_rows, NSA_HD), lambda b, g, i: (b, g, 0, 0))
    kv_spec = lambda t: pl.BlockSpec((None, 1, 1, seq, NSA_HD), lambda b, g, i, t=t: (t, b, g, 0, 0))
    return pl.pallas_call(
        _nsa_attn_kernel,
        grid=(bsz, NSA_GROUPS, seq // qb),
        in_specs=[pl.BlockSpec((1, qb, gw), lambda b, g, i: (b, i, g)),
                  pl.BlockSpec((1, 1, qb, 3 * NSA_HPG), lambda b, g, i: (b, g, i, 0)),
                  pl.BlockSpec((1, rows, 128), lambda b, g, i: (g, 0, 0)),
                  pl.BlockSpec((1, rows, 1), lambda b, g, i: (g, 0, 0)),
                  pl.BlockSpec((n_slc, n_cmp_rows), lambda b, g, i: (0, 0)),
                  pl.BlockSpec((seq, 256), lambda b, g, i: (0, 0)),
                  pl.BlockSpec((1, rows, nw), lambda b, g, i: (g, 0, 0)),
                  cmp_spec(), cmp_spec(), kv_spec(0), kv_spec(1), kv_spec(2), kv_spec(3)],
        out_specs=pl.BlockSpec((1, qb, gw), lambda b, g, i: (b, i, g)),
        out_shape=jax.ShapeDtypeStruct((bsz, seq, NSA_HEADS * NSA_HD), BF16),
        scratch_shapes=[pltpu.VMEM((seq, 256), BF16), pltpu.VMEM((seq, 128), BF16),
                        pltpu.VMEM((WIN_PAD + seq, 128), BF16), pltpu.VMEM((WIN_PAD + seq, 128), BF16),
                        pltpu.VMEM((n_slc, qb), F32),
                        pltpu.VMEM((rows, n_cmp_rows), F32), pltpu.VMEM((rows, n_cmp_rows), BF16),
                        pltpu.VMEM((qb, n_cmp_rows), F32), pltpu.VMEM((rows, 256), BF16),
                        pltpu.VMEM((rows, 1), F32), pltpu.VMEM((rows, 128), F32),
                        pltpu.VMEM((rows, SLC_KEY_TILE), F32), pltpu.VMEM((rows, SLC_KEY_TILE), F32),
                        pltpu.VMEM((rows, SLC_KEY_TILE), BF16),
                        pltpu.VMEM((rows, nw), F32), pltpu.VMEM((rows, nw), BF16),
                        pltpu.VMEM((rows, NSA_HD), F32)],
        compiler_params=_params("parallel", "parallel", "arbitrary"),
        name="nsa_attn",
    )(q, gates, qconst, slope_rows, ovl_t, kconst, cw, kc, vc, kv, kv, kv, kv)


def _nsa_mixer(x, norm_g, shift, scale, shared, w_in, b_gate):
    kcv, kv = shared
    bsz, seq, _ = x.shape
    q, gates = _nsa_proj(x, norm_g, shift, scale, w_in, b_gate)
    gates = gates.reshape(bsz, seq, 3, NSA_GROUPS, NSA_HPG).transpose(0, 3, 1, 2, 4)
    gates = gates.reshape(bsz, NSA_GROUPS, seq, 3 * NSA_HPG)
    return _nsa_attn(q, gates, kcv[0], kcv[1], kv)


def _final_norm_kernel(x_ref, g_ref, o_ref):
    x = x_ref[...]
    o_ref[...] = x * lax.rsqrt(jnp.mean(x * x, axis=-1, keepdims=True) + RMS_EPS) * g_ref[...]


def _final_norm(x, g):
    bsz, seq, d = x.shape
    x2d = x.reshape(bsz * seq, d)
    tm = min(ROW_TILE, seq)
    out = pl.pallas_call(
        _final_norm_kernel,
        grid=(x2d.shape[0] // tm,),
        in_specs=[pl.BlockSpec((tm, d), lambda i: (i, 0)), pl.BlockSpec((1, d), lambda i: (0, 0))],
        out_specs=pl.BlockSpec((tm, d), lambda i: (i, 0)),
        out_shape=jax.ShapeDtypeStruct(x2d.shape, F32),
        compiler_params=_params("parallel"),
        name="final_norm",
    )(x2d, g.reshape(1, d))
    return out.reshape(bsz, seq, d)


def kernel(x, c, ada_w, ada_b, norm1_g, norm2_g, gla_w_in, gla_w_gate2, gla_b_gate, gla_norm_g, gla_w_out, kv_norm_g, kv_ada_w, kv_ada_b, kv_w, cmp_pos, cmp_w1, cmp_b1, cmp_w2, cmp_b2, nsa_w_in, nsa_b_gate, nsa_w_out, router_w, router_b, moe_w_gate_up, moe_b_gate_up, moe_w_down, moe_b_down, final_g):
    bsz, seq, d = x.shape
    mod = _ada_vectors(c, ada_w, ada_b)
    kv_mod = _ada_vectors(c, kv_ada_w[None], kv_ada_b[None])[0]
    vec = lambda m, j: m[:, None, j * d:(j + 1) * d]
    shared = None
    for layer in range(DEPTH):
        m = mod[layer]
        sh1, sc1, g1, sh2, sc2, g2 = (vec(m, j) for j in range(6))
        if layer < N_A_LAYERS:
            i = layer
            q, k, v, r, la = _gla_proj(x, norm1_g[layer], sh1, sc1, gla_w_in[i], gla_w_gate2[i], gla_b_gate[i])
            o = _gla_core(q, k, v, r, la, gla_norm_g[i])
            x = _res_matmul(o, gla_w_out[i], x, g1)
        else:
            i = layer - N_A_LAYERS
            o = _nsa_mixer(x, norm1_g[layer], sh1, sc1, shared, nsa_w_in[i], nsa_b_gate[i])
            x = _res_matmul(o, nsa_w_out[i], x, g1)
        x = _moe_layer(x, norm2_g[layer], sh2, sc2, g2, router_w[layer], router_b[layer],
                       layer, moe_w_gate_up, moe_b_gate_up[layer], moe_w_down, moe_b_down[layer])
        if layer == N_A_LAYERS - 1:
            kv_cmp, kv = _kv_proj(x, kv_norm_g, vec(kv_mod, 0), vec(kv_mod, 1), kv_w)
            shared = (_compress(kv_cmp, cmp_pos, cmp_w1, cmp_b1, cmp_w2, cmp_b2), kv)
    return _final_norm(x, final_g)
```

```python
import functools

import numpy as np
import jax
import jax.numpy as jnp
from jax import lax
from jax.experimental import pallas as pl
from jax.experimental.pallas import tpu as pltpu

F32 = jnp.float32
BF16 = jnp.bfloat16
HIGHEST = lax.Precision.HIGHEST

D_MODEL = 1024
DEPTH = 4
N_A_LAYERS = DEPTH // 2
RMS_EPS = 1e-5

GLA_HEADS = 4
GLA_KEY_DIM = D_MODEL // 2
GLA_VAL_DIM = D_MODEL
GLA_HK = GLA_KEY_DIM // GLA_HEADS
GLA_HV = GLA_VAL_DIM // GLA_HEADS
GLA_RANK = 16
GLA_TAU = 16.0
GLA_CHUNK = 64

NSA_HEADS = 16
NSA_GROUPS = 4
NSA_HPG = NSA_HEADS // NSA_GROUPS
NSA_HD = D_MODEL // NSA_HEADS
CMP_LEN = 32
CMP_STRIDE = 16
CMP_HIDDEN = 2 * NSA_HD
SLC_LEN = 64
SLC_TOPK = 16
WIN = 512

N_EXPERTS = 32
TOP_K = 4
EXPERT_FF = D_MODEL
SWIGLU_LIMIT = 7.0
SWIGLU_ALPHA = 1.702

FORCE = 1e4
NEG = -1e30

VMEM_LIMIT_BYTES = 56 * 1024 * 1024

ROW_TILE = 512
EXPERT_TILE = 512
LANES = 128
ROW_CHUNK = 8
ROUTE_SPARE_ROWS = 96
ROUTE_TILE = 512
GLA_STEP = 512
SLC_KEY_TILE = 512
NSA_STEP = 256
NSA_ROW_CHUNK = 64
RANK_UNROLL = 4
WIN_PAD = WIN
MASK_BIG = 1e30


def _params(*sem):
    return pltpu.CompilerParams(dimension_semantics=sem, vmem_limit_bytes=VMEM_LIMIT_BYTES)


def _norm_mod(x, g, shift, scale):
    y = x * lax.rsqrt(jnp.mean(x * x, axis=-1, keepdims=True) + RMS_EPS)
    return (y * g) * (1.0 + scale) + shift


def _dot(a, b):
    return jnp.dot(a, b, preferred_element_type=F32)


def _dot_nt(a, b, precision=None):
    return lax.dot_general(a, b, (((1,), (1,)), ((), ())), precision=precision,
                           preferred_element_type=F32)


def _ada_kernel(c_ref, w_ref, b_ref, o_ref):
    c = c_ref[...]
    cs = c * jax.nn.sigmoid(c)
    o_ref[0] = jnp.dot(cs, w_ref[0], precision=HIGHEST, preferred_element_type=F32) + b_ref[0]


def _ada_vectors(c, w, b):
    n_l, d, m = w.shape
    bsz = c.shape[0]
    tn = 1024
    return pl.pallas_call(
        _ada_kernel,
        grid=(n_l, m // tn),
        in_specs=[
            pl.BlockSpec((bsz, d), lambda l, j: (0, 0)),
            pl.BlockSpec((1, d, tn), lambda l, j: (l, 0, j)),
            pl.BlockSpec((1, 1, tn), lambda l, j: (l, 0, j)),
        ],
        out_specs=pl.BlockSpec((1, bsz, tn), lambda l, j: (l, 0, j)),
        out_shape=jax.ShapeDtypeStruct((n_l, bsz, m), F32),
        compiler_params=_params("parallel", "parallel"),
        name="ada_vectors",
    )(c, w, b.reshape(n_l, 1, m))


def _gla_proj_kernel(x_ref, g_ref, sh_ref, sc_ref, wq_ref, wk_ref, wv_ref, wr_ref, wlr_ref,
                     wg2_ref, bg_ref, q_ref, k_ref, v_ref, r_ref, la_ref):
    h = _norm_mod(x_ref[0], g_ref[...], sh_ref[0], sc_ref[0]).astype(BF16)
    q_ref[0] = _dot(h, wq_ref[...]).astype(BF16)
    k_ref[0] = _dot(h, wk_ref[...]).astype(BF16)
    v_ref[0] = _dot(h, wv_ref[...]).astype(BF16)
    r_ref[0] = _dot(h, wr_ref[...]).astype(BF16)
    g_lr = _dot(h, wlr_ref[...])
    z = jnp.dot(g_lr, wg2_ref[...], precision=HIGHEST, preferred_element_type=F32) + bg_ref[...]
    log_sig = jnp.minimum(z, 0.0) - jnp.log(1.0 + jnp.exp(-jnp.abs(z)))
    la_ref[0] = log_sig / GLA_TAU


def _gla_proj(x, g, shift, scale, w_in, w_gate2, b_gate):
    bsz, seq, d = x.shape
    tm = min(ROW_TILE, seq)
    kd, vd = GLA_KEY_DIM, GLA_VAL_DIM
    wq = w_in[:, :kd].astype(BF16)
    wk = w_in[:, kd:2 * kd].astype(BF16)
    wv = w_in[:, 2 * kd:2 * kd + vd].astype(BF16)
    wr = w_in[:, 2 * kd + vd:2 * kd + 2 * vd].astype(BF16)
    wlr = w_in[:, 2 * kd + 2 * vd:].astype(BF16)
    full = lambda shape: pl.BlockSpec(shape, lambda b, i: (0,) * len(shape))
    row = lambda n: pl.BlockSpec((1, tm, n), lambda b, i: (b, i, 0))
    vec = pl.BlockSpec((1, 1, d), lambda b, i: (b, 0, 0))
    return pl.pallas_call(
        _gla_proj_kernel,
        grid=(bsz, seq // tm),
        in_specs=[row(d), full((1, d)), vec, vec, full((d, kd)), full((d, kd)), full((d, vd)),
                  full((d, vd)), full((d, GLA_RANK)), full((GLA_RANK, kd)), full((1, kd))],
        out_specs=[row(kd), row(kd), row(vd), row(vd), row(kd)],
        out_shape=[jax.ShapeDtypeStruct((bsz, seq, kd), BF16),
                   jax.ShapeDtypeStruct((bsz, seq, kd), BF16),
                   jax.ShapeDtypeStruct((bsz, seq, vd), BF16),
                   jax.ShapeDtypeStruct((bsz, seq, vd), BF16),
                   jax.ShapeDtypeStruct((bsz, seq, kd), F32)],
        compiler_params=_params("parallel", "parallel"),
        name="gla_proj",
    )(x, g.reshape(1, d), shift, scale, wq, wk, wv, wr, wlr, w_gate2, b_gate.reshape(1, kd))


def _gla_core_kernel(q_ref, k_ref, v_ref, r_ref, la_ref, ng_ref, o_ref, state_ref, *, n_chunks):
    @pl.when(pl.program_id(1) == 0)
    def _():
        state_ref[...] = jnp.zeros_like(state_ref)

    c_len = GLA_CHUNK
    row = lax.broadcasted_iota(jnp.int32, (c_len, c_len), 0)
    col = lax.broadcasted_iota(jnp.int32, (c_len, c_len), 1)
    causal = col <= row
    row_k = lax.broadcasted_iota(jnp.int32, (c_len, GLA_HK), 0)
    shifts = [1 << j for j in range(c_len.bit_length() - 1)]

    def chunk(c, carry):
        c0 = pl.multiple_of(c * c_len, c_len)
        for hh in range(GLA_HEADS):
            kc = slice(hh * GLA_HK, (hh + 1) * GLA_HK)
            vc = slice(hh * GLA_HV, (hh + 1) * GLA_HV)
            b = la_ref[0, pl.ds(c0, c_len), kc]
            for shift in shifts:
                b = b + jnp.where(row_k >= shift, pltpu.roll(b, shift, 0), 0.0)
            q = q_ref[0, pl.ds(c0, c_len), kc].astype(F32) * (GLA_HK ** -0.5)
            k = k_ref[0, pl.ds(c0, c_len), kc].astype(F32)
            v = v_ref[0, pl.ds(c0, c_len), vc]
            q_dec = (q * jnp.exp(b)).astype(BF16)
            k_intra = (k * jnp.exp(-b)).astype(BF16)
            b_t = b.T
            bl_t = b_t[:, c_len - 1:c_len]
            k_inter_t = (k.T * jnp.exp(bl_t - b_t)).astype(BF16)
            att = jnp.where(causal, _dot_nt(q_dec, k_intra), 0.0).astype(BF16)
            state = state_ref[hh]
            o = _dot(att, v) + _dot(q_dec, state.astype(BF16))
            state_ref[hh] = state * jnp.exp(bl_t) + _dot(k_inter_t, v)
            o = o * lax.rsqrt(jnp.mean(o * o, axis=-1, keepdims=True) + RMS_EPS)
            r = r_ref[0, pl.ds(c0, c_len), vc].astype(F32)
            o_ref[0, pl.ds(c0, c_len), vc] = ((o * ng_ref[:, vc]) * (r * jax.nn.sigmoid(r))).astype(BF16)
        return carry

    lax.fori_loop(0, n_chunks, chunk, 0, unroll=4)


def _gla_core(q, k, v, r, la, norm_g):
    bsz, seq, _ = q.shape
    ts = min(GLA_STEP, seq)
    kern = functools.partial(_gla_core_kernel, n_chunks=ts // GLA_CHUNK)
    kd = lambda: pl.BlockSpec((1, ts, GLA_KEY_DIM), lambda b, s: (b, s, 0))
    vd = lambda: pl.BlockSpec((1, ts, GLA_VAL_DIM), lambda b, s: (b, s, 0))
    return pl.pallas_call(
        kern,
        grid=(bsz, seq // ts),
        in_specs=[kd(), kd(), vd(), vd(), kd(), pl.BlockSpec((1, GLA_VAL_DIM), lambda b, s: (0, 0))],
        out_specs=vd(),
        out_shape=jax.ShapeDtypeStruct((bsz, seq, GLA_VAL_DIM), BF16),
        scratch_shapes=[pltpu.VMEM((GLA_HEADS, GLA_HK, GLA_HV), F32)],
        compiler_params=_params("parallel", "arbitrary"),
        name="gla_core",
    )(q, k, v, r, la, norm_g.reshape(1, GLA_VAL_DIM))


def _res_matmul_kernel(a_ref, w_ref, x_ref, gate_ref, o_ref):
    o_ref[0] = x_ref[0] + gate_ref[0] * _dot(a_ref[0], w_ref[...])


def _res_matmul(a, w, x, gate):
    bsz, seq, d = x.shape
    kdim = a.shape[-1]
    tm = min(ROW_TILE, seq)
    return pl.pallas_call(
        _res_matmul_kernel,
        grid=(bsz, seq // tm),
        in_specs=[pl.BlockSpec((1, tm, kdim), lambda b, i: (b, i, 0)),
                  pl.BlockSpec((kdim, d), lambda b, i: (0, 0)),
                  pl.BlockSpec((1, tm, d), lambda b, i: (b, i, 0)),
                  pl.BlockSpec((1, 1, d), lambda b, i: (b, 0, 0))],
        out_specs=pl.BlockSpec((1, tm, d), lambda b, i: (b, i, 0)),
        out_shape=jax.ShapeDtypeStruct((bsz, seq, d), F32),
        compiler_params=_params("parallel", "parallel"),
        name="res_matmul",
    )(a, w.astype(BF16), x, gate)


def _route_kernel(x_ref, g_ref, sh_ref, sc_ref, rwt_ref, rb_ref, xs_ref, w_ref, slot_ref, cnt_ref):
    tm = x_ref.shape[0]
    n_slot = xs_ref.shape[0]
    h = _norm_mod(x_ref[...], g_ref[...], sh_ref[0], sc_ref[0])
    logits = _dot_nt(rwt_ref[...], h, precision=HIGHEST) + rb_ref[...]
    e_iota = lax.broadcasted_iota(jnp.int32, logits.shape, 0)
    vals, hots = [], []
    for _ in range(TOP_K):
        m = jnp.max(logits, axis=0, keepdims=True)
        idx = jnp.min(jnp.where(logits == m, e_iota, N_EXPERTS), axis=0, keepdims=True)
        hot = e_iota == idx
        vals.append(m)
        hots.append(hot)
        logits = jnp.where(hot, -jnp.inf, logits)
    exps = [jnp.exp(v - vals[0]) for v in vals]
    denom = exps[0] + exps[1] + exps[2] + exps[3]
    for kk in range(TOP_K):
        w_ref[kk:kk + 1, :] = exps[kk] / denom

    sel = (hots[0] | hots[1] | hots[2] | hots[3]).astype(F32)
    s_iota = lax.broadcasted_iota(jnp.int32, (tm, tm), 0)
    t_iota = lax.broadcasted_iota(jnp.int32, (tm, tm), 1)
    before = (s_iota < t_iota).astype(BF16)
    prefix = _dot(sel.astype(BF16), before)
    cnt = jnp.sum(sel, axis=1, keepdims=True)
    chunks = jnp.floor((cnt + (ROW_CHUNK - 1.0)) * (1.0 / ROW_CHUNK))
    lower = (lax.broadcasted_iota(jnp.int32, (N_EXPERTS, N_EXPERTS), 0) >
             lax.broadcasted_iota(jnp.int32, (N_EXPERTS, N_EXPERTS), 1)).astype(BF16)
    seg0 = _dot(lower, jnp.broadcast_to(chunks, (N_EXPERTS, LANES)).astype(BF16))[:, 0:1] * ROW_CHUNK
    slots = [jnp.sum(jnp.where(hot, prefix + seg0, 0.0), axis=0, keepdims=True).astype(jnp.int32)
             for hot in hots]
    r_iota = lax.broadcasted_iota(jnp.int32, (n_slot, tm), 0)
    q = jnp.zeros((n_slot, tm), F32)
    for kk in range(TOP_K):
        q = q + jnp.where(r_iota == slots[kk], 1.0, 0.0)
    for kk in range(TOP_K):
        slot_ref[kk:kk + 1, :] = slots[kk]
    xs_ref[...] = _dot((q > 0.0).astype(BF16), h.astype(BF16))

    cnt_ref[0] = jnp.broadcast_to(cnt, (N_EXPERTS, LANES)).astype(jnp.int32)


def _route(x2d, g, shift, scale, router_w, router_b, seq):
    n_tok, d = x2d.shape
    tm = min(ROUTE_TILE, seq)
    per_b = seq // tm
    n_win = n_tok // tm
    n_slot = tm * TOP_K + N_EXPERTS * ROW_CHUNK + ROUTE_SPARE_ROWS
    vec = pl.BlockSpec((1, 1, d), lambda i: (i // per_b, 0, 0))
    return pl.pallas_call(
        _route_kernel,
        grid=(n_win,),
        in_specs=[pl.BlockSpec((tm, d), lambda i: (i, 0)),
                  pl.BlockSpec((1, d), lambda i: (0, 0)), vec, vec,
                  pl.BlockSpec((N_EXPERTS, d), lambda i: (0, 0)),
                  pl.BlockSpec((N_EXPERTS, 1), lambda i: (0, 0))],
        out_specs=[pl.BlockSpec((n_slot, d), lambda i: (i, 0)),
                   pl.BlockSpec((TOP_K, tm), lambda i: (0, i)),
                   pl.BlockSpec((TOP_K, tm), lambda i: (0, i)),
                   pl.BlockSpec((1, N_EXPERTS, LANES), lambda i: (i, 0, 0))],
        out_shape=[jax.ShapeDtypeStruct((n_win * n_slot, d), F32),
                   jax.ShapeDtypeStruct((TOP_K, n_tok), F32),
                   jax.ShapeDtypeStruct((TOP_K, n_tok), jnp.int32),
                   jax.ShapeDtypeStruct((n_win, N_EXPERTS, LANES), jnp.int32)],
        compiler_params=_params("parallel"),
        name="moe_route",
    )(x2d, g.reshape(1, d), shift, scale, router_w.T, router_b.reshape(N_EXPERTS, 1))


def _expert_kernel(te_ref, nu_ref, src_hbm, dst_hbm, xs_in, wgu_ref, wd_ref, bg_ref, bu_ref, bd_ref, xy_hbm,
                   wg_s, wu_s, wd_s, tr_s, xbuf, ybuf, src_s, dst_s, src_sem, dst_sem, g_sem, s_sem):
    del xs_in
    i = pl.program_id(0)
    n_tiles = pl.num_programs(0)
    tm = xbuf.shape[1]
    used = i < nu_ref[0]
    fresh = (i == 0) | (te_ref[i] != te_ref[jnp.maximum(i - 1, 0)])
    slot = i % 2
    other = 1 - slot

    def src_copy(t, sl):
        return pltpu.make_async_copy(src_hbm.at[t], src_s.at[sl], src_sem.at[sl])

    def dst_copy(t, sl):
        return pltpu.make_async_copy(dst_hbm.at[t], dst_s.at[sl], dst_sem.at[sl])

    def gather_start(sl):
        for c in range(tm // ROW_CHUNK):
            src = pl.multiple_of(src_s[sl, c] * ROW_CHUNK, ROW_CHUNK)
            pltpu.make_async_copy(xy_hbm.at[pl.ds(src, ROW_CHUNK), :],
                                  xbuf.at[sl, pl.ds(c * ROW_CHUNK, ROW_CHUNK), :], g_sem.at[sl]).start()

    def gather_wait(sl):
        pltpu.make_async_copy(xy_hbm.at[pl.ds(0, tm), :], xbuf.at[sl], g_sem.at[sl]).wait()

    def scatter_start(sl):
        for c in range(tm // ROW_CHUNK):
            dst = pl.multiple_of(dst_s[sl, c] * ROW_CHUNK, ROW_CHUNK)
            pltpu.make_async_copy(ybuf.at[sl, pl.ds(c * ROW_CHUNK, ROW_CHUNK), :],
                                  xy_hbm.at[pl.ds(dst, ROW_CHUNK), :], s_sem.at[sl]).start()

    def scatter_wait(sl):
        pltpu.make_async_copy(ybuf.at[sl], xy_hbm.at[pl.ds(0, tm), :], s_sem.at[sl]).wait()

    @pl.when(i == 0)
    def _():
        src_copy(0, 0).start()
        dst_copy(0, 0).start()
        src_copy(0, 0).wait()
        gather_start(0)
        src_copy(jnp.minimum(1, n_tiles - 1), 1).start()

    @pl.when(used & fresh)
    def _():
        n_slab, chunk, lanes = tr_s.shape
        half = chunk // 2
        for c in range(wgu_ref.shape[2] // chunk):
            t = wgu_ref[0, :, c * chunk:(c + 1) * chunk].T
            for j in range(n_slab):
                tr_s[j] = t[:, j * lanes:(j + 1) * lanes]
            for j in range(n_slab):
                wg_s[j * lanes:(j + 1) * lanes, c * half:(c + 1) * half] = (
                    tr_s[j, pl.ds(0, half, stride=2), :].T.astype(BF16))
                wu_s[j * lanes:(j + 1) * lanes, c * half:(c + 1) * half] = (
                    tr_s[j, pl.ds(1, half, stride=2), :].T.astype(BF16))
        wd_s[...] = wd_ref[0].astype(BF16)

    @pl.when(used)
    def _():
        nxt = jnp.minimum(i + 1, n_tiles - 1)
        src_copy(nxt, other).wait()
        gather_start(other)
        src_copy(jnp.minimum(i + 2, n_tiles - 1), slot).start()
        gather_wait(slot)
        xb = xbuf[slot].astype(BF16)
        gate = jnp.minimum(_dot(xb, wg_s[...]) + bg_ref[0], SWIGLU_LIMIT)
        up = jnp.clip(_dot(xb, wu_s[...]) + bu_ref[0], -SWIGLU_LIMIT, SWIGLU_LIMIT)
        act = (up + 1.0) * gate * jax.nn.sigmoid(SWIGLU_ALPHA * gate)
        ybuf[slot] = _dot(act.astype(BF16), wd_s[...]) + bd_ref[0]
        dst_copy(i, slot).wait()
        scatter_start(slot)
        dst_copy(nxt, other).start()

    @pl.when(used & (i > 0))
    def _():
        scatter_wait(other)

    @pl.when(i == nu_ref[0] - 1)
    def _():
        gather_wait(other)
        scatter_wait(slot)
        src_copy(0, slot).wait()
        dst_copy(0, other).wait()


def _experts(xs, src_tiles, dst_tiles, tile_expert, n_used, layer, w_gu, w_down, bg, bu, bd):
    d = xs.shape[1]
    n_tiles = src_tiles.shape[0]
    tm = EXPERT_TILE
    ff = w_down.shape[2]
    wsel = lambda i, te, nu: (te[i], 0, 0)
    lsel = lambda i, te, nu: (layer, te[i], 0, 0)
    hbm = pl.BlockSpec(memory_space=pl.ANY)
    return pl.pallas_call(
        _expert_kernel,
        grid_spec=pltpu.PrefetchScalarGridSpec(
            num_scalar_prefetch=2,
            grid=(n_tiles,),
            in_specs=[hbm, hbm, hbm,
                      pl.BlockSpec((None, 1, d, 2 * ff), lsel),
                      pl.BlockSpec((None, 1, ff, d), lsel),
                      pl.BlockSpec((1, 1, ff), wsel), pl.BlockSpec((1, 1, ff), wsel),
                      pl.BlockSpec((1, 1, d), wsel)],
            out_specs=hbm,
            scratch_shapes=[pltpu.VMEM((d, ff), BF16), pltpu.VMEM((d, ff), BF16),
                            pltpu.VMEM((ff, d), BF16), pltpu.VMEM((d // LANES, 2 * LANES, LANES), F32),
                            pltpu.VMEM((2, tm, d), F32), pltpu.VMEM((2, tm, d), F32),
                            pltpu.SMEM((2, tm // ROW_CHUNK), jnp.int32),
                            pltpu.SMEM((2, tm // ROW_CHUNK), jnp.int32),
                            pltpu.SemaphoreType.DMA((2,)), pltpu.SemaphoreType.DMA((2,)),
                            pltpu.SemaphoreType.DMA((2,)), pltpu.SemaphoreType.DMA((2,))]),
        out_shape=jax.ShapeDtypeStruct(xs.shape, xs.dtype),
        input_output_aliases={4: 0},
        compiler_params=_params("arbitrary"),
        name="moe_experts",
    )(tile_expert, n_used, src_tiles, dst_tiles, xs, w_gu, w_down, bg, bu, bd)


def _combine_kernel(y_ref, slot_ref, w_ref, x_ref, gate_ref, o_ref):
    tm = x_ref.shape[0]
    n_slot = y_ref.shape[0]
    r_iota = lax.broadcasted_iota(jnp.int32, (tm, n_slot), 1)
    w = w_ref[...]
    slot = slot_ref[...]
    c = jnp.zeros((tm, n_slot), F32)
    for kk in range(TOP_K):
        c = c + jnp.where(r_iota == slot[:, kk:kk + 1], w[:, kk:kk + 1], 0.0)
    y = _dot(c.astype(BF16), y_ref[...].astype(BF16))
    o_ref[...] = x_ref[...] + gate_ref[0] * y


def _combine(y, slots, w_tok, x2d, gate, seq):
    n_tok, d = x2d.shape
    tm = min(ROUTE_TILE, seq)
    per_b = seq // tm
    n_win = n_tok // tm
    n_slot = y.shape[0] // n_win
    return pl.pallas_call(
        _combine_kernel,
        grid=(n_win,),
        in_specs=[pl.BlockSpec((n_slot, d), lambda i: (i, 0)),
                  pl.BlockSpec((tm, TOP_K), lambda i: (i, 0)),
                  pl.BlockSpec((tm, TOP_K), lambda i: (i, 0)),
                  pl.BlockSpec((tm, d), lambda i: (i, 0)),
                  pl.BlockSpec((1, 1, d), lambda i: (i // per_b, 0, 0))],
        out_specs=pl.BlockSpec((tm, d), lambda i: (i, 0)),
        out_shape=jax.ShapeDtypeStruct((n_tok, d), F32),
        compiler_params=_params("parallel"),
        name="moe_combine",
    )(y, slots, w_tok, x2d, gate)


def _moe_layer(x, norm_g, shift, scale, gate, router_w, router_b, layer, w_gu, b_gu, w_down, b_down):
    bsz, seq, d = x.shape
    n_tok = bsz * seq
    x2d = x.reshape(n_tok, d)
    xs, w_t, slot_t, cnt = _route(x2d, norm_g, shift, scale, router_w, router_b, seq)

    cnt = cnt[:, :, 0]
    n_win = cnt.shape[0]
    win_chunks = xs.shape[0] // n_win // ROW_CHUNK
    tile_chunks = EXPERT_TILE // ROW_CHUNK
    c8 = (cnt + ROW_CHUNK - 1) // ROW_CHUNK
    seg_off = jnp.cumsum(c8, axis=1) - c8
    cum_w = jnp.cumsum(c8, axis=0)
    total = cum_w[-1]
    padded = (total + tile_chunks - 1) // tile_chunks * tile_chunks
    ends = jnp.cumsum(padded)
    starts = ends - padded
    n_chunk = n_tok * TOP_K // ROW_CHUNK + n_win * N_EXPERTS + N_EXPERTS * tile_chunks
    n_tiles = n_chunk // tile_chunks
    tile0 = jnp.arange(n_tiles, dtype=jnp.int32) * tile_chunks
    tile_expert = jnp.minimum(jnp.sum((ends[None, :] <= tile0[:, None]).astype(jnp.int32), axis=1),
                              N_EXPERTS - 1)
    e_hot = (tile_expert[:, None] == jnp.arange(N_EXPERTS, dtype=jnp.int32)[None, :]).astype(jnp.int32)
    pick = lambda tab: jnp.sum(e_hot[:, :, None] * tab.T[None, :, :], axis=1)
    cum_t, c8_t, off_t = pick(cum_w)[:, None, :], pick(c8)[:, None, :], pick(seg_off)[:, None, :]
    lane = jnp.arange(tile_chunks, dtype=jnp.int32)[None, :]
    rank = (tile0 - jnp.sum(e_hot * starts[None, :], axis=1))[:, None] + lane
    valid = rank < jnp.sum(e_hot * total[None, :], axis=1)[:, None]
    w_of = jnp.minimum(jnp.sum((cum_t <= rank[:, :, None]).astype(jnp.int32), axis=2), n_win - 1)
    w_hot = (w_of[:, :, None] == jnp.arange(n_win, dtype=jnp.int32)[None, None, :]).astype(jnp.int32)
    seg_first = jnp.sum(w_hot * (cum_t - c8_t), axis=2)
    src = w_of * win_chunks + jnp.sum(w_hot * off_t, axis=2) + rank - seg_first
    spare0 = win_chunks - ROUTE_SPARE_ROWS // ROW_CHUNK
    assert (ROUTE_SPARE_ROWS // ROW_CHUNK - 1) * n_win >= 2 * tile_chunks, "not enough spare chunks"
    u =(jnp.arange(n_tiles, dtype=jnp.int32) % 2)[:, None] * tile_chunks + lane
    dst = jnp.where(valid, src, (u % n_win) * win_chunks + spare0 + u // n_win)
    src = jnp.where(valid, src, win_chunks - 1)
    n_used = (ends[-1:] // tile_chunks).astype(jnp.int32)

    y = _experts(xs, src.reshape(n_tiles, tile_chunks), dst.reshape(n_tiles, tile_chunks), tile_expert,
                 n_used, layer, w_gu, w_down, b_gu[:, None, 0::2], b_gu[:, None, 1::2], b_down[:, None, :])
    out = _combine(y, slot_t.T, w_t.T, x2d, gate, seq)
    return out.reshape(bsz, seq, d)


def _kv_proj_kernel(x_ref, g_ref, sh_ref, sc_ref, w_ref, cmp_ref, kv_ref):
    h = _norm_mod(x_ref[0], g_ref[...], sh_ref[0], sc_ref[0]).astype(BF16)
    kv = _dot(h, w_ref[...])
    for t in range(6):
        for g in range(NSA_GROUPS):
            c0 = (t * NSA_GROUPS + g) * NSA_HD
            piece = kv[:, c0:c0 + NSA_HD]
            if t < 2:
                cmp_ref[t, 0, g] = piece
            else:
                kv_ref[t - 2, 0, g] = piece.astype(BF16)


def _kv_proj(x, g, shift, scale, kv_w):
    bsz, seq, d = x.shape
    tm = min(ROW_TILE, seq)
    n_out = kv_w.shape[1]
    vec = pl.BlockSpec((1, 1, d), lambda b, i: (b, 0, 0))
    return pl.pallas_call(
        _kv_proj_kernel,
        grid=(bsz, seq // tm),
        in_specs=[pl.BlockSpec((1, tm, d), lambda b, i: (b, i, 0)),
                  pl.BlockSpec((1, d), lambda b, i: (0, 0)), vec, vec,
                  pl.BlockSpec((d, n_out), lambda b, i: (0, 0))],
        out_specs=[pl.BlockSpec((2, 1, NSA_GROUPS, tm, NSA_HD), lambda b, i: (0, b, 0, i, 0)),
                   pl.BlockSpec((4, 1, NSA_GROUPS, tm, NSA_HD), lambda b, i: (0, b, 0, i, 0))],
        out_shape=[jax.ShapeDtypeStruct((2, bsz, NSA_GROUPS, seq, NSA_HD), F32),
                   jax.ShapeDtypeStruct((4, bsz, NSA_GROUPS, seq, NSA_HD), BF16)],
        compiler_params=_params("parallel", "parallel"),
        name="nsa_kv_proj",
    )(x, g.reshape(1, d), shift, scale, kv_w.astype(BF16))


def _compress_kernel(x_ref, pos_ref, w1_ref, b1_ref, w2_ref, b2_ref, o_ref, *, n_rows):
    for g in range(NSA_GROUPS):
        first = jnp.zeros((n_rows, CMP_HIDDEN), F32)
        second = jnp.zeros((n_rows, CMP_HIDDEN), F32)
        for l in range(CMP_STRIDE):
            rows = x_ref[0, 0, g, pl.ds(l, n_rows, stride=CMP_STRIDE), :]
            first = first + _dot((rows + pos_ref[0, l:l + 1, :]).astype(BF16), w1_ref[0, l])
            second = second + _dot((rows + pos_ref[0, CMP_STRIDE + l:CMP_STRIDE + l + 1, :]).astype(BF16),
                                   w1_ref[0, CMP_STRIDE + l])
        pre = first + pltpu.roll(second, n_rows - 1, 0) + b1_ref[0]
        hid = 0.5 * pre * (1.0 + jnp.tanh(0.7978845608028654 * (pre + 0.044715 * pre * pre * pre)))
        o_ref[0, 0, g] = (_dot(hid.astype(BF16), w2_ref[0]) + b2_ref[0]).astype(BF16)


def _compress(kv_cmp, cmp_pos, cmp_w1, cmp_b1, cmp_w2, cmp_b2):
    _, bsz, _, seq, _ = kv_cmp.shape
    n_rows = seq // CMP_STRIDE
    kern = functools.partial(_compress_kernel, n_rows=n_rows)
    w1 = cmp_w1.reshape(2, CMP_LEN, NSA_HD, CMP_HIDDEN).astype(BF16)
    return pl.pallas_call(
        kern,
        grid=(2, bsz),
        in_specs=[pl.BlockSpec((1, 1, NSA_GROUPS, seq, NSA_HD), lambda t, b: (t, b, 0, 0, 0)),
                  pl.BlockSpec((1, CMP_LEN, NSA_HD), lambda t, b: (t, 0, 0)),
                  pl.BlockSpec((1, CMP_LEN, NSA_HD, CMP_HIDDEN), lambda t, b: (t, 0, 0, 0)),
                  pl.BlockSpec((1, 1, CMP_HIDDEN), lambda t, b: (t, 0, 0)),
                  pl.BlockSpec((1, CMP_HIDDEN, NSA_HD), lambda t, b: (t, 0, 0)),
                  pl.BlockSpec((1, 1, NSA_HD), lambda t, b: (t, 0, 0))],
        out_specs=pl.BlockSpec((1, 1, NSA_GROUPS, n_rows, NSA_HD), lambda t, b: (t, b, 0, 0, 0)),
        out_shape=jax.ShapeDtypeStruct((2, bsz, NSA_GROUPS, n_rows, NSA_HD), BF16),
        compiler_params=_params("parallel", "parallel"),
        name="nsa_compress",
    )(kv_cmp, cmp_pos, w1, cmp_b1[:, None, :], cmp_w2.astype(BF16), cmp_b2[:, None, :])


def _nsa_proj_kernel(x_ref, g_ref, sh_ref, sc_ref, wq_ref, wg_ref, bg_ref, q_ref, gate_ref):
    h = _norm_mod(x_ref[0], g_ref[...], sh_ref[0], sc_ref[0]).astype(BF16)
    q_ref[0] = (_dot(h, wq_ref[...]) * (NSA_HD ** -0.5)).astype(BF16)
    gate_ref[0] = jax.nn.sigmoid(_dot(h, wg_ref[...]) + bg_ref[...])


def _nsa_proj(x, g, shift, scale, w_in, b_gate):
    bsz, seq, d = x.shape
    tm = min(ROW_TILE, seq)
    nq = NSA_HEADS * NSA_HD
    ng = 3 * NSA_HEADS
    vec = pl.BlockSpec((1, 1, d), lambda b, i: (b, 0, 0))
    return pl.pallas_call(
        _nsa_proj_kernel,
        grid=(bsz, seq // tm),
        in_specs=[pl.BlockSpec((1, tm, d), lambda b, i: (b, i, 0)),
                  pl.BlockSpec((1, d), lambda b, i: (0, 0)), vec, vec,
                  pl.BlockSpec((d, nq), lambda b, i: (0, 0)),
                  pl.BlockSpec((d, ng), lambda b, i: (0, 0)),
                  pl.BlockSpec((1, ng), lambda b, i: (0, 0))],
        out_specs=[pl.BlockSpec((1, tm, nq), lambda b, i: (b, i, 0)),
                   pl.BlockSpec((1, tm, ng), lambda b, i: (b, i, 0))],
        out_shape=[jax.ShapeDtypeStruct((bsz, seq, nq), BF16),
                   jax.ShapeDtypeStruct((bsz, seq, ng), F32)],
        compiler_params=_params("parallel", "parallel"),
        name="nsa_proj",
    )(x, g.reshape(1, d), shift, scale, w_in[:, :nq].astype(BF16), w_in[:, nq:].astype(BF16),
      b_gate.reshape(1, ng))


def _nsa_attn_kernel(q_ref, gate_ref, qc_ref, slope_ref, ovl_ref, kconst_ref, cw_ref, kc_ref, vc_ref,
                     ks_ref, vs_ref, kw_ref, vw_ref, o_ref, ksa, vsa, kwa, vwa, score_s,
                     sc_s, pc_s, psum_s, qa_s, m_s, acc_s, sa_s, sb_s, ps_s, sw_s, pw_s, ow_s):
    qi = pl.program_id(2)
    qs = NSA_STEP
    hd = NSA_HD
    rows = NSA_HPG * qs
    seq = ks_ref.shape[2]
    q0 = pl.multiple_of(qi * qs, qs)

    @pl.when(qi == 0)
    def _():
        ksa[...] = kconst_ref[...]
        ksa[:, 0:hd] = ks_ref[0, 0]
        one_col = (lax.broadcasted_iota(jnp.int32, (seq, 128), 1) == hd).astype(BF16)
        vsa[...] = one_col
        vsa[:, 0:hd] = vs_ref[0, 0]
        flag_col = (lax.broadcasted_iota(jnp.int32, (WIN_PAD, 128), 1) == hd).astype(BF16)
        kwa[0:WIN_PAD, :] = flag_col
        kwa[WIN_PAD:, :] = jnp.zeros((seq, 128), BF16)
        kwa[WIN_PAD:, 0:hd] = kw_ref[0, 0]
        vwa[0:WIN_PAD, :] = jnp.zeros((WIN_PAD, 128), BF16)
        vwa[WIN_PAD:, :] = one_col
        vwa[WIN_PAD:, 0:hd] = vw_ref[0, 0]
        qa_s[:, 2 * hd:] = qc_ref[0]

    qt = q_ref[0].astype(F32)
    q32 = jnp.concatenate([qt[:, hh * hd:(hh + 1) * hd] for hh in range(NSA_HPG)], axis=0)
    q = q32.astype(BF16)
    slope = slope_ref[0]
    t_row = q0 + lax.broadcasted_iota(jnp.int32, (rows, 1), 0) % qs

    rc = NSA_ROW_CHUNK

    nw = WIN_PAD + qs
    q_w = jnp.concatenate([q32, jnp.full((rows, hd), -MASK_BIG, F32)], axis=1).astype(BF16)
    sw_s[...] = _dot_nt(q_w, kwa[pl.ds(q0, nw), :])
    rw = rc // 2
    for c in range(rows // rw):
        r = slice(c * rw, (c + 1) * rw)
        s_w = sw_s[r, :] + cw_ref[0, r, :]
        m_w = jnp.max(s_w, axis=-1, keepdims=True)
        pw_s[r, :] = jnp.exp(s_w - m_w).astype(BF16)
    acc_w = _dot(pw_s[...], vwa[pl.ds(q0, nw), :])
    ow_s[...] = acc_w[:, 0:hd] * (1.0 / acc_w[:, hd:hd + 1])

    n_cmp = kc_ref.shape[2]
    cmp_end = lax.broadcasted_iota(jnp.int32, (1, n_cmp), 1) * CMP_STRIDE + (CMP_LEN - 1)
    cmp_bias = cmp_end.astype(F32)
    sc_s[...] = _dot_nt(q, kc_ref[0, 0])
    for c in range(rows // rc):
        r = slice(c * rc, (c + 1) * rc)
        ok_c = cmp_end <= t_row[r]
        s_c = jnp.where(ok_c, sc_s[r, :] + slope[r] * cmp_bias, NEG)
        m_c = jnp.max(s_c, axis=-1, keepdims=True)
        e_c = jnp.where(ok_c, jnp.exp(s_c - m_c), 0.0)
        l_c = jnp.sum(e_c, axis=-1, keepdims=True)
        p_c = e_c * (1.0 / jnp.where(l_c > 0.0, l_c, 1.0))
        pc_s[r, :] = p_c.astype(BF16)
        qr = slice((c * rc) % qs, (c * rc) % qs + rc)
        if c * rc < qs:
            psum_s[qr, :] = p_c
        else:
            psum_s[qr, :] = psum_s[qr, :] + p_c
    o_c = _dot(pc_s[...], vc_ref[0, 0])

    imp_t = _dot_nt(ovl_ref[...], psum_s[...], precision=HIGHEST)
    n_slc = imp_t.shape[0]
    blk = lax.broadcasted_iota(jnp.int32, (n_slc, qs), 0)
    cur = (q0 + lax.broadcasted_iota(jnp.int32, (1, qs), 1)) // SLC_LEN
    forced = (blk == 0) | (blk == cur) | (blk == cur - 1)
    score = jnp.where(blk <= cur, imp_t + jnp.where(forced, FORCE, 0.0), -FORCE)
    score_s[...] = score

    def rank_step(ip, rank):
        for u in range(RANK_UNROLL):
            i = RANK_UNROLL * ip + u
            other = score_s[pl.ds(i, 1), :]
            ahead = (other > score) | ((other == score) & (blk > i))
            rank = rank + ahead.astype(F32)
        return rank

    last_blk = (q0 + qs - 1) // SLC_LEN
    n_rank = jnp.where(last_blk >= SLC_TOPK, last_blk // RANK_UNROLL + 1, 0)
    rank = lax.fori_loop(0, n_rank, rank_step, jnp.zeros((n_slc, qs), F32))
    sel_t = (rank < float(min(SLC_TOPK, n_slc))).astype(BF16)
    eye = (lax.broadcasted_iota(jnp.int32, (qs, qs), 0) ==
           lax.broadcasted_iota(jnp.int32, (qs, qs), 1)).astype(BF16)
    sel = _dot_nt(eye, sel_t)
    drop = (sel - 1.0) * MASK_BIG
    if n_slc < hd:
        drop = jnp.concatenate([drop, jnp.zeros((qs, hd - n_slc), F32)], axis=1)
    qa_s[:, 0:2 * hd] = jnp.concatenate([q32, jnp.concatenate([drop] * NSA_HPG, axis=0)], axis=1).astype(BF16)

    tk = SLC_KEY_TILE

    n_full = q0 // tk
    m_s[...] = jnp.full((rows, 1), -MASK_BIG, F32)
    acc_s[...] = jnp.zeros((rows, 128), F32)

    def scores(kt, s_out):
        k0 = pl.multiple_of(kt * tk, tk)
        s_out[...] = _dot_nt(qa_s[...], ksa[pl.ds(k0, tk), :])

    def absorb(s_in, kt, causal):
        k0 = pl.multiple_of(kt * tk, tk)
        kpos = k0 + lax.broadcasted_iota(jnp.int32, (1, tk), 1)
        for c in range(rows // rc):
            r = slice(c * rc, (c + 1) * rc)
            s = s_in[r, :]
            if causal:
                s = jnp.where(kpos <= t_row[r], s, -MASK_BIG)
            m_old = m_s[r, :]
            m_new = jnp.maximum(m_old, jnp.max(s, axis=-1, keepdims=True))
            m_s[r, :] = m_new
            ps_s[r, :] = jnp.exp(s - m_new).astype(BF16)
            acc_s[r, :] = jnp.exp(m_old - m_new) * acc_s[r, :]
        acc_s[...] = acc_s[...] + _dot(ps_s[...], vsa[pl.ds(k0, tk), :])

    scores(0, sa_s)

    def slc_pair(j, carry):
        scores(2 * j + 1, sb_s)
        absorb(sa_s, 2 * j, False)

        @pl.when(2 * j + 1 < n_full)
        def _():
            scores(2 * j + 2, sa_s)
            absorb(sb_s, 2 * j + 1, False)
        return carry

    lax.fori_loop(0, (n_full + 1) // 2, slc_pair, 0)

    @pl.when(n_full % 2 == 0)
    def _():
        absorb(sa_s, n_full, True)

    @pl.when(n_full % 2 == 1)
    def _():
        absorb(sb_s, n_full, True)

    o_s = acc_s[:, 0:hd] * (1.0 / acc_s[:, hd:hd + 1])

    gates = gate_ref[0, 0]
    outs = []
    for hh in range(NSA_HPG):
        r0 = hh * qs
        outs.append(gates[:, hh:hh + 1] * o_c[r0:r0 + qs]
                    + gates[:, NSA_HPG + hh:NSA_HPG + hh + 1] * o_s[r0:r0 + qs]
                    + gates[:, 2 * NSA_HPG + hh:2 * NSA_HPG + hh + 1] * ow_s[r0:r0 + qs, :])
    o_ref[0] = jnp.concatenate(outs, axis=1).astype(BF16)


def _bf16_pieces(x):
    x = np.asarray(x, np.float32)
    out = []
    for _ in range(3):
        p = x.astype(BF16).astype(np.float32)
        out.append(p)
        x = x - p
    return out


def _nsa_constants(seq):
    qb, hd = NSA_STEP, NSA_HD
    n_cmp_rows = seq // CMP_STRIDE
    n_slc = seq // SLC_LEN
    cmp_start = np.arange(n_cmp_rows) * CMP_STRIDE
    cmp_end = cmp_start + CMP_LEN - 1
    slc_start = np.arange(n_slc) * SLC_LEN
    overlap = ((cmp_start[:, None] <= slc_start[None, :] + SLC_LEN - 1)
               & (cmp_end[:, None] >= slc_start[None, :])).astype(np.float32)
    start = 2.0 ** (-8.0 / NSA_HEADS)
    slopes = np.asarray(start ** np.arange(1, NSA_HEADS + 1), np.float32).reshape(NSA_GROUPS, NSA_HPG)
    slope_rows = np.repeat(slopes, qb, axis=1)

    pos = np.arange(seq)
    kconst = np.zeros((seq, 256), np.float32)
    kconst[pos, hd + pos // SLC_LEN] = 1.0
    kconst[:, 2 * hd + 0:2 * hd + 3] = (pos // 64 * 64)[:, None]
    kconst[:, 2 * hd + 3:2 * hd + 6] = (pos % 64)[:, None]
    qconst = np.zeros((NSA_GROUPS, NSA_HPG * qb, 128), np.float32)
    for j, piece in enumerate(_bf16_pieces(slope_rows)):
        qconst[:, :, j] = piece
        qconst[:, :, 3 + j] = piece
    dist = (np.arange(NSA_HPG * qb) % qb)[:, None] + WIN_PAD - np.arange(WIN_PAD + qb)[None, :]
    cw = np.where((dist >= 0) & (dist < WIN), -slope_rows[:, :, None] * dist[None].astype(np.float32),
                  -MASK_BIG).astype(np.float32)
    return (jnp.asarray(overlap.T), jnp.asarray(kconst, BF16), jnp.asarray(qconst, BF16),
            jnp.asarray(slope_rows[:, :, None]), jnp.asarray(cw))


def _nsa_attn(q, gates, kc, vc, kv):
    bsz, seq, _ = q.shape
    qb = NSA_STEP
    rows = NSA_HPG * qb
    gw = NSA_HPG * NSA_HD
    n_cmp_rows = seq // CMP_STRIDE
    n_slc = seq // SLC_LEN
    nw = WIN_PAD + qb
    ovl_t, kconst, qconst, slope_rows, cw = _nsa_constants(seq)
    cmp_spec = lambda: pl.BlockSpec((1, 1, n_cmp_rows, NSA_HD), lambda b, g, i: (b, g, 0, 0))
    kv_spec = lambda t: pl.BlockSpec((None, 1, 1, seq, NSA_HD), lambda b, g, i, t=t: (t, b, g, 0, 0))
    return pl.pallas_call(
        _nsa_attn_kernel,
        grid=(bsz, NSA_GROUPS, seq // qb),
        in_specs=[pl.BlockSpec((1, qb, gw), lambda b, g, i: (b, i, g)),
                  pl.BlockSpec((1, 1, qb, 3 * NSA_HPG), lambda b, g, i: (b, g, i, 0)),
                  pl.BlockSpec((1, rows, 128), lambda b, g, i: (g, 0, 0)),
                  pl.BlockSpec((1, rows, 1), lambda b, g, i: (g, 0, 0)),
                  pl.BlockSpec((n_slc, n_cmp_rows), lambda b, g, i: (0, 0)),
                  pl.BlockSpec((seq, 256), lambda b, g, i: (0, 0)),
                  pl.BlockSpec((1, rows, nw), lambda b, g, i: (g, 0, 0)),
                  cmp_spec(), cmp_spec(), kv_spec(0), kv_spec(1), kv_spec(2), kv_spec(3)],
        out_specs=pl.BlockSpec((1, qb, gw), lambda b, g, i: (b, i, g)),
        out_shape=jax.ShapeDtypeStruct((bsz, seq, NSA_HEADS * NSA_HD), BF16),
        scratch_shapes=[pltpu.VMEM((seq, 256), BF16), pltpu.VMEM((seq, 128), BF16),
                        pltpu.VMEM((WIN_PAD + seq, 128), BF16), pltpu.VMEM((WIN_PAD + seq, 128), BF16),
                        pltpu.VMEM((n_slc, qb), F32),
                        pltpu.VMEM((rows, n_cmp_rows), F32), pltpu.VMEM((rows, n_cmp_rows), BF16),
                        pltpu.VMEM((qb, n_cmp_rows), F32), pltpu.VMEM((rows, 256), BF16),
                        pltpu.VMEM((rows, 1), F32), pltpu.VMEM((rows, 128), F32),
                        pltpu.VMEM((rows, SLC_KEY_TILE), F32), pltpu.VMEM((rows, SLC_KEY_TILE), F32),
                        pltpu.VMEM((rows, SLC_KEY_TILE), BF16),
                        pltpu.VMEM((rows, nw), F32), pltpu.VMEM((rows, nw), BF16),
                        pltpu.VMEM((rows, NSA_HD), F32)],
        compiler_params=_params("parallel", "parallel", "arbitrary"),
        name="nsa_attn",
    )(q, gates, qconst, slope_rows, ovl_t, kconst, cw, kc, vc, kv, kv, kv, kv)


def _nsa_mixer(x, norm_g, shift, scale, shared, w_in, b_gate):
    kcv, kv = shared
    bsz, seq, _ = x.shape
    q, gates = _nsa_proj(x, norm_g, shift, scale, w_in, b_gate)
    gates = gates.reshape(bsz, seq, 3, NSA_GROUPS, NSA_HPG).transpose(0, 3, 1, 2, 4)
    gates = gates.reshape(bsz, NSA_GROUPS, seq, 3 * NSA_HPG)
    return _nsa_attn(q, gates, kcv[0], kcv[1], kv)


def _final_norm_kernel(x_ref, g_ref, o_ref):
    x = x_ref[...]
    o_ref[...] = x * lax.rsqrt(jnp.mean(x * x, axis=-1, keepdims=True) + RMS_EPS) * g_ref[...]


def _final_norm(x, g):
    bsz, seq, d = x.shape
    x2d = x.reshape(bsz * seq, d)
    tm = min(ROW_TILE, seq)
    out = pl.pallas_call(
        _final_norm_kernel,
        grid=(x2d.shape[0] // tm,),
        in_specs=[pl.BlockSpec((tm, d), lambda i: (i, 0)), pl.BlockSpec((1, d), lambda i: (0, 0))],
        out_specs=pl.BlockSpec((tm, d), lambda i: (i, 0)),
        out_shape=jax.ShapeDtypeStruct(x2d.shape, F32),
        compiler_params=_params("parallel"),
        name="final_norm",
    )(x2d, g.reshape(1, d))
    return out.reshape(bsz, seq, d)


def kernel(x, c, ada_w, ada_b, norm1_g, norm2_g, gla_w_in, gla_w_gate2, gla_b_gate, gla_norm_g, gla_w_out, kv_norm_g, kv_ada_w, kv_ada_b, kv_w, cmp_pos, cmp_w1, cmp_b1, cmp_w2, cmp_b2, nsa_w_in, nsa_b_gate, nsa_w_out, router_w, router_b, moe_w_gate_up, moe_b_gate_up, moe_w_down, moe_b_down, final_g):
    bsz, seq, d = x.shape
    mod = _ada_vectors(c, ada_w, ada_b)
    kv_mod = _ada_vectors(c, kv_ada_w[None], kv_ada_b[None])[0]
    vec = lambda m, j: m[:, None, j * d:(j + 1) * d]
    shared = None
    for layer in range(DEPTH):
        m = mod[layer]
        sh1, sc1, g1, sh2, sc2, g2 = (vec(m, j) for j in range(6))
        if layer < N_A_LAYERS:
            i = layer
            q, k, v, r, la = _gla_proj(x, norm1_g[layer], sh1, sc1, gla_w_in[i], gla_w_gate2[i], gla_b_gate[i])
            o = _gla_core(q, k, v, r, la, gla_norm_g[i])
            x = _res_matmul(o, gla_w_out[i], x, g1)
        else:
            i = layer - N_A_LAYERS
            o = _nsa_mixer(x, norm1_g[layer], sh1, sc1, shared, nsa_w_in[i], nsa_b_gate[i])
            x = _res_matmul(o, nsa_w_out[i], x, g1)
        x = _moe_layer(x, norm2_g[layer], sh2, sc2, g2, router_w[layer], router_b[layer],
                       layer, moe_w_gate_up, moe_b_gate_up[layer], moe_w_down, moe_b_down[layer])
        if layer == N_A_LAYERS - 1:
            kv_cmp, kv = _kv_proj(x, kv_norm_g, vec(kv_mod, 0), vec(kv_mod, 1), kv_w)
            shared = (_compress(kv_cmp, cmp_pos, cmp_w1, cmp_b1, cmp_w2, cmp_b2), kv)
    return _final_norm(x, final_g)
```
